```python
import jax, jax.numpy as jnp
from jax import lax
import numpy as np

D_MODEL = 1024
BATCH = 4
SEQ = 4096
DEPTH = 1

MEM_LEN = 256
N_HEADS_A = 8
HEAD_DIM_A = 64
WIDTH_A = N_HEADS_A * HEAD_DIM_A
N_IDX_HEADS = 8
IDX_DIM = 64
TOPK_MAX = 256
Q_BLOCK = 128
WIDTH_B = 512
CONV_WIDTH = 3
N_HEADS_M = 4
HEAD_DIM_M = 128
WIDTH_M = N_HEADS_M * HEAD_DIM_M
N_BRANCHES = 3
RMS_EPS = 1e-6

IN_SIZES = (WIDTH_A, WIDTH_A, WIDTH_A, WIDTH_A,
            N_IDX_HEADS * IDX_DIM, IDX_DIM, N_IDX_HEADS,
            WIDTH_B, WIDTH_B, WIDTH_B, WIDTH_B,
            WIDTH_M, WIDTH_M,
            N_BRANCHES * D_MODEL)
IN_SPLIT_POINTS = tuple(int(v) for v in np.cumsum(IN_SIZES)[:-1])
D_IN = int(sum(IN_SIZES))

kernel_name = "hybrid_dsa_shortconv_memxattn_gated_merge"


def rmsnorm(x, g):
    xf = x.astype(jnp.float32)
    y = xf * lax.rsqrt(jnp.mean(xf * xf, axis=-1, keepdims=True) + RMS_EPS)
    return (y * g.astype(jnp.float32)).astype(x.dtype)


def alibi_slopes(n_heads):
    return 2.0 ** (-8.0 * jnp.arange(1, n_heads + 1, dtype=jnp.float32) / n_heads)


def dsa_attention(q, k, v, q_idx, k_idx, w_idx):
    B, S, H, Dh = q.shape
    topk = min(TOPK_MAX, S // 4)
    n_blk = S // Q_BLOCK
    slopes = alibi_slopes(H)
    key_pos = jnp.arange(S)
    scale = HEAD_DIM_A ** -0.5
    idx_scale = (IDX_DIM ** -0.5) * (N_IDX_HEADS ** -0.5)
    k_idx_f = k_idx.astype(jnp.float32)

    def block(i):
        start = i * Q_BLOCK
        qb = lax.dynamic_slice_in_dim(q, start, Q_BLOCK, axis=1)
        qib = lax.dynamic_slice_in_dim(q_idx, start, Q_BLOCK, axis=1)
        wb = lax.dynamic_slice_in_dim(w_idx, start, Q_BLOCK, axis=1)
        t = start + jnp.arange(Q_BLOCK)
        dots = jnp.einsum('bthd,bsd->bths', qib.astype(jnp.float32), k_idx_f)
        score = jnp.einsum('bths,bth->bts', jax.nn.relu(dots), wb.astype(jnp.float32)) * idx_scale
        causal = key_pos[None, :] <= t[:, None]
        score = jnp.where(causal[None], score, -jnp.inf)
        _, sel = lax.top_k(score, topk)
        k_sel = jax.vmap(lambda kb, ib: kb[ib])(k, sel)
        v_sel = jax.vmap(lambda vb, ib: vb[ib])(v, sel)
        logits = jnp.einsum('bthd,btkhd->bhtk', qb, k_sel).astype(jnp.float32) * scale
        dist = (t[None, :, None] - sel).astype(jnp.float32)
        logits = logits - slopes[None, :, None, None] * dist[:, None]
        valid = sel <= t[None, :, None]
        logits = jnp.where(valid[:, None], logits, -jnp.inf)
        p = jax.nn.softmax(logits, axis=-1).astype(v.dtype)
        return jnp.einsum('bhtk,btkhd->bthd', p, v_sel)

    out = lax.map(block, jnp.arange(n_blk))
    return out.transpose(1, 0, 2, 3, 4).reshape(B, S, H * Dh)


def memory_attention(q, mk, mv):
    B, S = q.shape[:2]
    logits = jnp.einsum('bthd,bmhd->bhtm', q, mk).astype(jnp.float32) * (HEAD_DIM_M ** -0.5)
    p = jax.nn.softmax(logits, axis=-1).astype(mv.dtype)
    return jnp.einsum('bhtm,bmhd->bthd', p, mv).reshape(B, S, WIDTH_M)


def short_conv(u, w):
    S = u.shape[1]
    up = jnp.pad(u, ((0, 0), (CONV_WIDTH - 1, 0), (0, 0)))
    y = w[0] * up[:, 0:S]
    for j in range(1, CONV_WIDTH):
        y = y + w[j] * up[:, j:j + S]
    return y


def setup_inputs(seed: int = 0) -> dict:
    key = jax.random.key(seed)
    ks = jax.random.split(key, 16)
    f32 = jnp.float32
    nrm = lambda k, shape, s: jax.random.normal(k, shape, f32) * s
    return {
        "x": nrm(ks[0], (BATCH, SEQ, D_MODEL), 1.0),
        "mem": nrm(ks[1], (BATCH, MEM_LEN, D_MODEL), 1.0),
        "norm_g": 1.0 + nrm(ks[2], (DEPTH, D_MODEL), 0.02),
        "mem_norm_g": 1.0 + nrm(ks[3], (DEPTH, D_MODEL), 0.02),
        "w_in": nrm(ks[4], (DEPTH, D_MODEL, D_IN), D_MODEL ** -0.5),
        "b_gate": nrm(ks[5], (DEPTH, N_BRANCHES * D_MODEL), 0.01),
        "w_mem_kv": nrm(ks[6], (DEPTH, D_MODEL, 2 * WIDTH_M), D_MODEL ** -0.5),
        "q_norm_a": 1.0 + nrm(ks[7], (DEPTH, HEAD_DIM_A), 0.02),
        "k_norm_a": 1.0 + nrm(ks[8], (DEPTH, HEAD_DIM_A), 0.02),
        "q_norm_m": 1.0 + nrm(ks[9], (DEPTH, HEAD_DIM_M), 0.02),
        "k_norm_m": 1.0 + nrm(ks[10], (DEPTH, HEAD_DIM_M), 0.02),
        "conv_w": nrm(ks[11], (DEPTH, CONV_WIDTH, WIDTH_B), CONV_WIDTH ** -0.5),
        "w_out_a": nrm(ks[12], (DEPTH, WIDTH_A, D_MODEL), WIDTH_A ** -0.5),
        "w_out_b": nrm(ks[13], (DEPTH, WIDTH_B, D_MODEL), WIDTH_B ** -0.5),
        "w_out_m": nrm(ks[14], (DEPTH, WIDTH_M, D_MODEL), WIDTH_M ** -0.5),
        "w_o": nrm(ks[15], (DEPTH, D_MODEL, D_MODEL), D_MODEL ** -0.5),
    }


def reference(x, mem, norm_g, mem_norm_g, w_in, b_gate, w_mem_kv, q_norm_a, k_norm_a,
              q_norm_m, k_norm_m, conv_w, w_out_a, w_out_b, w_out_m, w_o):
    B, S, D = x.shape
    M = mem.shape[1]
    h = x
    for l in range(DEPTH):
        xn = rmsnorm(h, norm_g[l])
        proj = xn @ w_in[l]
        (qa, ka, va, za, qi, ki, wi, bg, cg, hb, zb, qm, zm, g) = jnp.split(
            proj, IN_SPLIT_POINTS, axis=-1)

        qa = rmsnorm(qa.reshape(B, S, N_HEADS_A, HEAD_DIM_A), q_norm_a[l])
        ka = rmsnorm(ka.reshape(B, S, N_HEADS_A, HEAD_DIM_A), k_norm_a[l])
        va = va.reshape(B, S, N_HEADS_A, HEAD_DIM_A)
        qi = qi.reshape(B, S, N_IDX_HEADS, IDX_DIM)
        attn_a = dsa_attention(qa, ka, va, qi, ki, wi)
        ya = (attn_a * jax.nn.silu(za)) @ w_out_a[l]

        conv_out = bg * short_conv(cg * hb, conv_w[l])
        yb = (conv_out * jax.nn.silu(zb)) @ w_out_b[l]

        memn = rmsnorm(mem, mem_norm_g[l])
        mk, mv = jnp.split(memn @ w_mem_kv[l], 2, axis=-1)
        mk = rmsnorm(mk.reshape(B, M, N_HEADS_M, HEAD_DIM_M), k_norm_m[l])
        mv = mv.reshape(B, M, N_HEADS_M, HEAD_DIM_M)
        qm = rmsnorm(qm.reshape(B, S, N_HEADS_M, HEAD_DIM_M), q_norm_m[l])
        attn_m = memory_attention(qm, mk, mv)
        ym = (attn_m * jax.nn.silu(zm)) @ w_out_m[l]

        gates = jax.nn.sigmoid((g + b_gate[l]).astype(jnp.float32)).astype(h.dtype)
        gates = gates.reshape(B, S, N_BRANCHES, D)
        merged = gates[:, :, 0] * ya + gates[:, :, 1] * yb + gates[:, :, 2] * ym
        h = h + merged @ w_o[l]
    return h
```

```python
import functools

import jax
import jax.numpy as jnp
import numpy as np
from jax import lax
from jax.experimental import pallas as pl
from jax.experimental.pallas import tpu as pltpu

F32 = jnp.float32
BF16 = jnp.bfloat16

D_MODEL = 1024
N_HEADS_A = 8
HEAD_DIM_A = 64
WIDTH_A = 512
N_IDX_HEADS = 8
IDX_DIM = 64
TOPK_MAX = 256
WIDTH_B = 512
CONV_WIDTH = 3
N_HEADS_M = 4
HEAD_DIM_M = 128
WIDTH_M = 512
N_BRANCHES = 3
RMS_EPS = 1e-6

_C_QA, _C_KA, _C_VA, _C_ZA = 0, 512, 1024, 1536
_C_QI, _C_KI, _C_WI = 2048, 2560, 2624
_C_REST = 2632
_D_IN = 8776

VMEM_LIMIT_BYTES = 56 * 1024 * 1024

TM_A = 512
TM_C = 256
TQ = 256
CH0 = 256
CH3 = 512
BISECT_STEPS = 24


def _rms_rows(xf, eps=RMS_EPS):
    return xf * lax.rsqrt(jnp.mean(xf * xf, axis=-1, keepdims=True) + eps)


def _dot(a, b):
    return jnp.dot(a, b, preferred_element_type=F32)


def _dot_nt(a, b):
    return lax.dot_general(a, b, (((1,), (1,)), ((), ())), preferred_element_type=F32)


def _proj_a_kernel(x_ref, g_ref, w1_ref, w2t_ref, gsum_ref, gk_ref, gq_ref,
                   qat_ref, ka_ref, vat_ref, qit_ref, kib_ref, wt_ref):
    tm = x_ref.shape[0]
    xn = (_rms_rows(x_ref[...]) * g_ref[...]).astype(BF16)

    y1 = _dot(xn, w1_ref[...])
    ka_raw = y1[:, :WIDTH_A]
    sq = ka_raw * ka_raw
    sq_hi = sq.astype(BF16)
    sq_lo = (sq - sq_hi.astype(F32)).astype(BF16)
    ss = _dot(sq_hi, gsum_ref[...]) + _dot(sq_lo, gsum_ref[...])
    ka = ka_raw * lax.rsqrt(ss * (1.0 / HEAD_DIM_A) + RMS_EPS) * gk_ref[...]
    ka_ref[...] = ka.astype(BF16)
    kiwi = y1[:, WIDTH_A:]
    lane = lax.broadcasted_iota(jnp.int32, kiwi.shape, 1)
    kib_ref[...] = jnp.where(lane < IDX_DIM, kiwi, 0.0).astype(BF16)
    wt_ref[...] = kiwi.T[IDX_DIM:IDX_DIM + N_IDX_HEADS, :]

    yt = _dot_nt(w2t_ref[...], xn)
    gq = jnp.concatenate([gq_ref[...]] * (tm // 128), axis=1)
    for h in range(N_HEADS_A):
        qh = yt[h * HEAD_DIM_A:(h + 1) * HEAD_DIM_A, :]
        ms = jnp.mean(qh * qh, axis=0, keepdims=True)
        qn = qh * lax.rsqrt(ms + RMS_EPS) * gq
        qat_ref[h * HEAD_DIM_A:(h + 1) * HEAD_DIM_A, :] = (qn * (HEAD_DIM_A ** -0.5)).astype(BF16)
    vat_ref[...] = yt[WIDTH_A:2 * WIDTH_A, :].astype(BF16)
    qit_ref[...] = yt[2 * WIDTH_A:3 * WIDTH_A, :].astype(BF16)


def _proj_a(x2, norm_g, w_in, q_norm_a, k_norm_a, batch, seq):
    n = x2.shape[0]
    tm = min(TM_A, seq)
    per_b = seq // tm
    w1 = jnp.concatenate(
        [w_in[:, _C_KA:_C_KA + 512], w_in[:, _C_KI:_C_KI + 72],
         jnp.zeros((D_MODEL, 56), F32)], axis=1).astype(BF16)
    w2t = jnp.concatenate(
        [w_in[:, _C_QA:_C_QA + 512], w_in[:, _C_VA:_C_VA + 512],
         w_in[:, _C_QI:_C_QI + 512]], axis=1).T.astype(BF16)
    hid = np.arange(WIDTH_A) // HEAD_DIM_A
    gsum = jnp.asarray(hid[:, None] == hid[None, :], BF16)
    gk = jnp.tile(k_norm_a.reshape(1, HEAD_DIM_A), (1, N_HEADS_A))
    gq = jnp.broadcast_to(q_norm_a.reshape(HEAD_DIM_A, 1), (HEAD_DIM_A, 128))

    full = lambda shape: pl.BlockSpec(shape, lambda i: (0,) * len(shape))
    t_spec = lambda rows: pl.BlockSpec((None, rows, tm), lambda i: (i // per_b, 0, i % per_b))
    return pl.pallas_call(
        _proj_a_kernel,
        grid=(n // tm,),
        in_specs=[
            pl.BlockSpec((tm, D_MODEL), lambda i: (i, 0)),
            full((1, D_MODEL)), full((D_MODEL, 640)), full((1536, D_MODEL)),
            full((WIDTH_A, WIDTH_A)), full((1, WIDTH_A)), full((HEAD_DIM_A, 128)),
        ],
        out_specs=[
            t_spec(WIDTH_A),
            pl.BlockSpec((tm, WIDTH_A), lambda i: (i, 0)),
            t_spec(WIDTH_A), t_spec(WIDTH_A),
            pl.BlockSpec((tm, 128), lambda i: (i, 0)),
            t_spec(N_IDX_HEADS),
        ],
        out_shape=[
            jax.ShapeDtypeStruct((batch, WIDTH_A, seq), BF16),
            jax.ShapeDtypeStruct((n, WIDTH_A), BF16),
            jax.ShapeDtypeStruct((batch, WIDTH_A, seq), BF16),
            jax.ShapeDtypeStruct((batch, WIDTH_A, seq), BF16),
            jax.ShapeDtypeStruct((n, 128), BF16),
            jax.ShapeDtypeStruct((batch, N_IDX_HEADS, seq), F32),
        ],
        compiler_params=pltpu.CompilerParams(
            dimension_semantics=("arbitrary",), vmem_limit_bytes=VMEM_LIMIT_BYTES),
        name="proj_a",
    )(x2, norm_g.reshape(1, D_MODEL), w1, w2t, gsum, gk, gq)


def _mem_kv_kernel(mem_ref, g_ref, w_ref, gk_ref, mk_ref, mv_ref):
    mn = (_rms_rows(mem_ref[...]) * g_ref[...]).astype(BF16)
    y = _dot(mn, w_ref[...])
    for h in range(N_HEADS_M):
        kh = y[:, h * HEAD_DIM_M:(h + 1) * HEAD_DIM_M]
        mk_ref[:, h * HEAD_DIM_M:(h + 1) * HEAD_DIM_M] = (_rms_rows(kh) * gk_ref[...]).astype(BF16)
    mv_ref[...] = y[:, WIDTH_M:].astype(BF16)


def _mem_kv(mem, mem_norm_g, w_mem_kv, k_norm_m):
    batch, m_len, _ = mem.shape
    full = lambda shape: pl.BlockSpec(shape, lambda b: (0,) * len(shape))
    return pl.pallas_call(
        _mem_kv_kernel,
        grid=(batch,),
        in_specs=[
            pl.BlockSpec((None, m_len, D_MODEL), lambda b: (b, 0, 0)),
            full((1, D_MODEL)), full((D_MODEL, 2 * WIDTH_M)), full((1, HEAD_DIM_M)),
        ],
        out_specs=[pl.BlockSpec((None, m_len, WIDTH_M), lambda b: (b, 0, 0))] * 2,
        out_shape=[jax.ShapeDtypeStruct((batch, m_len, WIDTH_M), BF16)] * 2,
        compiler_params=pltpu.CompilerParams(
            dimension_semantics=("arbitrary",), vmem_limit_bytes=VMEM_LIMIT_BYTES),
        name="mem_kv",
    )(mem, mem_norm_g.reshape(1, D_MODEL), w_mem_kv.astype(BF16), k_norm_m.reshape(1, HEAD_DIM_M))


def _alibi_slope(h):
    return float(2.0 ** (-8.0 * (h + 1) / N_HEADS_A))


def _dsa_attn_kernel(qat_ref, qit_ref, wt_ref, ka_ref, vat_ref, kib_ref, out_ref,
                     sc_ref, l_ref, ot_ref, rhs_ref, cut_ref, *, seq, topk):
    tq = qat_ref.shape[1]
    t0 = pl.program_id(1) * tq
    inf = jnp.float32(jnp.inf)
    kf = jnp.float32(topk)
    t_row = t0 + lax.broadcasted_iota(jnp.int32, (1, tq), 1)

    zero_rows = jnp.zeros((128 - IDX_DIM, tq), BF16)
    for h in range(N_IDX_HEADS):
        rhs_ref[h] = jnp.concatenate(
            [qit_ref[h * IDX_DIM:(h + 1) * IDX_DIM, :], zero_rows], axis=0)
    idx_scale = (IDX_DIM ** -0.5) * (N_IDX_HEADS ** -0.5)

    def score_chunk(c, carry):
        r0 = pl.multiple_of(c * CH0, CH0)
        kc = kib_ref[pl.ds(r0, CH0), :]
        acc = jnp.zeros((CH0, tq), F32)
        for h in range(N_IDX_HEADS):
            d = _dot(kc, rhs_ref[h])
            acc = acc + jnp.maximum(d, 0.0) * wt_ref[h:h + 1, :]
        acc = acc * idx_scale
        s_idx = r0 + lax.broadcasted_iota(jnp.int32, (CH0, tq), 0)
        sc_ref[pl.ds(r0, CH0), :] = jnp.where(s_idx <= t_row, acc, -inf)
        return carry

    lax.fori_loop(0, seq // CH0, score_chunk, 0)

    def col_count(pred):
        return jnp.sum(jnp.where(pred, 1.0, 0.0), axis=0, keepdims=True)

    x = sc_ref[...]
    hi = jnp.max(x, axis=0, keepdims=True)
    lo_min = jnp.min(jnp.where(x == -inf, inf, x), axis=0, keepdims=True)
    lo = lo_min
    n_causal = (t_row + 1).astype(F32)
    short = n_causal < kf

    def bisect(_, lh):
        lo, hi = lh
        mid = lo * 0.5 + hi * 0.5
        ok = col_count(sc_ref[...] >= mid) >= kf
        return jnp.where(ok, mid, lo), jnp.where(ok, hi, mid)

    lo, hi = lax.fori_loop(0, BISECT_STEPS, bisect, (lo, hi))

    def fix_cond(st):
        return jnp.max(st[4]) > 0.0

    def fix_body(st):
        lo, thr, cgt_f, cge_f, todo = st
        x = sc_ref[...]
        lo_e = jnp.min(jnp.where(x >= lo, x, inf), axis=0, keepdims=True)
        gt = x > lo_e
        cgt = col_count(gt)
        cge = col_count(x >= lo_e)
        nxt = jnp.min(jnp.where(gt, x, inf), axis=0, keepdims=True)
        active = todo > 0.0
        fin = active & (cgt < kf)
        thr = jnp.where(fin, lo_e, thr)
        cgt_f = jnp.where(fin, cgt, cgt_f)
        cge_f = jnp.where(fin, cge, cge_f)
        lo = jnp.where(active & (cgt >= kf), nxt, lo)
        todo = jnp.where(fin, 0.0, todo)
        return lo, thr, cgt_f, cge_f, todo

    todo0 = jnp.where(short, 0.0, 1.0)
    zeros = jnp.zeros((1, tq), F32)
    _, thr, cgt_f, cge_f, _ = lax.while_loop(
        fix_cond, fix_body, (lo, lo, zeros, zeros, todo0))
    thr = jnp.where(short, lo_min, thr)

    need = (cge_f > kf) & jnp.logical_not(short)
    take = kf - cgt_f
    cut_ref[...] = jnp.full((8, tq), seq, jnp.int32)

    @pl.when(jnp.max(jnp.where(need, 1.0, 0.0)) > 0.0)
    def _():
        s_idx = lax.broadcasted_iota(jnp.int32, (seq, tq), 0)
        eq = sc_ref[...] == thr

        def step(_, lh):
            lo_j, hi_j = lh
            mid = (lo_j + hi_j) >> 1
            c = col_count(eq & (s_idx <= mid))
            ge = c >= take
            return jnp.where(ge, lo_j, mid), jnp.where(ge, mid, hi_j)

        lo_j = jnp.full((1, tq), -1, jnp.int32)
        hi_j = jnp.full((1, tq), seq - 1, jnp.int32)
        _, hi_j = lax.fori_loop(0, int(np.ceil(np.log2(seq))) + 1, step, (lo_j, hi_j))
        cut_ref[...] = jnp.broadcast_to(jnp.where(need, hi_j, seq), (8, tq))

    cut = cut_ref[0:1, :]

    def pen_chunk(c, carry):
        r0 = pl.multiple_of(c * CH0, CH0)
        x = sc_ref[pl.ds(r0, CH0), :]
        s_idx = r0 + lax.broadcasted_iota(jnp.int32, (CH0, tq), 0)
        dist = (t_row - s_idx).astype(F32)
        tie = jnp.where(s_idx <= cut, dist, inf)
        sc_ref[pl.ds(r0, CH0), :] = jnp.where(x > thr, dist, jnp.where(x == thr, tie, inf))
        return carry

    lax.fori_loop(0, seq // CH0, pen_chunk, 0)

    row = lax.broadcasted_iota(jnp.int32, (128, tq), 0)
    for h in range(N_HEADS_A):
        p2 = (h // 2) * 128
        qpair = qat_ref[p2:p2 + 128, :]
        rhs_ref[h] = jnp.where((row // HEAD_DIM_A) == (h % 2), qpair, jnp.zeros_like(qpair))

    for h in range(N_HEADS_A):
        p2 = (h // 2) * 128
        slope = _alibi_slope(h)

        def logit_chunk(c, m, h=h, p2=p2, slope=slope):
            r0 = pl.multiple_of(c * CH3, CH3)
            lg = _dot(ka_ref[pl.ds(r0, CH3), p2:p2 + 128], rhs_ref[h])
            lg = lg - slope * sc_ref[pl.ds(r0, CH3), :]
            l_ref[pl.ds(r0, CH3), :] = lg
            return jnp.maximum(m, jnp.max(lg, axis=0, keepdims=True))

        m = lax.fori_loop(0, seq // CH3, logit_chunk, jnp.full((1, tq), -inf, F32))

        def pv_chunk(c, carry, h=h, m=m):
            den, acc = carry
            r0 = pl.multiple_of(c * CH3, CH3)
            p = jnp.exp(l_ref[pl.ds(r0, CH3), :] - m)
            den = den + jnp.sum(p, axis=0, keepdims=True)
            vt = vat_ref[h * HEAD_DIM_A:(h + 1) * HEAD_DIM_A, pl.ds(r0, CH3)]
            acc = acc + _dot(vt, p.astype(BF16))
            return den, acc

        den, acc = lax.fori_loop(
            0, seq // CH3, pv_chunk,
            (jnp.zeros((1, tq), F32), jnp.zeros((HEAD_DIM_A, tq), F32)))
        ot_ref[h * HEAD_DIM_A:(h + 1) * HEAD_DIM_A, :] = acc / den

    out_ref[...] = ot_ref[...].T


def _dsa_attn(qat, qit, wt, ka, vat, kib, batch, seq):
    tq = min(TQ, seq)
    topk = min(TOPK_MAX, seq // 4)
    kern = functools.partial(_dsa_attn_kernel, seq=seq, topk=topk)
    q_spec = lambda rows: pl.BlockSpec((None, rows, tq), lambda b, i: (b, 0, i))
    return pl.pallas_call(
        kern,
        grid=(batch, seq // tq),
        in_specs=[
            q_spec(WIDTH_A), q_spec(WIDTH_A), q_spec(N_IDX_HEADS),
            pl.BlockSpec((None, seq, WIDTH_A), lambda b, i: (b, 0, 0)),
            pl.BlockSpec((None, WIDTH_A, seq), lambda b, i: (b, 0, 0)),
            pl.BlockSpec((None, seq, 128), lambda b, i: (b, 0, 0)),
        ],
        out_specs=pl.BlockSpec((None, tq, WIDTH_A), lambda b, i: (b, i, 0)),
        out_shape=jax.ShapeDtypeStruct((batch, seq, WIDTH_A), F32),
        scratch_shapes=[
            pltpu.VMEM((seq, tq), F32),
            pltpu.VMEM((seq, tq), F32),
            pltpu.VMEM((WIDTH_A, tq), F32),
            pltpu.VMEM((N_HEADS_A, 128, tq), BF16),
            pltpu.VMEM((8, tq), jnp.int32),
        ],
        compiler_params=pltpu.CompilerParams(
            dimension_semantics=("arbitrary", "arbitrary"), vmem_limit_bytes=VMEM_LIMIT_BYTES),
        name="dsa_attn",
    )(qat, qit, wt, ka.reshape(batch, seq, WIDTH_A), vat, kib.reshape(batch, seq, 128))


_M_ZA, _M_BG, _M_CG, _M_HB, _M_ZB, _M_QM, _M_ZM, _M_G = (
    0, 512, 1024, 1536, 2048, 2560, 3072, 3584)
_M_COLS = 3584 + N_BRANCHES * D_MODEL


def _merge_kernel(x_ref, xh_ref, attn_ref, g_ref, wc_ref, bg_ref, cw_ref, mk_ref, mv_ref,
                  gqm_ref, wa_ref, wb_ref, wm_ref, wo_ref, out_ref, *, per_b):
    tm = x_ref.shape[0]
    xf = x_ref[...]
    xn = (_rms_rows(xf) * g_ref[...]).astype(BF16)

    def proj(lhs, c0, width):
        return _dot(lhs, wc_ref[:, c0:c0 + width])

    za = proj(xn, _M_ZA, WIDTH_A)
    ya = _dot((attn_ref[...] * jax.nn.silu(za)).astype(BF16), wa_ref[...])

    u = proj(xn, _M_CG, WIDTH_B) * proj(xn, _M_HB, WIDTH_B)
    xhn = (_rms_rows(xh_ref[...]) * g_ref[...]).astype(BF16)
    uh = proj(xhn, _M_CG, WIDTH_B) * proj(xhn, _M_HB, WIDTH_B)
    first = (pl.program_id(0) % per_b) == 0
    uh = jnp.where(first, 0.0, uh)
    rows = lax.broadcasted_iota(jnp.int32, (tm, WIDTH_B), 0)
    u1 = jnp.where(rows == 0, uh[7:8, :], pltpu.roll(u, 1, 0))
    u2 = jnp.where(rows == 0, uh[6:7, :], jnp.where(rows == 1, uh[7:8, :], pltpu.roll(u, 2, 0)))
    conv = cw_ref[0:1, :] * u2 + cw_ref[1:2, :] * u1 + cw_ref[2:3, :] * u
    bgate = proj(xn, _M_BG, WIDTH_B)
    zb = proj(xn, _M_ZB, WIDTH_B)
    yb = _dot(((bgate * conv) * jax.nn.silu(zb)).astype(BF16), wb_ref[...])

    qm = proj(xn, _M_QM, WIDTH_M)
    zm = proj(xn, _M_ZM, WIDTH_M)
    heads = []
    for h in range(N_HEADS_M):
        sl = slice(h * HEAD_DIM_M, (h + 1) * HEAD_DIM_M)
        qh = (_rms_rows(qm[:, sl]) * gqm_ref[...]).astype(BF16)
        lg = _dot_nt(qh, mk_ref[:, sl]) * (HEAD_DIM_M ** -0.5)
        e = jnp.exp(lg - jnp.max(lg, axis=-1, keepdims=True))
        p = e / jnp.sum(e, axis=-1, keepdims=True)
        heads.append(_dot(p.astype(BF16), mv_ref[:, sl]))
    attn_m = jnp.concatenate(heads, axis=1)
    ym = _dot((attn_m * jax.nn.silu(zm)).astype(BF16), wm_ref[...])

    def gate(j):
        gj = proj(xn, _M_G + j * D_MODEL, D_MODEL) + bg_ref[:, j * D_MODEL:(j + 1) * D_MODEL]
        return jax.nn.sigmoid(gj)

    merged = gate(0) * ya + gate(1) * yb + gate(2) * ym
    out_ref[...] = xf + _dot(merged.astype(BF16), wo_ref[...])


def _merge(x2, attn2, mk, mv, norm_g, w_in, b_gate, conv_w, q_norm_m,
           w_out_a, w_out_b, w_out_m, w_o, batch, seq):
    n = x2.shape[0]
    tm = min(TM_C, seq)
    per_b = seq // tm
    m_len = mk.shape[1]
    wc = jnp.concatenate([w_in[:, _C_ZA:_C_ZA + 512], w_in[:, _C_REST:]], axis=1).astype(BF16)
    const = lambda shape: pl.BlockSpec(shape, lambda i: (0,) * len(shape),
                                       pipeline_mode=pl.Buffered(1))
    kern = functools.partial(_merge_kernel, per_b=per_b)
    return pl.pallas_call(
        kern,
        grid=(n // tm,),
        in_specs=[
            pl.BlockSpec((tm, D_MODEL), lambda i: (i, 0)),
            pl.BlockSpec((8, D_MODEL), lambda i: (jnp.maximum(i * (tm // 8) - 1, 0), 0)),
            pl.BlockSpec((tm, WIDTH_A), lambda i: (i, 0)),
            const((1, D_MODEL)), const((D_MODEL, _M_COLS)), const((1, N_BRANCHES * D_MODEL)),
            const((CONV_WIDTH, WIDTH_B)),
            pl.BlockSpec((None, m_len, WIDTH_M), lambda i: (i // per_b, 0, 0)),
            pl.BlockSpec((None, m_len, WIDTH_M), lambda i: (i // per_b, 0, 0)),
            const((1, HEAD_DIM_M)),
            const((WIDTH_A, D_MODEL)), const((WIDTH_B, D_MODEL)), const((WIDTH_M, D_MODEL)),
            const((D_MODEL, D_MODEL)),
        ],
        out_specs=pl.BlockSpec((tm, D_MODEL), lambda i: (i, 0)),
        out_shape=jax.ShapeDtypeStruct((n, D_MODEL), F32),
        compiler_params=pltpu.CompilerParams(
            dimension_semantics=("arbitrary",), vmem_limit_bytes=VMEM_LIMIT_BYTES),
        name="merge",
    )(x2, x2, attn2, norm_g.reshape(1, D_MODEL), wc, b_gate.reshape(1, -1), conv_w,
      mk, mv, q_norm_m.reshape(1, HEAD_DIM_M),
      w_out_a.astype(BF16), w_out_b.astype(BF16), w_out_m.astype(BF16), w_o.astype(BF16))


def _layer(h, mem, norm_g, mem_norm_g, w_in, b_gate, w_mem_kv, q_norm_a, k_norm_a,
           q_norm_m, k_norm_m, conv_w, w_out_a, w_out_b, w_out_m, w_o):
    batch, seq, _ = h.shape
    x2 = h.reshape(batch * seq, D_MODEL)
    qat, ka, vat, qit, kib, wt = _proj_a(x2, norm_g, w_in, q_norm_a, k_norm_a, batch, seq)
    mk, mv = _mem_kv(mem, mem_norm_g, w_mem_kv, k_norm_m)
    attn = _dsa_attn(qat, qit, wt, ka, vat, kib, batch, seq)
    out = _merge(x2, attn.reshape(batch * seq, WIDTH_A), mk, mv, norm_g, w_in, b_gate, conv_w,
                 q_norm_m, w_out_a, w_out_b, w_out_m, w_o, batch, seq)
    return out.reshape(batch, seq, D_MODEL)


def kernel(x, mem, norm_g, mem_norm_g, w_in, b_gate, w_mem_kv, q_norm_a, k_norm_a,
           q_norm_m, k_norm_m, conv_w, w_out_a, w_out_b, w_out_m, w_o):
    h = x
    for l in range(norm_g.shape[0]):
        h = _layer(h, mem, norm_g[l], mem_norm_g[l], w_in[l], b_gate[l], w_mem_kv[l],
                   q_norm_a[l], k_norm_a[l], q_norm_m[l], k_norm_m[l], conv_w[l],
                   w_out_a[l], w_out_b[l], w_out_m[l], w_o[l])
    return h
```

```python
import functools

import jax
import jax.numpy as jnp
import numpy as np
from jax import lax
from jax.experimental import pallas as pl
from jax.experimental.pallas import tpu as pltpu

F32 = jnp.float32
BF16 = jnp.bfloat16

D_MODEL = 1024
N_HEADS_A = 8
HEAD_DIM_A = 64
WIDTH_A = 512
N_IDX_HEADS = 8
IDX_DIM = 64
TOPK_MAX = 256
WIDTH_B = 512
CONV_WIDTH = 3
N_HEADS_M = 4
HEAD_DIM_M = 128
WIDTH_M = 512
N_BRANCHES = 3
RMS_EPS = 1e-6

_C_QA, _C_KA, _C_VA, _C_ZA = 0, 512, 1024, 1536
_C_QI, _C_KI, _C_WI = 2048, 2560, 2624
_C_REST = 2632
_D_IN = 8776

VMEM_LIMIT_BYTES = 56 * 1024 * 1024

TM_A = 512
TM_C = 256
TQ = 256
FOLD_ROWS = 32
BISECT_BLIND = 18
BISECT_CAP = 28
SOFTMAX_M_INIT = -1e30


def _rms_rows(xf, eps=RMS_EPS):
    return xf * lax.rsqrt(jnp.mean(xf * xf, axis=-1, keepdims=True) + eps)


def _dot(a, b):
    return jnp.dot(a, b, preferred_element_type=F32)


def _dot_nt(a, b):
    return lax.dot_general(a, b, (((1,), (1,)), ((), ())), preferred_element_type=F32)


def _proj_a_kernel(x_ref, g_ref, w1_ref, w2t_ref, gsum_ref, gk_ref, gq_ref,
                   qat_ref, ka_ref, vat_ref, qit_ref, kib_ref, wt_ref):
    tm = x_ref.shape[0]
    xn = (_rms_rows(x_ref[...]) * g_ref[...]).astype(BF16)

    y1 = _dot(xn, w1_ref[...])
    ka_raw = y1[:, :WIDTH_A]
    sq = ka_raw * ka_raw
    sq_hi = sq.astype(BF16)
    sq_lo = (sq - sq_hi.astype(F32)).astype(BF16)
    ss = _dot(sq_hi, gsum_ref[...]) + _dot(sq_lo, gsum_ref[...])
    ka = ka_raw * lax.rsqrt(ss * (1.0 / HEAD_DIM_A) + RMS_EPS) * gk_ref[...]
    ka_ref[...] = ka.astype(BF16)
    kiwi = y1[:, WIDTH_A:]
    lane = lax.broadcasted_iota(jnp.int32, kiwi.shape, 1)
    kib_ref[...] = jnp.where(lane < IDX_DIM, kiwi, 0.0).astype(BF16)
    wt_ref[...] = kiwi.T[IDX_DIM:IDX_DIM + N_IDX_HEADS, :]

    yt = _dot_nt(w2t_ref[...], xn)
    gq = jnp.concatenate([gq_ref[...]] * (tm // 128), axis=1)
    for h in range(N_HEADS_A):
        qh = yt[h * HEAD_DIM_A:(h + 1) * HEAD_DIM_A, :]
        ms = jnp.mean(qh * qh, axis=0, keepdims=True)
        qn = qh * lax.rsqrt(ms + RMS_EPS) * gq
        qat_ref[h * HEAD_DIM_A:(h + 1) * HEAD_DIM_A, :] = (qn * (HEAD_DIM_A ** -0.5)).astype(BF16)
    vat_ref[...] = yt[WIDTH_A:2 * WIDTH_A, :].astype(BF16)
    qit_ref[...] = yt[2 * WIDTH_A:3 * WIDTH_A, :].astype(BF16)


def _proj_a(x2, norm_g, w_in, q_norm_a, k_norm_a, batch, seq):
    n = x2.shape[0]
    tm = min(TM_A, seq)
    per_b = seq // tm
    w1 = jnp.concatenate(
        [w_in[:, _C_KA:_C_KA + 512], w_in[:, _C_KI:_C_KI + 72],
         jnp.zeros((D_MODEL, 56), F32)], axis=1).astype(BF16)
    w2t = jnp.concatenate(
        [w_in[:, _C_QA:_C_QA + 512], w_in[:, _C_VA:_C_VA + 512],
         w_in[:, _C_QI:_C_QI + 512]], axis=1).T.astype(BF16)
    hid = np.arange(WIDTH_A) // HEAD_DIM_A
    gsum = jnp.asarray(hid[:, None] == hid[None, :], BF16)
    gk = jnp.tile(k_norm_a.reshape(1, HEAD_DIM_A), (1, N_HEADS_A))
    gq = jnp.broadcast_to(q_norm_a.reshape(HEAD_DIM_A, 1), (HEAD_DIM_A, 128))

    full = lambda shape: pl.BlockSpec(shape, lambda i: (0,) * len(shape))
    t_spec = lambda rows: pl.BlockSpec((None, rows, tm), lambda i: (i // per_b, 0, i % per_b))
    return pl.pallas_call(
        _proj_a_kernel,
        grid=(n // tm,),
        in_specs=[
            pl.BlockSpec((tm, D_MODEL), lambda i: (i, 0)),
            full((1, D_MODEL)), full((D_MODEL, 640)), full((1536, D_MODEL)),
            full((WIDTH_A, WIDTH_A)), full((1, WIDTH_A)), full((HEAD_DIM_A, 128)),
        ],
        out_specs=[
            t_spec(WIDTH_A),
            pl.BlockSpec((tm, WIDTH_A), lambda i: (i, 0)),
            t_spec(WIDTH_A), t_spec(WIDTH_A),
            pl.BlockSpec((tm, 128), lambda i: (i, 0)),
            t_spec(N_IDX_HEADS),
        ],
        out_shape=[
            jax.ShapeDtypeStruct((batch, WIDTH_A, seq), BF16),
            jax.ShapeDtypeStruct((n, WIDTH_A), BF16),
            jax.ShapeDtypeStruct((batch, WIDTH_A, seq), BF16),
            jax.ShapeDtypeStruct((batch, WIDTH_A, seq), BF16),
            jax.ShapeDtypeStruct((n, 128), BF16),
            jax.ShapeDtypeStruct((batch, N_IDX_HEADS, seq), F32),
        ],
        compiler_params=pltpu.CompilerParams(
            dimension_semantics=("arbitrary",), vmem_limit_bytes=VMEM_LIMIT_BYTES),
        name="proj_a",
    )(x2, norm_g.reshape(1, D_MODEL), w1, w2t, gsum, gk, gq)


def _mem_kv_kernel(mem_ref, g_ref, w_ref, gk_ref, mk_ref, mv_ref):
    mn = (_rms_rows(mem_ref[...]) * g_ref[...]).astype(BF16)
    y = _dot(mn, w_ref[...])
    for h in range(N_HEADS_M):
        kh = y[:, h * HEAD_DIM_M:(h + 1) * HEAD_DIM_M]
        mk_ref[:, h * HEAD_DIM_M:(h + 1) * HEAD_DIM_M] = (_rms_rows(kh) * gk_ref[...]).astype(BF16)
    mv_ref[...] = y[:, WIDTH_M:].astype(BF16)


def _mem_kv(mem, mem_norm_g, w_mem_kv, k_norm_m):
    batch, m_len, _ = mem.shape
    full = lambda shape: pl.BlockSpec(shape, lambda b: (0,) * len(shape))
    return pl.pallas_call(
        _mem_kv_kernel,
        grid=(batch,),
        in_specs=[
            pl.BlockSpec((None, m_len, D_MODEL), lambda b: (b, 0, 0)),
            full((1, D_MODEL)), full((D_MODEL, 2 * WIDTH_M)), full((1, HEAD_DIM_M)),
        ],
        out_specs=[pl.BlockSpec((None, m_len, WIDTH_M), lambda b: (b, 0, 0))] * 2,
        out_shape=[jax.ShapeDtypeStruct((batch, m_len, WIDTH_M), BF16)] * 2,
        compiler_params=pltpu.CompilerParams(
            dimension_semantics=("arbitrary",), vmem_limit_bytes=VMEM_LIMIT_BYTES),
        name="mem_kv",
    )(mem, mem_norm_g.reshape(1, D_MODEL), w_mem_kv.astype(BF16), k_norm_m.reshape(1, HEAD_DIM_M))


def _alibi_slope(h):
    return float(2.0 ** (-8.0 * (h + 1) / N_HEADS_A))


def _dsa_attn_kernel(qat_ref, qit_ref, wt_ref, ka_ref, vat_ref, kib_ref, out_ref,
                     sc_ref, ot_ref, rhsi_ref, rhsa_ref, cut_ref, m_ref, d_ref, lga_ref, lgb_ref,
                     *, seq, topk):
    tq = qat_ref.shape[1]
    ck = tq
    i = pl.program_id(1)
    t0 = i * tq
    nk = i + 1
    inf = jnp.float32(jnp.inf)
    kf = jnp.float32(topk)
    t_row = t0 + lax.broadcasted_iota(jnp.int32, (1, tq), 1)

    def fold(w, op):
        return op(w.reshape(ck // FOLD_ROWS, FOLD_ROWS, w.shape[1]), axis=0)

    def over_chunks(body, init):
        def step(c, carry):
            r0 = pl.multiple_of(c * ck, ck)
            return body(sc_ref[pl.ds(r0, ck), :], r0, carry)
        return lax.fori_loop(0, nk, step, init)

    def col(acc, op):
        return op(acc, axis=0, keepdims=True)

    facc = lambda v: jnp.full((FOLD_ROWS, tq), v, F32)

    zero_rows = jnp.zeros((128 - IDX_DIM, tq), BF16)
    for h in range(N_IDX_HEADS):
        rhsi_ref[h] = jnp.concatenate(
            [qit_ref[h * IDX_DIM:(h + 1) * IDX_DIM, :], zero_rows], axis=0)
    idx_scale = (IDX_DIM ** -0.5) * (N_IDX_HEADS ** -0.5)

    def score_chunk(c, carry):
        r0 = pl.multiple_of(c * ck, ck)
        kc = kib_ref[pl.ds(r0, ck), :]
        acc = jnp.zeros((ck, tq), F32)
        for h in range(N_IDX_HEADS):
            d = _dot(kc, rhsi_ref[h])
            acc = acc + jnp.maximum(d, 0.0) * wt_ref[h:h + 1, :]
        acc = acc * idx_scale
        s_idx = r0 + lax.broadcasted_iota(jnp.int32, (ck, tq), 0)
        sc_ref[pl.ds(r0, ck), :] = jnp.where(s_idx <= t_row, acc, -inf)
        return carry

    lax.fori_loop(0, nk, score_chunk, 0)

    def count_ge(v):
        acc = over_chunks(lambda x, r0, a: a + fold(jnp.where(x >= v, 1.0, 0.0), jnp.sum), facc(0.0))
        return col(acc, jnp.sum)

    def minmax(x, r0, carry):
        lo_a, hi_a = carry
        return (jnp.minimum(lo_a, fold(jnp.where(x == -inf, inf, x), jnp.min)),
                jnp.maximum(hi_a, fold(x, jnp.max)))

    lo_a, hi_a = over_chunks(minmax, (facc(inf), facc(-inf)))
    lo_min = col(lo_a, jnp.min)
    hi_max = col(hi_a, jnp.max)
    n_causal = (t_row + 1).astype(F32)
    short = n_causal < kf

    def bis_step(lo, hi, cl):
        mid = lo * 0.5 + hi * 0.5
        c = count_ge(mid)
        ok = c >= kf
        return jnp.where(ok, mid, lo), jnp.where(ok, hi, mid), jnp.where(ok, c, cl)

    def bis_cond(st):
        return (st[3] < BISECT_CAP) & (jnp.max(jnp.where(st[2] == kf, 0.0, 1.0)) > 0.0)

    cl0 = jnp.where(short, kf, n_causal)
    st = lax.fori_loop(0, BISECT_BLIND, lambda _, s: bis_step(*s), (lo_min, hi_max, cl0))
    lo, _, cl, _ = lax.while_loop(
        bis_cond, lambda s: bis_step(*s[:3]) + (s[3] + 1,), st + (jnp.int32(BISECT_BLIND),))

    def fix_cond(st):
        return jnp.max(st[4]) > 0.0

    def fix_body(st):
        lo, thr, cgt_f, cge_f, todo = st
        lo_e = col(over_chunks(
            lambda x, r0, a: jnp.minimum(a, fold(jnp.where(x >= lo, x, inf), jnp.min)), facc(inf)), jnp.min)

        def stats(x, r0, carry):
            cgt_a, cge_a, nxt_a = carry
            gt = x > lo_e
            return (cgt_a + fold(jnp.where(gt, 1.0, 0.0), jnp.sum),
                    cge_a + fold(jnp.where(x >= lo_e, 1.0, 0.0), jnp.sum),
                    jnp.minimum(nxt_a, fold(jnp.where(gt, x, inf), jnp.min)))

        cgt_a, cge_a, nxt_a = over_chunks(stats, (facc(0.0), facc(0.0), facc(inf)))
        cgt, cge, nxt = col(cgt_a, jnp.sum), col(cge_a, jnp.sum), col(nxt_a, jnp.min)
        active = todo > 0.0
        fin = active & (cgt < kf)
        thr = jnp.where(fin, lo_e, thr)
        cgt_f = jnp.where(fin, cgt, cgt_f)
        cge_f = jnp.where(fin, cge, cge_f)
        lo = jnp.where(active & (cgt >= kf), nxt, lo)
        todo = jnp.where(fin, 0.0, todo)
        return lo, thr, cgt_f, cge_f, todo

    todo0 = jnp.where(cl == kf, 0.0, 1.0)
    zeros = jnp.zeros((1, tq), F32)
    _, thr, cgt_f, cge_f, _ = lax.while_loop(fix_cond, fix_body, (lo, lo, zeros, zeros, todo0))
    thr = jnp.where(short, lo_min, thr)

    need = cge_f > kf
    take = kf - cgt_f
    cut_ref[...] = jnp.full((8, tq), seq, jnp.int32)

    @pl.when(jnp.max(jnp.where(need, 1.0, 0.0)) > 0.0)
    def _():
        def step(_, lh):
            lo_j, hi_j = lh
            mid = (lo_j + hi_j) >> 1

            def tied(x, r0, a):
                s_idx = r0 + lax.broadcasted_iota(jnp.int32, (ck, tq), 0)
                hit = jnp.where(x == thr, jnp.where(s_idx <= mid, 1.0, 0.0), 0.0)
                return a + fold(hit, jnp.sum)

            ge = col(over_chunks(tied, facc(0.0)), jnp.sum) >= take
            return jnp.where(ge, lo_j, mid), jnp.where(ge, mid, hi_j)

        lo_j = jnp.full((1, tq), -1, jnp.int32)
        hi_j = jnp.full((1, tq), seq - 1, jnp.int32)
        _, hi_j = lax.fori_loop(0, int(np.ceil(np.log2(seq))) + 1, step, (lo_j, hi_j))
        cut_ref[...] = jnp.broadcast_to(jnp.where(need, hi_j, seq), (8, tq))

    cut = cut_ref[0:1, :]

    def pen_chunk(c, carry):
        r0 = pl.multiple_of(c * ck, ck)
        x = sc_ref[pl.ds(r0, ck), :]
        s_idx = r0 + lax.broadcasted_iota(jnp.int32, (ck, tq), 0)
        dist = (t_row - s_idx).astype(F32)
        tie = jnp.where(s_idx <= cut, dist, inf)
        sc_ref[pl.ds(r0, ck), :] = jnp.where(x > thr, dist, jnp.where(x == thr, tie, inf))
        return carry

    lax.fori_loop(0, nk, pen_chunk, 0)

    row = lax.broadcasted_iota(jnp.int32, (128, tq), 0)
    for p in range(N_HEADS_A // 2):
        qpair = qat_ref[p * 128:(p + 1) * 128, :]
        rhsa_ref[p] = jnp.concatenate(
            [jnp.where(row < HEAD_DIM_A, qpair, jnp.zeros_like(qpair)),
             jnp.where(row >= HEAD_DIM_A, qpair, jnp.zeros_like(qpair))], axis=1)

    m_ref[...] = jnp.full((N_HEADS_A, tq), SOFTMAX_M_INIT, F32)
    d_ref[...] = jnp.zeros((N_HEADS_A, tq), F32)
    ot_ref[...] = jnp.zeros((WIDTH_A, tq), F32)

    def qk_into(buf, c):
        r0 = pl.multiple_of(c * ck, ck)
        for p in range(N_HEADS_A // 2):
            buf[:, p * 2 * tq:(p + 1) * 2 * tq] = _dot(
                ka_ref[pl.ds(r0, ck), p * 128:(p + 1) * 128], rhsa_ref[p])

    def soft_pv(buf, c):
        r0 = pl.multiple_of(c * ck, ck)
        pen = sc_ref[pl.ds(r0, ck), :]
        for h in range(N_HEADS_A):
            rows = slice(h * HEAD_DIM_A, (h + 1) * HEAD_DIM_A)
            alphas, probs = [], []
            for lt in range(tq // 128):
                ln = slice(lt * 128, (lt + 1) * 128)
                m = m_ref[h:h + 1, ln]
                lg = buf[:, h * tq + lt * 128:h * tq + (lt + 1) * 128] - _alibi_slope(h) * pen[:, ln]
                m_new = jnp.maximum(m, col(fold(lg, jnp.max), jnp.max))
                alpha = jnp.exp(m - m_new)
                pr = jnp.exp(lg - m_new)
                m_ref[h:h + 1, ln] = m_new
                d_ref[h:h + 1, ln] = d_ref[h:h + 1, ln] * alpha + col(fold(pr, jnp.sum), jnp.sum)
                alphas.append(alpha)
                probs.append(pr.astype(BF16))
            vt = vat_ref[rows, pl.ds(r0, ck)]
            ot_ref[rows, :] = (ot_ref[rows, :] * jnp.concatenate(alphas, axis=1)
                               + _dot(vt, jnp.concatenate(probs, axis=1)))

    last = nk - 1
    qk_into(lga_ref, 0)

    def attn_pair(cc, carry):
        c0 = 2 * cc
        qk_into(lgb_ref, jnp.minimum(c0 + 1, last))
        soft_pv(lga_ref, c0)

        @pl.when(c0 + 1 < nk)
        def _():
            qk_into(lga_ref, jnp.minimum(c0 + 2, last))
            soft_pv(lgb_ref, c0 + 1)

        return carry

    lax.fori_loop(0, (nk + 1) // 2, attn_pair, 0)
    for h in range(N_HEADS_A):
        rows = slice(h * HEAD_DIM_A, (h + 1) * HEAD_DIM_A)
        ot_ref[rows, :] = ot_ref[rows, :] / d_ref[h:h + 1, :]
    out_ref[...] = ot_ref[...].T


def _dsa_attn(qat, qit, wt, ka, vat, kib, batch, seq):
    tq = min(TQ, seq)
    topk = min(TOPK_MAX, seq // 4)
    kern = functools.partial(_dsa_attn_kernel, seq=seq, topk=topk)
    q_spec = lambda rows: pl.BlockSpec((None, rows, tq), lambda b, i: (b, 0, i))
    return pl.pallas_call(
        kern,
        grid=(batch, seq // tq),
        in_specs=[
            q_spec(WIDTH_A), q_spec(WIDTH_A), q_spec(N_IDX_HEADS),
            pl.BlockSpec((None, seq, WIDTH_A), lambda b, i: (b, 0, 0)),
            pl.BlockSpec((None, WIDTH_A, seq), lambda b, i: (b, 0, 0)),
            pl.BlockSpec((None, seq, 128), lambda b, i: (b, 0, 0)),
        ],
        out_specs=pl.BlockSpec((None, tq, WIDTH_A), lambda b, i: (b, i, 0)),
        out_shape=jax.ShapeDtypeStruct((batch, seq, WIDTH_A), F32),
        scratch_shapes=[
            pltpu.VMEM((seq, tq), F32),
            pltpu.VMEM((WIDTH_A, tq), F32),
            pltpu.VMEM((N_IDX_HEADS, 128, tq), BF16),
            pltpu.VMEM((N_HEADS_A // 2, 128, 2 * tq), BF16),
            pltpu.VMEM((8, tq), jnp.int32),
            pltpu.VMEM((N_HEADS_A, tq), F32),
            pltpu.VMEM((N_HEADS_A, tq), F32),
            pltpu.VMEM((tq, N_HEADS_A * tq), F32),
            pltpu.VMEM((tq, N_HEADS_A * tq), F32),
        ],
        compiler_params=pltpu.CompilerParams(
            dimension_semantics=("arbitrary", "arbitrary"), vmem_limit_bytes=VMEM_LIMIT_BYTES),
        name="dsa_attn",
    )(qat, qit, wt, ka.reshape(batch, seq, WIDTH_A), vat, kib.reshape(batch, seq, 128))


_M_ZA, _M_BG, _M_CG, _M_HB, _M_ZB, _M_QM, _M_ZM, _M_G = (
    0, 512, 1024, 1536, 2048, 2560, 3072, 3584)
_M_COLS = 3584 + N_BRANCHES * D_MODEL


def _merge_kernel(x_ref, xh_ref, attn_ref, g_ref, wc_ref, bg_ref, cw_ref, mk_ref, mv_ref,
                  gqm_ref, wa_ref, wb_ref, wm_ref, wo_ref, out_ref, *, per_b):
    tm = x_ref.shape[0]
    xf = x_ref[...]
    xn = (_rms_rows(xf) * g_ref[...]).astype(BF16)

    def proj(lhs, c0, width):
        return _dot(lhs, wc_ref[:, c0:c0 + width])

    za = proj(xn, _M_ZA, WIDTH_A)
    ya = _dot((attn_ref[...] * jax.nn.silu(za)).astype(BF16), wa_ref[...])

    u = proj(xn, _M_CG, WIDTH_B) * proj(xn, _M_HB, WIDTH_B)
    xhn = (_rms_rows(xh_ref[...]) * g_ref[...]).astype(BF16)
    uh = proj(xhn, _M_CG, WIDTH_B) * proj(xhn, _M_HB, WIDTH_B)
    first = (pl.program_id(0) % per_b) == 0
    uh = jnp.where(first, 0.0, uh)
    rows = lax.broadcasted_iota(jnp.int32, (tm, WIDTH_B), 0)
    u1 = jnp.where(rows == 0, uh[7:8, :], pltpu.roll(u, 1, 0))
    u2 = jnp.where(rows == 0, uh[6:7, :], jnp.where(rows == 1, uh[7:8, :], pltpu.roll(u, 2, 0)))
    conv = cw_ref[0:1, :] * u2 + cw_ref[1:2, :] * u1 + cw_ref[2:3, :] * u
    bgate = proj(xn, _M_BG, WIDTH_B)
    zb = proj(xn, _M_ZB, WIDTH_B)
    yb = _dot(((bgate * conv) * jax.nn.silu(zb)).astype(BF16), wb_ref[...])

    qm = proj(xn, _M_QM, WIDTH_M)
    zm = proj(xn, _M_ZM, WIDTH_M)
    heads = []
    for h in range(N_HEADS_M):
        sl = slice(h * HEAD_DIM_M, (h + 1) * HEAD_DIM_M)
        qh = (_rms_rows(qm[:, sl]) * gqm_ref[...]).astype(BF16)
        lg = _dot_nt(qh, mk_ref[:, sl]) * (HEAD_DIM_M ** -0.5)
        e = jnp.exp(lg - jnp.max(lg, axis=-1, keepdims=True))
        p = e / jnp.sum(e, axis=-1, keepdims=True)
        heads.append(_dot(p.astype(BF16), mv_ref[:, sl]))
    attn_m = jnp.concatenate(heads, axis=1)
    ym = _dot((attn_m * jax.nn.silu(zm)).astype(BF16), wm_ref[...])

    def gate(j):
        gj = proj(xn, _M_G + j * D_MODEL, D_MODEL) + bg_ref[:, j * D_MODEL:(j + 1) * D_MODEL]
        return jax.nn.sigmoid(gj)

    merged = gate(0) * ya + gate(1) * yb + gate(2) * ym
    out_ref[...] = xf + _dot(merged.astype(BF16), wo_ref[...])


def _merge(x2, attn2, mk, mv, norm_g, w_in, b_gate, conv_w, q_norm_m,
           w_out_a, w_out_b, w_out_m, w_o, batch, seq):
    n = x2.shape[0]
    tm = min(TM_C, seq)
    per_b = seq // tm
    m_len = mk.shape[1]
    wc = jnp.concatenate([w_in[:, _C_ZA:_C_ZA + 512], w_in[:, _C_REST:]], axis=1).astype(BF16)
    const = lambda shape: pl.BlockSpec(shape, lambda i: (0,) * len(shape),
                                       pipeline_mode=pl.Buffered(1))
    kern = functools.partial(_merge_kernel, per_b=per_b)
    return pl.pallas_call(
        kern,
        grid=(n // tm,),
        in_specs=[
            pl.BlockSpec((tm, D_MODEL), lambda i: (i, 0)),
            pl.BlockSpec((8, D_MODEL), lambda i: (jnp.maximum(i * (tm // 8) - 1, 0), 0)),
            pl.BlockSpec((tm, WIDTH_A), lambda i: (i, 0)),
            const((1, D_MODEL)), const((D_MODEL, _M_COLS)), const((1, N_BRANCHES * D_MODEL)),
            const((CONV_WIDTH, WIDTH_B)),
            pl.BlockSpec((None, m_len, WIDTH_M), lambda i: (i // per_b, 0, 0)),
            pl.BlockSpec((None, m_len, WIDTH_M), lambda i: (i // per_b, 0, 0)),
            const((1, HEAD_DIM_M)),
            const((WIDTH_A, D_MODEL)), const((WIDTH_B, D_MODEL)), const((WIDTH_M, D_MODEL)),
            const((D_MODEL, D_MODEL)),
        ],
        out_specs=pl.BlockSpec((tm, D_MODEL), lambda i: (i, 0)),
        out_shape=jax.ShapeDtypeStruct((n, D_MODEL), F32),
        compiler_params=pltpu.CompilerParams(
            dimension_semantics=("arbitrary",), vmem_limit_bytes=VMEM_LIMIT_BYTES),
        name="merge",
    )(x2, x2, attn2, norm_g.reshape(1, D_MODEL), wc, b_gate.reshape(1, -1), conv_w,
      mk, mv, q_norm_m.reshape(1, HEAD_DIM_M),
      w_out_a.astype(BF16), w_out_b.astype(BF16), w_out_m.astype(BF16), w_o.astype(BF16))


def _layer(h, mem, norm_g, mem_norm_g, w_in, b_gate, w_mem_kv, q_norm_a, k_norm_a,
           q_norm_m, k_norm_m, conv_w, w_out_a, w_out_b, w_out_m, w_o):
    batch, seq, _ = h.shape
    x2 = h.reshape(batch * seq, D_MODEL)
    qat, ka, vat, qit, kib, wt = _proj_a(x2, norm_g, w_in, q_norm_a, k_norm_a, batch, seq)
    mk, mv = _mem_kv(mem, mem_norm_g, w_mem_kv, k_norm_m)
    attn = _dsa_attn(qat, qit, wt, ka, vat, kib, batch, seq)
    out = _merge(x2, attn.reshape(batch * seq, WIDTH_A), mk, mv, norm_g, w_in, b_gate, conv_w,
                 q_norm_m, w_out_a, w_out_b, w_out_m, w_o, batch, seq)
    return out.reshape(batch, seq, D_MODEL)


def kernel(x, mem, norm_g, mem_norm_g, w_in, b_gate, w_mem_kv, q_norm_a, k_norm_a,
           q_norm_m, k_norm_m, conv_w, w_out_a, w_out_b, w_out_m, w_o):
    h = x
    for l in range(norm_g.shape[0]):
        h = _layer(h, mem, norm_g[l], mem_norm_g[l], w_in[l], b_gate[l], w_mem_kv[l],
                   q_norm_a[l], k_norm_a[l], q_norm_m[l], k_norm_m[l], conv_w[l],
                   w_out_a[l], w_out_b[l], w_out_m[l], w_o[l])
    return h
```

```python
import functools

import jax
import jax.numpy as jnp
import numpy as np
from jax import lax
from jax.experimental import pallas as pl
from jax.experimental.pallas import tpu as pltpu

F32 = jnp.float32
BF16 = jnp.bfloat16

D_MODEL = 1024
N_HEADS_A = 8
HEAD_DIM_A = 64
WIDTH_A = 512
N_IDX_HEADS = 8
IDX_DIM = 64
TOPK_MAX = 256
WIDTH_B = 512
CONV_WIDTH = 3
N_HEADS_M = 4
HEAD_DIM_M = 128
WIDTH_M = 512
N_BRANCHES = 3
RMS_EPS = 1e-6

_C_QA, _C_KA, _C_VA, _C_ZA = 0, 512, 1024, 1536
_C_QI, _C_KI, _C_WI = 2048, 2560, 2624
_C_REST = 2632
_D_IN = 8776

VMEM_LIMIT_BYTES = 56 * 1024 * 1024

TM_A = 512
TM_C = 256
TQ = 256
FOLD_ROWS = 32
BISECT_BLIND = 20
BISECT_CAP = 28
SOFTMAX_M_INIT = -1e30


def _rms_rows(xf, eps=RMS_EPS):
    return xf * lax.rsqrt(jnp.mean(xf * xf, axis=-1, keepdims=True) + eps)


def _dot(a, b):
    return jnp.dot(a, b, preferred_element_type=F32)


def _dot_nt(a, b):
    return lax.dot_general(a, b, (((1,), (1,)), ((), ())), preferred_element_type=F32)


def _proj_a_kernel(x_ref, g_ref, w1_ref, w2t_ref, gsum_ref, gk_ref, gq_ref,
                   qat_ref, ka_ref, vat_ref, qit_ref, kib_ref, wt_ref):
    tm = x_ref.shape[0]
    xn = (_rms_rows(x_ref[...]) * g_ref[...]).astype(BF16)

    y1 = _dot(xn, w1_ref[...])
    ka_raw = y1[:, :WIDTH_A]
    sq = ka_raw * ka_raw
    sq_hi = sq.astype(BF16)
    sq_lo = (sq - sq_hi.astype(F32)).astype(BF16)
    ss = _dot(sq_hi, gsum_ref[...]) + _dot(sq_lo, gsum_ref[...])
    ka = ka_raw * lax.rsqrt(ss * (1.0 / HEAD_DIM_A) + RMS_EPS) * gk_ref[...]
    ka_ref[...] = ka.astype(BF16)
    kiwi = y1[:, WIDTH_A:]
    lane = lax.broadcasted_iota(jnp.int32, kiwi.shape, 1)
    kib_ref[...] = jnp.where(lane < IDX_DIM, kiwi, 0.0).astype(BF16)
    wt_ref[...] = kiwi.T[IDX_DIM:IDX_DIM + N_IDX_HEADS, :]

    yt = _dot_nt(w2t_ref[...], xn)
    gq = jnp.concatenate([gq_ref[...]] * (tm // 128), axis=1)
    for h in range(N_HEADS_A):
        qh = yt[h * HEAD_DIM_A:(h + 1) * HEAD_DIM_A, :]
        ms = jnp.mean(qh * qh, axis=0, keepdims=True)
        qn = qh * lax.rsqrt(ms + RMS_EPS) * gq
        qat_ref[h * HEAD_DIM_A:(h + 1) * HEAD_DIM_A, :] = (qn * (HEAD_DIM_A ** -0.5)).astype(BF16)
    vat_ref[...] = yt[WIDTH_A:2 * WIDTH_A, :].astype(BF16)
    qit_ref[...] = yt[2 * WIDTH_A:3 * WIDTH_A, :].astype(BF16)


def _proj_a(x2, norm_g, w_in, q_norm_a, k_norm_a, batch, seq):
    n = x2.shape[0]
    tm = min(TM_A, seq)
    per_b = seq // tm
    w1 = jnp.concatenate(
        [w_in[:, _C_KA:_C_KA + 512], w_in[:, _C_KI:_C_KI + 72],
         jnp.zeros((D_MODEL, 56), F32)], axis=1).astype(BF16)
    w2t = jnp.concatenate(
        [w_in[:, _C_QA:_C_QA + 512], w_in[:, _C_VA:_C_VA + 512],
         w_in[:, _C_QI:_C_QI + 512]], axis=1).T.astype(BF16)
    hid = np.arange(WIDTH_A) // HEAD_DIM_A
    gsum = jnp.asarray(hid[:, None] == hid[None, :], BF16)
    gk = jnp.tile(k_norm_a.reshape(1, HEAD_DIM_A), (1, N_HEADS_A))
    gq = jnp.broadcast_to(q_norm_a.reshape(HEAD_DIM_A, 1), (HEAD_DIM_A, 128))

    full = lambda shape: pl.BlockSpec(shape, lambda i: (0,) * len(shape))
    t_spec = lambda rows: pl.BlockSpec((None, rows, tm), lambda i: (i // per_b, 0, i % per_b))
    return pl.pallas_call(
        _proj_a_kernel,
        grid=(n // tm,),
        in_specs=[
            pl.BlockSpec((tm, D_MODEL), lambda i: (i, 0)),
            full((1, D_MODEL)), full((D_MODEL, 640)), full((1536, D_MODEL)),
            full((WIDTH_A, WIDTH_A)), full((1, WIDTH_A)), full((HEAD_DIM_A, 128)),
        ],
        out_specs=[
            t_spec(WIDTH_A),
            pl.BlockSpec((tm, WIDTH_A), lambda i: (i, 0)),
            t_spec(WIDTH_A), t_spec(WIDTH_A),
            pl.BlockSpec((tm, 128), lambda i: (i, 0)),
            t_spec(N_IDX_HEADS),
        ],
        out_shape=[
            jax.ShapeDtypeStruct((batch, WIDTH_A, seq), BF16),
            jax.ShapeDtypeStruct((n, WIDTH_A), BF16),
            jax.ShapeDtypeStruct((batch, WIDTH_A, seq), BF16),
            jax.ShapeDtypeStruct((batch, WIDTH_A, seq), BF16),
            jax.ShapeDtypeStruct((n, 128), BF16),
            jax.ShapeDtypeStruct((batch, N_IDX_HEADS, seq), F32),
        ],
        compiler_params=pltpu.CompilerParams(
            dimension_semantics=("arbitrary",), vmem_limit_bytes=VMEM_LIMIT_BYTES),
        name="proj_a",
    )(x2, norm_g.reshape(1, D_MODEL), w1, w2t, gsum, gk, gq)


def _mem_kv_kernel(mem_ref, g_ref, w_ref, gk_ref, mk_ref, mv_ref):
    mn = (_rms_rows(mem_ref[...]) * g_ref[...]).astype(BF16)
    y = _dot(mn, w_ref[...])
    for h in range(N_HEADS_M):
        kh = y[:, h * HEAD_DIM_M:(h + 1) * HEAD_DIM_M]
        mk_ref[:, h * HEAD_DIM_M:(h + 1) * HEAD_DIM_M] = (_rms_rows(kh) * gk_ref[...]).astype(BF16)
    mv_ref[...] = y[:, WIDTH_M:].astype(BF16)


def _mem_kv(mem, mem_norm_g, w_mem_kv, k_norm_m):
    batch, m_len, _ = mem.shape
    full = lambda shape: pl.BlockSpec(shape, lambda b: (0,) * len(shape))
    return pl.pallas_call(
        _mem_kv_kernel,
        grid=(batch,),
        in_specs=[
            pl.BlockSpec((None, m_len, D_MODEL), lambda b: (b, 0, 0)),
            full((1, D_MODEL)), full((D_MODEL, 2 * WIDTH_M)), full((1, HEAD_DIM_M)),
        ],
        out_specs=[pl.BlockSpec((None, m_len, WIDTH_M), lambda b: (b, 0, 0))] * 2,
        out_shape=[jax.ShapeDtypeStruct((batch, m_len, WIDTH_M), BF16)] * 2,
        compiler_params=pltpu.CompilerParams(
            dimension_semantics=("arbitrary",), vmem_limit_bytes=VMEM_LIMIT_BYTES),
        name="mem_kv",
    )(mem, mem_norm_g.reshape(1, D_MODEL), w_mem_kv.astype(BF16), k_norm_m.reshape(1, HEAD_DIM_M))


def _alibi_slope(h):
    return float(2.0 ** (-8.0 * (h + 1) / N_HEADS_A))


def _dsa_attn_kernel(qat_ref, qit_ref, wt_ref, ka_ref, vat_ref, kib_ref, out_ref,
                     sc_ref, ot_ref, rhsi_ref, rhsa_ref, cut_ref, m_ref, d_ref, lga_ref, lgb_ref,
                     *, seq, topk):
    tq = qat_ref.shape[1]
    ck = tq
    i = pl.program_id(1)
    t0 = i * tq
    nk = i + 1
    inf = jnp.float32(jnp.inf)
    kf = jnp.float32(topk)
    t_row = t0 + lax.broadcasted_iota(jnp.int32, (1, tq), 1)

    def fold(w, op):
        return op(w.reshape(ck // FOLD_ROWS, FOLD_ROWS, w.shape[1]), axis=0)

    def over_chunks(body, init):
        def step(c, carry):
            r0 = pl.multiple_of(c * ck, ck)
            return body(sc_ref[pl.ds(r0, ck), :], r0, carry)
        return lax.fori_loop(0, nk, step, init)

    def col(acc, op):
        return op(acc, axis=0, keepdims=True)

    facc = lambda v: jnp.full((FOLD_ROWS, tq), v, F32)

    zero_rows = jnp.zeros((128 - IDX_DIM, tq), BF16)
    for h in range(N_IDX_HEADS):
        rhsi_ref[h] = jnp.concatenate(
            [qit_ref[h * IDX_DIM:(h + 1) * IDX_DIM, :], zero_rows], axis=0)
    idx_scale = (IDX_DIM ** -0.5) * (N_IDX_HEADS ** -0.5)

    def score_chunk(c, carry):
        lo_a, hi_a = carry
        r0 = pl.multiple_of(c * ck, ck)
        kc = kib_ref[pl.ds(r0, ck), :]
        acc = jnp.zeros((ck, tq), F32)
        for h in range(N_IDX_HEADS):
            d = _dot(kc, rhsi_ref[h])
            acc = acc + jnp.maximum(d, 0.0) * wt_ref[h:h + 1, :]
        acc = acc * idx_scale
        causal = r0 + lax.broadcasted_iota(jnp.int32, (ck, tq), 0) <= t_row
        sc_ref[pl.ds(r0, ck), :] = jnp.where(causal, acc, -inf)
        return (jnp.minimum(lo_a, fold(jnp.where(causal, acc, inf), jnp.min)),
                jnp.maximum(hi_a, fold(jnp.where(causal, acc, -inf), jnp.max)))

    lo_a, hi_a = lax.fori_loop(0, nk, score_chunk, (facc(inf), facc(-inf)))

    def count_ge(v):
        acc = over_chunks(lambda x, r0, a: a + fold(jnp.where(x >= v, 1.0, 0.0), jnp.sum), facc(0.0))
        return col(acc, jnp.sum)

    lo_min = col(lo_a, jnp.min)
    hi_max = col(hi_a, jnp.max)
    n_causal = (t_row + 1).astype(F32)
    short = n_causal < kf

    def bis_step(lo, hi, cl):
        mid = lo * 0.5 + hi * 0.5
        c = count_ge(mid)
        ok = c >= kf
        return jnp.where(ok, mid, lo), jnp.where(ok, hi, mid), jnp.where(ok, c, cl)

    def bis_cond(st):
        return (st[3] < BISECT_CAP) & (jnp.max(jnp.where(st[2] == kf, 0.0, 1.0)) > 0.0)

    cl0 = jnp.where(short, kf, n_causal)
    st = lax.fori_loop(0, BISECT_BLIND, lambda _, s: bis_step(*s), (lo_min, hi_max, cl0))
    lo, _, cl, _ = lax.while_loop(
        bis_cond, lambda s: bis_step(*s[:3]) + (s[3] + 1,), st + (jnp.int32(BISECT_BLIND),))

    def fix_cond(st):
        return jnp.max(st[4]) > 0.0

    def fix_body(st):
        lo, thr, cgt_f, cge_f, todo = st
        lo_e = col(over_chunks(
            lambda x, r0, a: jnp.minimum(a, fold(jnp.where(x >= lo, x, inf), jnp.min)), facc(inf)), jnp.min)

        def stats(x, r0, carry):
            cgt_a, cge_a, nxt_a = carry
            gt = x > lo_e
            return (cgt_a + fold(jnp.where(gt, 1.0, 0.0), jnp.sum),
                    cge_a + fold(jnp.where(x >= lo_e, 1.0, 0.0), jnp.sum),
                    jnp.minimum(nxt_a, fold(jnp.where(gt, x, inf), jnp.min)))

        cgt_a, cge_a, nxt_a = over_chunks(stats, (facc(0.0), facc(0.0), facc(inf)))
        cgt, cge, nxt = col(cgt_a, jnp.sum), col(cge_a, jnp.sum), col(nxt_a, jnp.min)
        active = todo > 0.0
        fin = active & (cgt < kf)
        thr = jnp.where(fin, lo_e, thr)
        cgt_f = jnp.where(fin, cgt, cgt_f)
        cge_f = jnp.where(fin, cge, cge_f)
        lo = jnp.where(active & (cgt >= kf), nxt, lo)
        todo = jnp.where(fin, 0.0, todo)
        return lo, thr, cgt_f, cge_f, todo

    todo0 = jnp.where(cl == kf, 0.0, 1.0)
    zeros = jnp.zeros((1, tq), F32)
    _, thr, cgt_f, cge_f, _ = lax.while_loop(fix_cond, fix_body, (lo, lo, zeros, zeros, todo0))
    thr = jnp.where(short, lo_min, thr)

    need = cge_f > kf
    take = kf - cgt_f
    cut_ref[...] = jnp.full((8, tq), seq, jnp.int32)

    @pl.when(jnp.max(jnp.where(need, 1.0, 0.0)) > 0.0)
    def _():
        def step(_, lh):
            lo_j, hi_j = lh
            mid = (lo_j + hi_j) >> 1

            def tied(x, r0, a):
                s_idx = r0 + lax.broadcasted_iota(jnp.int32, (ck, tq), 0)
                hit = jnp.where(x == thr, jnp.where(s_idx <= mid, 1.0, 0.0), 0.0)
                return a + fold(hit, jnp.sum)

            ge = col(over_chunks(tied, facc(0.0)), jnp.sum) >= take
            return jnp.where(ge, lo_j, mid), jnp.where(ge, mid, hi_j)

        lo_j = jnp.full((1, tq), -1, jnp.int32)
        hi_j = jnp.full((1, tq), seq - 1, jnp.int32)
        _, hi_j = lax.fori_loop(0, int(np.ceil(np.log2(seq))) + 1, step, (lo_j, hi_j))
        cut_ref[...] = jnp.broadcast_to(jnp.where(need, hi_j, seq), (8, tq))

    cut = cut_ref[0:1, :]

    def pen_chunk(c, carry):
        r0 = pl.multiple_of(c * ck, ck)
        x = sc_ref[pl.ds(r0, ck), :]
        s_idx = r0 + lax.broadcasted_iota(jnp.int32, (ck, tq), 0)
        dist = (t_row - s_idx).astype(F32)
        tie = jnp.where(s_idx <= cut, dist, inf)
        sc_ref[pl.ds(r0, ck), :] = jnp.where(x > thr, dist, jnp.where(x == thr, tie, inf))
        return carry

    lax.fori_loop(0, nk, pen_chunk, 0)

    row = lax.broadcasted_iota(jnp.int32, (128, tq), 0)
    for p in range(N_HEADS_A // 2):
        qpair = qat_ref[p * 128:(p + 1) * 128, :]
        rhsa_ref[p] = jnp.concatenate(
            [jnp.where(row < HEAD_DIM_A, qpair, jnp.zeros_like(qpair)),
             jnp.where(row >= HEAD_DIM_A, qpair, jnp.zeros_like(qpair))], axis=1)

    m_ref[...] = jnp.full((N_HEADS_A, tq), SOFTMAX_M_INIT, F32)
    d_ref[...] = jnp.zeros((N_HEADS_A, tq), F32)
    ot_ref[...] = jnp.zeros((WIDTH_A, tq), F32)

    def k_chunk(c):
        r0 = pl.multiple_of(c * ck, ck)
        return [ka_ref[pl.ds(r0, ck), p * 128:(p + 1) * 128] for p in range(N_HEADS_A // 2)]

    def qk_pair(buf, p, kc):
        buf[:, p * 2 * tq:(p + 1) * 2 * tq] = _dot(kc, rhsa_ref[p])

    def qk_into(buf, kcs):
        for p in range(N_HEADS_A // 2):
            qk_pair(buf, p, kcs[p])

    def step(cur, nxt, c, c_next):
        r0 = pl.multiple_of(c * ck, ck)
        pen = sc_ref[pl.ds(r0, ck), :]
        vts = [vat_ref[h * HEAD_DIM_A:(h + 1) * HEAD_DIM_A, pl.ds(r0, ck)]
               for h in range(N_HEADS_A)]
        kcs = k_chunk(c_next)
        qk_pair(nxt, 0, kcs[0])
        for p in range(N_HEADS_A // 2):
            m_seen = soft_pv(cur, pen, vts, (2 * p, 2 * p + 1))
            if p + 1 < N_HEADS_A // 2:
                tie = jnp.where(m_seen != m_seen, 1.0, 0.0).astype(BF16)
                qk_pair(nxt, p + 1, kcs[p + 1] + tie)

    def soft_pv(buf, pen, vts, heads):
        m_seen = None
        for h in heads:
            rows = slice(h * HEAD_DIM_A, (h + 1) * HEAD_DIM_A)
            alphas, probs = [], []
            for lt in range(tq // 128):
                ln = slice(lt * 128, (lt + 1) * 128)
                m = m_ref[h:h + 1, ln]
                lg = buf[:, h * tq + lt * 128:h * tq + (lt + 1) * 128] - _alibi_slope(h) * pen[:, ln]
                m_new = jnp.maximum(m, col(fold(lg, jnp.max), jnp.max))
                alpha = jnp.exp(m - m_new)
                pr = jnp.exp(lg - m_new)
                m_ref[h:h + 1, ln] = m_new
                d_ref[h:h + 1, ln] = d_ref[h:h + 1, ln] * alpha + col(fold(pr, jnp.sum), jnp.sum)
                alphas.append(alpha)
                probs.append(pr.astype(BF16))
                m_seen = m_new if m_seen is None else m_seen
            ot_ref[rows, :] = (ot_ref[rows, :] * jnp.concatenate(alphas, axis=1)
                               + _dot(vts[h], jnp.concatenate(probs, axis=1)))
        return m_seen

    last = nk - 1
    qk_into(lga_ref, k_chunk(0))

    def attn_pair(cc, carry):
        c0 = 2 * cc
        step(lga_ref, lgb_ref, c0, jnp.minimum(c0 + 1, last))

        @pl.when(c0 + 1 < nk)
        def _():
            step(lgb_ref, lga_ref, c0 + 1, jnp.minimum(c0 + 2, last))

        return carry

    lax.fori_loop(0, (nk + 1) // 2, attn_pair, 0)
    for h in range(N_HEADS_A):
        rows = slice(h * HEAD_DIM_A, (h + 1) * HEAD_DIM_A)
        ot_ref[rows, :] = ot_ref[rows, :] / d_ref[h:h + 1, :]
    out_ref[...] = ot_ref[...].T


def _dsa_attn(qat, qit, wt, ka, vat, kib, batch, seq):
    tq = min(TQ, seq)
    topk = min(TOPK_MAX, seq // 4)
    kern = functools.partial(_dsa_attn_kernel, seq=seq, topk=topk)
    q_spec = lambda rows: pl.BlockSpec((None, rows, tq), lambda b, i: (b, 0, i))
    return pl.pallas_call(
        kern,
        grid=(batch, seq // tq),
        in_specs=[
            q_spec(WIDTH_A), q_spec(WIDTH_A), q_spec(N_IDX_HEADS),
            pl.BlockSpec((None, seq, WIDTH_A), lambda b, i: (b, 0, 0)),
            pl.BlockSpec((None, WIDTH_A, seq), lambda b, i: (b, 0, 0)),
            pl.BlockSpec((None, seq, 128), lambda b, i: (b, 0, 0)),
        ],
        out_specs=pl.BlockSpec((None, tq, WIDTH_A), lambda b, i: (b, i, 0)),
        out_shape=jax.ShapeDtypeStruct((batch, seq, WIDTH_A), F32),
        scratch_shapes=[
            pltpu.VMEM((seq, tq), F32),
            pltpu.VMEM((WIDTH_A, tq), F32),
            pltpu.VMEM((N_IDX_HEADS, 128, tq), BF16),
            pltpu.VMEM((N_HEADS_A // 2, 128, 2 * tq), BF16),
            pltpu.VMEM((8, tq), jnp.int32),
            pltpu.VMEM((N_HEADS_A, tq), F32),
            pltpu.VMEM((N_HEADS_A, tq), F32),
            pltpu.VMEM((tq, N_HEADS_A * tq), F32),
            pltpu.VMEM((tq, N_HEADS_A * tq), F32),
        ],
        compiler_params=pltpu.CompilerParams(
            dimension_semantics=("arbitrary", "arbitrary"), vmem_limit_bytes=VMEM_LIMIT_BYTES),
        name="dsa_attn",
    )(qat, qit, wt, ka.reshape(batch, seq, WIDTH_A), vat, kib.reshape(batch, seq, 128))


_M_ZA, _M_BG, _M_CG, _M_HB, _M_ZB, _M_QM, _M_ZM, _M_G = (
    0, 512, 1024, 1536, 2048, 2560, 3072, 3584)
_M_COLS = 3584 + N_BRANCHES * D_MODEL


def _merge_kernel(x_ref, attn_ref, g_ref, wc_ref, bg_ref, cw_ref, mk_ref, mv_ref,
                  gqm_ref, wa_ref, wb_ref, wm_ref, wo_ref, out_ref, utail_ref, *, per_b):
    tm = x_ref.shape[0]
    xf = x_ref[...]
    xn = (_rms_rows(xf) * g_ref[...]).astype(BF16)

    def proj(lhs, c0, width):
        return _dot(lhs, wc_ref[:, c0:c0 + width])

    za = proj(xn, _M_ZA, WIDTH_A)
    ya = _dot((attn_ref[...] * jax.nn.silu(za)).astype(BF16), wa_ref[...])

    u = proj(xn, _M_CG, WIDTH_B) * proj(xn, _M_HB, WIDTH_B)
    @pl.when((pl.program_id(0) % per_b) == 0)
    def _():
        utail_ref[...] = jnp.zeros((8, WIDTH_B), F32)

    uh = utail_ref[...]
    utail_ref[...] = u[tm - 8:, :]
    rows = lax.broadcasted_iota(jnp.int32, (tm, WIDTH_B), 0)
    u1 = jnp.where(rows == 0, uh[7:8, :], pltpu.roll(u, 1, 0))
    u2 = jnp.where(rows == 0, uh[6:7, :], jnp.where(rows == 1, uh[7:8, :], pltpu.roll(u, 2, 0)))
    conv = cw_ref[0:1, :] * u2 + cw_ref[1:2, :] * u1 + cw_ref[2:3, :] * u
    bgate = proj(xn, _M_BG, WIDTH_B)
    zb = proj(xn, _M_ZB, WIDTH_B)
    yb = _dot(((bgate * conv) * jax.nn.silu(zb)).astype(BF16), wb_ref[...])

    qm = proj(xn, _M_QM, WIDTH_M)
    zm = proj(xn, _M_ZM, WIDTH_M)
    heads = []
    for h in range(N_HEADS_M):
        sl = slice(h * HEAD_DIM_M, (h + 1) * HEAD_DIM_M)
        qh = (_rms_rows(qm[:, sl]) * gqm_ref[...]).astype(BF16)
        lg = _dot_nt(qh, mk_ref[:, sl]) * (HEAD_DIM_M ** -0.5)
        e = jnp.exp(lg - jnp.max(lg, axis=-1, keepdims=True))
        p = e / jnp.sum(e, axis=-1, keepdims=True)
        heads.append(_dot(p.astype(BF16), mv_ref[:, sl]))
    attn_m = jnp.concatenate(heads, axis=1)
    ym = _dot((attn_m * jax.nn.silu(zm)).astype(BF16), wm_ref[...])

    def gate(j):
        gj = proj(xn, _M_G + j * D_MODEL, D_MODEL) + bg_ref[:, j * D_MODEL:(j + 1) * D_MODEL]
        return jax.nn.sigmoid(gj)

    merged = gate(0) * ya + gate(1) * yb + gate(2) * ym
    out_ref[...] = xf + _dot(merged.astype(BF16), wo_ref[...])


def _merge(x2, attn2, mk, mv, norm_g, w_in, b_gate, conv_w, q_norm_m,
           w_out_a, w_out_b, w_out_m, w_o, batch, seq):
    n = x2.shape[0]
    tm = min(TM_C, seq)
    per_b = seq // tm
    m_len = mk.shape[1]
    wc = jnp.concatenate([w_in[:, _C_ZA:_C_ZA + 512], w_in[:, _C_REST:]], axis=1).astype(BF16)
    const = lambda shape: pl.BlockSpec(shape, lambda i: (0,) * len(shape),
                                       pipeline_mode=pl.Buffered(1))
    kern = functools.partial(_merge_kernel, per_b=per_b)
    return pl.pallas_call(
        kern,
        grid=(n // tm,),
        in_specs=[
            pl.BlockSpec((tm, D_MODEL), lambda i: (i, 0)),
            pl.BlockSpec((tm, WIDTH_A), lambda i: (i, 0)),
            const((1, D_MODEL)), const((D_MODEL, _M_COLS)), const((1, N_BRANCHES * D_MODEL)),
            const((CONV_WIDTH, WIDTH_B)),
            pl.BlockSpec((None, m_len, WIDTH_M), lambda i: (i // per_b, 0, 0)),
            pl.BlockSpec((None, m_len, WIDTH_M), lambda i: (i // per_b, 0, 0)),
            const((1, HEAD_DIM_M)),
            const((WIDTH_A, D_MODEL)), const((WIDTH_B, D_MODEL)), const((WIDTH_M, D_MODEL)),
            const((D_MODEL, D_MODEL)),
        ],
        out_specs=pl.BlockSpec((tm, D_MODEL), lambda i: (i, 0)),
        out_shape=jax.ShapeDtypeStruct((n, D_MODEL), F32),
        scratch_shapes=[pltpu.VMEM((8, WIDTH_B), F32)],
        compiler_params=pltpu.CompilerParams(
            dimension_semantics=("arbitrary",), vmem_limit_bytes=VMEM_LIMIT_BYTES),
        name="merge",
    )(x2, attn2, norm_g.reshape(1, D_MODEL), wc, b_gate.reshape(1, -1), conv_w,
      mk, mv, q_norm_m.reshape(1, HEAD_DIM_M),
      w_out_a.astype(BF16), w_out_b.astype(BF16), w_out_m.astype(BF16), w_o.astype(BF16))


def _layer(h, mem, norm_g, mem_norm_g, w_in, b_gate, w_mem_kv, q_norm_a, k_norm_a,
           q_norm_m, k_norm_m, conv_w, w_out_a, w_out_b, w_out_m, w_o):
    batch, seq, _ = h.shape
    x2 = h.reshape(batch * seq, D_MODEL)
    qat, ka, vat, qit, kib, wt = _proj_a(x2, norm_g, w_in, q_norm_a, k_norm_a, batch, seq)
    mk, mv = _mem_kv(mem, mem_norm_g, w_mem_kv, k_norm_m)
    attn = _dsa_attn(qat, qit, wt, ka, vat, kib, batch, seq)
    out = _merge(x2, attn.reshape(batch * seq, WIDTH_A), mk, mv, norm_g, w_in, b_gate, conv_w,
                 q_norm_m, w_out_a, w_out_b, w_out_m, w_o, batch, seq)
    return out.reshape(batch, seq, D_MODEL)


def kernel(x, mem, norm_g, mem_norm_g, w_in, b_gate, w_mem_kv, q_norm_a, k_norm_a,
           q_norm_m, k_norm_m, conv_w, w_out_a, w_out_b, w_out_m, w_o):
    h = x
    for l in range(norm_g.shape[0]):
        h = _layer(h, mem, norm_g[l], mem_norm_g[l], w_in[l], b_gate[l], w_mem_kv[l],
                   q_norm_a[l], k_norm_a[l], q_norm_m[l], k_norm_m[l], conv_w[l],
                   w_out_a[l], w_out_b[l], w_out_m[l], w_o[l])
    return h
```

```python
import functools

import jax
import jax.numpy as jnp
import numpy as np
from jax import lax
from jax.experimental import pallas as pl
from jax.experimental.pallas import tpu as pltpu

F32 = jnp.float32
BF16 = jnp.bfloat16

D_MODEL = 1024
N_HEADS_A = 8
HEAD_DIM_A = 64
WIDTH_A = 512
N_IDX_HEADS = 8
IDX_DIM = 64
TOPK_MAX = 256
WIDTH_B = 512
CONV_WIDTH = 3
N_HEADS_M = 4
HEAD_DIM_M = 128
WIDTH_M = 512
N_BRANCHES = 3
RMS_EPS = 1e-6

_C_QA, _C_KA, _C_VA, _C_ZA = 0, 512, 1024, 1536
_C_QI, _C_KI, _C_WI = 2048, 2560, 2624
_C_REST = 2632
_D_IN = 8776

VMEM_LIMIT_BYTES = 56 * 1024 * 1024

TM_A = 512
TM_C = 256
TQ = 256
FOLD_ROWS = 32
BISECT_STEPS = 22
SOFTMAX_M_INIT = -1e30


def _rms_rows(xf, eps=RMS_EPS):
    return xf * lax.rsqrt(jnp.mean(xf * xf, axis=-1, keepdims=True) + eps)


def _dot(a, b):
    return jnp.dot(a, b, preferred_element_type=F32)


def _dot_nt(a, b):
    return lax.dot_general(a, b, (((1,), (1,)), ((), ())), preferred_element_type=F32)


def _proj_a_kernel(x_ref, g_ref, w1_ref, w2t_ref, gsum_ref, gk_ref, gq_ref,
                   qat_ref, ka_ref, vat_ref, qit_ref, kib_ref, wt_ref):
    tm = x_ref.shape[0]
    xn = (_rms_rows(x_ref[...]) * g_ref[...]).astype(BF16)

    y1 = _dot(xn, w1_ref[...])
    ka_raw = y1[:, :WIDTH_A]
    sq = ka_raw * ka_raw
    sq_hi = sq.astype(BF16)
    sq_lo = (sq - sq_hi.astype(F32)).astype(BF16)
    ss = _dot(sq_hi, gsum_ref[...]) + _dot(sq_lo, gsum_ref[...])
    ka = ka_raw * lax.rsqrt(ss * (1.0 / HEAD_DIM_A) + RMS_EPS) * gk_ref[...]
    ka_ref[...] = ka.astype(BF16)
    kiwi = y1[:, WIDTH_A:]
    lane = lax.broadcasted_iota(jnp.int32, kiwi.shape, 1)
    kib_ref[...] = jnp.where(lane < IDX_DIM, kiwi, 0.0).astype(BF16)
    wt_ref[...] = kiwi.T[IDX_DIM:IDX_DIM + N_IDX_HEADS, :]

    yt = _dot_nt(w2t_ref[...], xn)
    gq = jnp.concatenate([gq_ref[...]] * (tm // 128), axis=1)
    for h in range(N_HEADS_A):
        qh = yt[h * HEAD_DIM_A:(h + 1) * HEAD_DIM_A, :]
        ms = jnp.mean(qh * qh, axis=0, keepdims=True)
        qn = qh * lax.rsqrt(ms + RMS_EPS) * gq
        qat_ref[h * HEAD_DIM_A:(h + 1) * HEAD_DIM_A, :] = (qn * (HEAD_DIM_A ** -0.5)).astype(BF16)
    vat_ref[...] = yt[WIDTH_A:2 * WIDTH_A, :].astype(BF16)
    qit_ref[...] = yt[2 * WIDTH_A:3 * WIDTH_A, :].astype(BF16)


def _proj_a(x2, norm_g, w_in, q_norm_a, k_norm_a, batch, seq):
    n = x2.shape[0]
    tm = min(TM_A, seq)
    per_b = seq // tm
    w1 = jnp.concatenate(
        [w_in[:, _C_KA:_C_KA + 512], w_in[:, _C_KI:_C_KI + 72],
         jnp.zeros((D_MODEL, 56), F32)], axis=1).astype(BF16)
    w2t = jnp.concatenate(
        [w_in[:, _C_QA:_C_QA + 512], w_in[:, _C_VA:_C_VA + 512],
         w_in[:, _C_QI:_C_QI + 512]], axis=1).T.astype(BF16)
    hid = np.arange(WIDTH_A) // HEAD_DIM_A
    gsum = jnp.asarray(hid[:, None] == hid[None, :], BF16)
    gk = jnp.tile(k_norm_a.reshape(1, HEAD_DIM_A), (1, N_HEADS_A))
    gq = jnp.broadcast_to(q_norm_a.reshape(HEAD_DIM_A, 1), (HEAD_DIM_A, 128))

    full = lambda shape: pl.BlockSpec(shape, lambda i: (0,) * len(shape))
    t_spec = lambda rows: pl.BlockSpec((None, rows, tm), lambda i: (i // per_b, 0, i % per_b))
    return pl.pallas_call(
        _proj_a_kernel,
        grid=(n // tm,),
        in_specs=[
            pl.BlockSpec((tm, D_MODEL), lambda i: (i, 0)),
            full((1, D_MODEL)), full((D_MODEL, 640)), full((1536, D_MODEL)),
            full((WIDTH_A, WIDTH_A)), full((1, WIDTH_A)), full((HEAD_DIM_A, 128)),
        ],
        out_specs=[
            t_spec(WIDTH_A),
            pl.BlockSpec((tm, WIDTH_A), lambda i: (i, 0)),
            t_spec(WIDTH_A), t_spec(WIDTH_A),
            pl.BlockSpec((tm, 128), lambda i: (i, 0)),
            t_spec(N_IDX_HEADS),
        ],
        out_shape=[
            jax.ShapeDtypeStruct((batch, WIDTH_A, seq), BF16),
            jax.ShapeDtypeStruct((n, WIDTH_A), BF16),
            jax.ShapeDtypeStruct((batch, WIDTH_A, seq), BF16),
            jax.ShapeDtypeStruct((batch, WIDTH_A, seq), BF16),
            jax.ShapeDtypeStruct((n, 128), BF16),
            jax.ShapeDtypeStruct((batch, N_IDX_HEADS, seq), F32),
        ],
        compiler_params=pltpu.CompilerParams(
            dimension_semantics=("arbitrary",), vmem_limit_bytes=VMEM_LIMIT_BYTES),
        name="proj_a",
    )(x2, norm_g.reshape(1, D_MODEL), w1, w2t, gsum, gk, gq)


def _mem_kv_kernel(mem_ref, g_ref, w_ref, gk_ref, mk_ref, mv_ref):
    mn = (_rms_rows(mem_ref[...]) * g_ref[...]).astype(BF16)
    y = _dot(mn, w_ref[...])
    for h in range(N_HEADS_M):
        kh = y[:, h * HEAD_DIM_M:(h + 1) * HEAD_DIM_M]
        mk_ref[:, h * HEAD_DIM_M:(h + 1) * HEAD_DIM_M] = (_rms_rows(kh) * gk_ref[...]).astype(BF16)
    mv_ref[...] = y[:, WIDTH_M:].astype(BF16)


def _mem_kv(mem, mem_norm_g, w_mem_kv, k_norm_m):
    batch, m_len, _ = mem.shape
    full = lambda shape: pl.BlockSpec(shape, lambda b: (0,) * len(shape))
    return pl.pallas_call(
        _mem_kv_kernel,
        grid=(batch,),
        in_specs=[
            pl.BlockSpec((None, m_len, D_MODEL), lambda b: (b, 0, 0)),
            full((1, D_MODEL)), full((D_MODEL, 2 * WIDTH_M)), full((1, HEAD_DIM_M)),
        ],
        out_specs=[pl.BlockSpec((None, m_len, WIDTH_M), lambda b: (b, 0, 0))] * 2,
        out_shape=[jax.ShapeDtypeStruct((batch, m_len, WIDTH_M), BF16)] * 2,
        compiler_params=pltpu.CompilerParams(
            dimension_semantics=("arbitrary",), vmem_limit_bytes=VMEM_LIMIT_BYTES),
        name="mem_kv",
    )(mem, mem_norm_g.reshape(1, D_MODEL), w_mem_kv.astype(BF16), k_norm_m.reshape(1, HEAD_DIM_M))


def _alibi_slope(h):
    return float(2.0 ** (-8.0 * (h + 1) / N_HEADS_A))


def _dsa_attn_kernel(qat_ref, qit_ref, wt_ref, ka_ref, vat_ref, kib_ref, out_ref,
                     sc_ref, ot_ref, rhsi_ref, rhsa_ref, tri_ref, m_ref, d_ref, lga_ref, lgb_ref,
                     *, seq, topk):
    tq = qat_ref.shape[1]
    ck = tq
    i = pl.program_id(1)
    t0 = i * tq
    nk = i + 1
    inf = jnp.float32(jnp.inf)
    kf = jnp.float32(topk)
    t_row = t0 + lax.broadcasted_iota(jnp.int32, (1, tq), 1)

    def fold(w, op):
        return op(w.reshape(ck // FOLD_ROWS, FOLD_ROWS, w.shape[1]), axis=0)

    def over_chunks(body, init):
        def step(c, carry):
            r0 = pl.multiple_of(c * ck, ck)
            return body(sc_ref[pl.ds(r0, ck), :], r0, carry)
        return lax.fori_loop(0, nk, step, init)

    def col(acc, op):
        return op(acc, axis=0, keepdims=True)

    facc = lambda v: jnp.full((FOLD_ROWS, tq), v, F32)

    zero_rows = jnp.zeros((128 - IDX_DIM, tq), BF16)
    for h in range(N_IDX_HEADS):
        rhsi_ref[h] = jnp.concatenate(
            [qit_ref[h * IDX_DIM:(h + 1) * IDX_DIM, :], zero_rows], axis=0)
    idx_scale = (IDX_DIM ** -0.5) * (N_IDX_HEADS ** -0.5)

    def score_chunk(c, carry):
        lo_a, hi_a = carry
        r0 = pl.multiple_of(c * ck, ck)
        kc = kib_ref[pl.ds(r0, ck), :]
        acc = jnp.zeros((ck, tq), F32)
        for h in range(N_IDX_HEADS):
            d = _dot(kc, rhsi_ref[h])
            acc = acc + jnp.maximum(d, 0.0) * wt_ref[h:h + 1, :]
        acc = acc * idx_scale
        causal = r0 + lax.broadcasted_iota(jnp.int32, (ck, tq), 0) <= t_row
        sc_ref[pl.ds(r0, ck), :] = jnp.where(causal, acc, -inf)
        return (jnp.minimum(lo_a, fold(jnp.where(causal, acc, inf), jnp.min)),
                jnp.maximum(hi_a, fold(jnp.where(causal, acc, -inf), jnp.max)))

    lo_a, hi_a = lax.fori_loop(0, nk, score_chunk, (facc(inf), facc(-inf)))

    def count_ge(v):
        acc = over_chunks(lambda x, r0, a: a + fold(jnp.where(x >= v, 1.0, 0.0), jnp.sum), facc(0.0))
        return col(acc, jnp.sum)

    lo_min = col(lo_a, jnp.min)
    hi_max = col(hi_a, jnp.max)
    n_causal = (t_row + 1).astype(F32)
    short = n_causal < kf

    def bis_step(lo, hi, cl):
        mid = lo * 0.5 + hi * 0.5
        c = count_ge(mid)
        ok = c >= kf
        return jnp.where(ok, mid, lo), jnp.where(ok, hi, mid), jnp.where(ok, c, cl)

    cl0 = jnp.where(short, kf, n_causal)
    lo, _, cl = lax.fori_loop(0, BISECT_STEPS, lambda _, s: bis_step(*s), (lo_min, hi_max, cl0))

    def walk_cond(st):
        return jnp.max(st[3]) > 0.0

    def walk_body(st):
        lo, thr, take, todo = st
        lo_e = col(over_chunks(
            lambda x, r0, a: jnp.minimum(a, fold(jnp.where(x >= lo, x, inf), jnp.min)), facc(inf)), jnp.min)

        def above(x, r0, carry):
            cgt_a, nxt_a = carry
            gt = x > lo_e
            return (cgt_a + fold(jnp.where(gt, 1.0, 0.0), jnp.sum),
                    jnp.minimum(nxt_a, fold(jnp.where(gt, x, inf), jnp.min)))

        cgt_a, nxt_a = over_chunks(above, (facc(0.0), facc(inf)))
        cgt, nxt = col(cgt_a, jnp.sum), col(nxt_a, jnp.min)
        active = todo > 0.0
        fin = active & (cgt < kf)
        thr = jnp.where(fin, lo_e, thr)
        take = jnp.where(fin, kf - cgt, take)
        lo = jnp.where(active & (cgt >= kf), nxt, lo)
        todo = jnp.where(fin, 0.0, todo)
        return lo, thr, take, todo

    thr0 = jnp.where(short, lo_min, lo)
    take0 = jnp.full((1, tq), float(seq), F32)
    todo0 = jnp.where(cl == kf, 0.0, 1.0)
    _, thr, take, _ = lax.while_loop(walk_cond, walk_body, (lo, thr0, take0, todo0))

    def alibi_dist(r0):
        s_idx = r0 + lax.broadcasted_iota(jnp.int32, (ck, tq), 0)
        return (t_row - s_idx).astype(F32)

    some_tie = jnp.max(jnp.where(take < float(seq), 1.0, 0.0)) > 0.0

    @pl.when(some_tie)
    def _():
        r_i = lax.broadcasted_iota(jnp.int32, (ck, ck), 0)
        c_i = lax.broadcasted_iota(jnp.int32, (ck, ck), 1)
        tri_ref[...] = jnp.where(c_i <= r_i, 1.0, 0.0).astype(BF16)

        def pen_chunk(c, ties_before):
            r0 = pl.multiple_of(c * ck, ck)
            x = sc_ref[pl.ds(r0, ck), :]
            dist = alibi_dist(r0)
            eq = x == thr
            rank = _dot(tri_ref[...], jnp.where(eq, 1.0, 0.0).astype(BF16)) + ties_before
            tie = jnp.where(eq, jnp.where(rank <= take, dist, inf), inf)
            sc_ref[pl.ds(r0, ck), :] = jnp.where(x > thr, dist, tie)
            return rank[ck - 1:ck, :]

        lax.fori_loop(0, nk, pen_chunk, jnp.zeros((1, tq), F32))

    @pl.when(jnp.logical_not(some_tie))
    def _():
        def pen_chunk(c, carry):
            r0 = pl.multiple_of(c * ck, ck)
            x = sc_ref[pl.ds(r0, ck), :]
            sc_ref[pl.ds(r0, ck), :] = jnp.where(x >= thr, alibi_dist(r0), inf)
            return carry

        lax.fori_loop(0, nk, pen_chunk, 0)

    row = lax.broadcasted_iota(jnp.int32, (128, tq), 0)
    for p in range(N_HEADS_A // 2):
        qpair = qat_ref[p * 128:(p + 1) * 128, :]
        rhsa_ref[p] = jnp.concatenate(
            [jnp.where(row < HEAD_DIM_A, qpair, jnp.zeros_like(qpair)),
             jnp.where(row >= HEAD_DIM_A, qpair, jnp.zeros_like(qpair))], axis=1)

    m_ref[...] = jnp.full((N_HEADS_A, tq), SOFTMAX_M_INIT, F32)
    d_ref[...] = jnp.zeros((N_HEADS_A, tq), F32)
    ot_ref[...] = jnp.zeros((WIDTH_A, tq), F32)

    def k_chunk(c):
        r0 = pl.multiple_of(c * ck, ck)
        return [ka_ref[pl.ds(r0, ck), p * 128:(p + 1) * 128] for p in range(N_HEADS_A // 2)]

    def qk_pair(buf, p, kc):
        buf[:, p * 2 * tq:(p + 1) * 2 * tq] = _dot(kc, rhsa_ref[p])

    def qk_into(buf, kcs):
        for p in range(N_HEADS_A // 2):
            qk_pair(buf, p, kcs[p])

    def step(cur, nxt, c, c_next):
        r0 = pl.multiple_of(c * ck, ck)
        pen = sc_ref[pl.ds(r0, ck), :]
        vts = [vat_ref[h * HEAD_DIM_A:(h + 1) * HEAD_DIM_A, pl.ds(r0, ck)]
               for h in range(N_HEADS_A)]
        kcs = k_chunk(c_next)
        qk_pair(nxt, 0, kcs[0])
        for p in range(N_HEADS_A // 2):
            m_seen = soft_pv(cur, pen, vts, (2 * p, 2 * p + 1))
            if p + 1 < N_HEADS_A // 2:
                tie = jnp.where(m_seen != m_seen, 1.0, 0.0).astype(BF16)
                qk_pair(nxt, p + 1, kcs[p + 1] + tie)

    def soft_pv(buf, pen, vts, heads):
        m_seen = None
        for h in heads:
            rows = slice(h * HEAD_DIM_A, (h + 1) * HEAD_DIM_A)
            alphas, probs = [], []
            for lt in range(tq // 128):
                ln = slice(lt * 128, (lt + 1) * 128)
                m = m_ref[h:h + 1, ln]
                lg = buf[:, h * tq + lt * 128:h * tq + (lt + 1) * 128] - _alibi_slope(h) * pen[:, ln]
                m_new = jnp.maximum(m, col(fold(lg, jnp.max), jnp.max))
                alpha = jnp.exp(m - m_new)
                pr = jnp.exp(lg - m_new)
                m_ref[h:h + 1, ln] = m_new
                d_ref[h:h + 1, ln] = d_ref[h:h + 1, ln] * alpha + col(fold(pr, jnp.sum), jnp.sum)
                alphas.append(alpha)
                probs.append(pr.astype(BF16))
                m_seen = m_new if m_seen is None else m_seen
            ot_ref[rows, :] = (ot_ref[rows, :] * jnp.concatenate(alphas, axis=1)
                               + _dot(vts[h], jnp.concatenate(probs, axis=1)))
        return m_seen

    last = nk - 1
    qk_into(lga_ref, k_chunk(0))

    def attn_pair(cc, carry):
        c0 = 2 * cc
        step(lga_ref, lgb_ref, c0, jnp.minimum(c0 + 1, last))

        @pl.when(c0 + 1 < nk)
        def _():
            step(lgb_ref, lga_ref, c0 + 1, jnp.minimum(c0 + 2, last))

        return carry

    lax.fori_loop(0, (nk + 1) // 2, attn_pair, 0)
    for h in range(N_HEADS_A):
        rows = slice(h * HEAD_DIM_A, (h + 1) * HEAD_DIM_A)
        ot_ref[rows, :] = ot_ref[rows, :] / d_ref[h:h + 1, :]
    out_ref[...] = ot_ref[...].T


def _dsa_attn(qat, qit, wt, ka, vat, kib, batch, seq):
    tq = min(TQ, seq)
    topk = min(TOPK_MAX, seq // 4)
    kern = functools.partial(_dsa_attn_kernel, seq=seq, topk=topk)
    q_spec = lambda rows: pl.BlockSpec((None, rows, tq), lambda b, i: (b, 0, i))
    return pl.pallas_call(
        kern,
        grid=(batch, seq // tq),
        in_specs=[
            q_spec(WIDTH_A), q_spec(WIDTH_A), q_spec(N_IDX_HEADS),
            pl.BlockSpec((None, seq, WIDTH_A), lambda b, i: (b, 0, 0)),
            pl.BlockSpec((None, WIDTH_A, seq), lambda b, i: (b, 0, 0)),
            pl.BlockSpec((None, seq, 128), lambda b, i: (b, 0, 0)),
        ],
        out_specs=pl.BlockSpec((None, tq, WIDTH_A), lambda b, i: (b, i, 0)),
        out_shape=jax.ShapeDtypeStruct((batch, seq, WIDTH_A), F32),
        scratch_shapes=[
            pltpu.VMEM((seq, tq), F32),
            pltpu.VMEM((WIDTH_A, tq), F32),
            pltpu.VMEM((N_IDX_HEADS, 128, tq), BF16),
            pltpu.VMEM((N_HEADS_A // 2, 128, 2 * tq), BF16),
            pltpu.VMEM((tq, tq), BF16),
            pltpu.VMEM((N_HEADS_A, tq), F32),
            pltpu.VMEM((N_HEADS_A, tq), F32),
            pltpu.VMEM((tq, N_HEADS_A * tq), F32),
            pltpu.VMEM((tq, N_HEADS_A * tq), F32),
        ],
        compiler_params=pltpu.CompilerParams(
            dimension_semantics=("arbitrary", "arbitrary"), vmem_limit_bytes=VMEM_LIMIT_BYTES),
        name="dsa_attn",
    )(qat, qit, wt, ka.reshape(batch, seq, WIDTH_A), vat, kib.reshape(batch, seq, 128))


_M_ZA, _M_BG, _M_CG, _M_HB, _M_ZB, _M_QM, _M_ZM, _M_G = (
    0, 512, 1024, 1536, 2048, 2560, 3072, 3584)
_M_COLS = 3584 + N_BRANCHES * D_MODEL


def _merge_kernel(x_ref, attn_ref, g_ref, wc_ref, bg_ref, cw_ref, mk_ref, mv_ref,
                  gqm_ref, wa_ref, wb_ref, wm_ref, wo_ref, out_ref, utail_ref, *, per_b):
    tm = x_ref.shape[0]
    xf = x_ref[...]
    xn = (_rms_rows(xf) * g_ref[...]).astype(BF16)

    def proj(lhs, c0, width):
        return _dot(lhs, wc_ref[:, c0:c0 + width])

    za = proj(xn, _M_ZA, WIDTH_A)
    ya = _dot((attn_ref[...] * jax.nn.silu(za)).astype(BF16), wa_ref[...])

    u = proj(xn, _M_CG, WIDTH_B) * proj(xn, _M_HB, WIDTH_B)
    @pl.when((pl.program_id(0) % per_b) == 0)
    def _():
        utail_ref[...] = jnp.zeros((8, WIDTH_B), F32)

    uh = utail_ref[...]
    utail_ref[...] = u[tm - 8:, :]
    rows = lax.broadcasted_iota(jnp.int32, (tm, WIDTH_B), 0)
    u1 = jnp.where(rows == 0, uh[7:8, :], pltpu.roll(u, 1, 0))
    u2 = jnp.where(rows == 0, uh[6:7, :], jnp.where(rows == 1, uh[7:8, :], pltpu.roll(u, 2, 0)))
    conv = cw_ref[0:1, :] * u2 + cw_ref[1:2, :] * u1 + cw_ref[2:3, :] * u
    bgate = proj(xn, _M_BG, WIDTH_B)
    zb = proj(xn, _M_ZB, WIDTH_B)
    yb = _dot(((bgate * conv) * jax.nn.silu(zb)).astype(BF16), wb_ref[...])

    qm = proj(xn, _M_QM, WIDTH_M)
    zm = proj(xn, _M_ZM, WIDTH_M)
    heads = []
    for h in range(N_HEADS_M):
        sl = slice(h * HEAD_DIM_M, (h + 1) * HEAD_DIM_M)
        qh = (_rms_rows(qm[:, sl]) * gqm_ref[...]).astype(BF16)
        lg = _dot_nt(qh, mk_ref[:, sl]) * (HEAD_DIM_M ** -0.5)
        e = jnp.exp(lg - jnp.max(lg, axis=-1, keepdims=True))
        p = e / jnp.sum(e, axis=-1, keepdims=True)
        heads.append(_dot(p.astype(BF16), mv_ref[:, sl]))
    attn_m = jnp.concatenate(heads, axis=1)
    ym = _dot((attn_m * jax.nn.silu(zm)).astype(BF16), wm_ref[...])

    def gate(j):
        gj = proj(xn, _M_G + j * D_MODEL, D_MODEL) + bg_ref[:, j * D_MODEL:(j + 1) * D_MODEL]
        return jax.nn.sigmoid(gj)

    merged = gate(0) * ya + gate(1) * yb + gate(2) * ym
    out_ref[...] = xf + _dot(merged.astype(BF16), wo_ref[...])


def _merge(x2, attn2, mk, mv, norm_g, w_in, b_gate, conv_w, q_norm_m,
           w_out_a, w_out_b, w_out_m, w_o, batch, seq):
    n = x2.shape[0]
    tm = min(TM_C, seq)
    per_b = seq // tm
    m_len = mk.shape[1]
    wc = jnp.concatenate([w_in[:, _C_ZA:_C_ZA + 512], w_in[:, _C_REST:]], axis=1).astype(BF16)
    const = lambda shape: pl.BlockSpec(shape, lambda i: (0,) * len(shape),
                                       pipeline_mode=pl.Buffered(1))
    kern = functools.partial(_merge_kernel, per_b=per_b)
    return pl.pallas_call(
        kern,
        grid=(n // tm,),
        in_specs=[
            pl.BlockSpec((tm, D_MODEL), lambda i: (i, 0)),
            pl.BlockSpec((tm, WIDTH_A), lambda i: (i, 0)),
            const((1, D_MODEL)), const((D_MODEL, _M_COLS)), const((1, N_BRANCHES * D_MODEL)),
            const((CONV_WIDTH, WIDTH_B)),
            pl.BlockSpec((None, m_len, WIDTH_M), lambda i: (i // per_b, 0, 0)),
            pl.BlockSpec((None, m_len, WIDTH_M), lambda i: (i // per_b, 0, 0)),
            const((1, HEAD_DIM_M)),
            const((WIDTH_A, D_MODEL)), const((WIDTH_B, D_MODEL)), const((WIDTH_M, D_MODEL)),
            const((D_MODEL, D_MODEL)),
        ],
        out_specs=pl.BlockSpec((tm, D_MODEL), lambda i: (i, 0)),
        out_shape=jax.ShapeDtypeStruct((n, D_MODEL), F32),
        scratch_shapes=[pltpu.VMEM((8, WIDTH_B), F32)],
        compiler_params=pltpu.CompilerParams(
            dimension_semantics=("arbitrary",), vmem_limit_bytes=VMEM_LIMIT_BYTES),
        name="merge",
    )(x2, attn2, norm_g.reshape(1, D_MODEL), wc, b_gate.reshape(1, -1), conv_w,
      mk, mv, q_norm_m.reshape(1, HEAD_DIM_M),
      w_out_a.astype(BF16), w_out_b.astype(BF16), w_out_m.astype(BF16), w_o.astype(BF16))


def _layer(h, mem, norm_g, mem_norm_g, w_in, b_gate, w_mem_kv, q_norm_a, k_norm_a,
           q_norm_m, k_norm_m, conv_w, w_out_a, w_out_b, w_out_m, w_o):
    batch, seq, _ = h.shape
    x2 = h.reshape(batch * seq, D_MODEL)
    qat, ka, vat, qit, kib, wt = _proj_a(x2, norm_g, w_in, q_norm_a, k_norm_a, batch, seq)
    mk, mv = _mem_kv(mem, mem_norm_g, w_mem_kv, k_norm_m)
    attn = _dsa_attn(qat, qit, wt, ka, vat, kib, batch, seq)
    out = _merge(x2, attn.reshape(batch * seq, WIDTH_A), mk, mv, norm_g, w_in, b_gate, conv_w,
                 q_norm_m, w_out_a, w_out_b, w_out_m, w_o, batch, seq)
    return out.reshape(batch, seq, D_MODEL)


def kernel(x, mem, norm_g, mem_norm_g, w_in, b_gate, w_mem_kv, q_norm_a, k_norm_a,
           q_norm_m, k_norm_m, conv_w, w_out_a, w_out_b, w_out_m, w_o):
    h = x
    for l in range(norm_g.shape[0]):
        h = _layer(h, mem, norm_g[l], mem_norm_g[l], w_in[l], b_gate[l], w_mem_kv[l],
                   q_norm_a[l], k_norm_a[l], q_norm_m[l], k_norm_m[l], conv_w[l],
                   w_out_a[l], w_out_b[l], w_out_m[l], w_o[l])
    return h
```

```python
import functools

import jax
import jax.numpy as jnp
import numpy as np
from jax import lax
from jax.experimental import pallas as pl
from jax.experimental.pallas import tpu as pltpu

F32 = jnp.float32
BF16 = jnp.bfloat16

D_MODEL = 1024
N_HEADS_A = 8
HEAD_DIM_A = 64
WIDTH_A = 512
N_IDX_HEADS = 8
IDX_DIM = 64
TOPK_MAX = 256
WIDTH_B = 512
CONV_WIDTH = 3
N_HEADS_M = 4
HEAD_DIM_M = 128
WIDTH_M = 512
N_BRANCHES = 3
RMS_EPS = 1e-6

_C_QA, _C_KA, _C_VA, _C_ZA = 0, 512, 1024, 1536
_C_QI, _C_KI, _C_WI = 2048, 2560, 2624
_C_REST = 2632
_D_IN = 8776

VMEM_LIMIT_BYTES = 56 * 1024 * 1024

TM_A = 512
TM_C = 512
TQ = 256
FOLD_ROWS = 32
BISECT_STEPS = 16
SOFTMAX_DIRECT_BOUND = 60.0
SOFTMAX_M_INIT = -1e30


def _rms_rows(xf, eps=RMS_EPS):
    return xf * lax.rsqrt(jnp.mean(xf * xf, axis=-1, keepdims=True) + eps)


def _dot(a, b):
    return jnp.dot(a, b, preferred_element_type=F32)


def _dot_nt(a, b):
    return lax.dot_general(a, b, (((1,), (1,)), ((), ())), preferred_element_type=F32)


def _proj_a_kernel(x_ref, g_ref, w1_ref, w2t_ref, gsum_ref, gk_ref, gq_ref,
                   qat_ref, ka_ref, vat_ref, qit_ref, kib_ref, wt_ref):
    tm = x_ref.shape[0]
    xn = (_rms_rows(x_ref[...]) * g_ref[...]).astype(BF16)

    y1 = _dot(xn, w1_ref[...])
    ka_raw = y1[:, :WIDTH_A]
    sq = ka_raw * ka_raw
    sq_hi = sq.astype(BF16)
    sq_lo = (sq - sq_hi.astype(F32)).astype(BF16)
    ss = _dot(sq_hi, gsum_ref[...]) + _dot(sq_lo, gsum_ref[...])
    ka = ka_raw * lax.rsqrt(ss * (1.0 / HEAD_DIM_A) + RMS_EPS) * gk_ref[...]
    ka_ref[...] = ka.astype(BF16)
    kiwi = y1[:, WIDTH_A:]
    lane = lax.broadcasted_iota(jnp.int32, kiwi.shape, 1)
    kib_ref[...] = jnp.where(lane < IDX_DIM, kiwi, 0.0).astype(BF16)
    wt_ref[...] = kiwi.T[IDX_DIM:IDX_DIM + N_IDX_HEADS, :]

    yt = _dot_nt(w2t_ref[...], xn)
    gq = jnp.concatenate([gq_ref[...]] * (tm // 128), axis=1)
    for h in range(N_HEADS_A):
        qh = yt[h * HEAD_DIM_A:(h + 1) * HEAD_DIM_A, :]
        ms = jnp.mean(qh * qh, axis=0, keepdims=True)
        qn = qh * lax.rsqrt(ms + RMS_EPS) * gq
        qat_ref[h * HEAD_DIM_A:(h + 1) * HEAD_DIM_A, :] = (qn * (HEAD_DIM_A ** -0.5)).astype(BF16)
    vat_ref[...] = yt[WIDTH_A:2 * WIDTH_A, :].astype(BF16)
    qit_ref[...] = yt[2 * WIDTH_A:3 * WIDTH_A, :].astype(BF16)


def _proj_a(x2, norm_g, w_in, q_norm_a, k_norm_a, batch, seq):
    n = x2.shape[0]
    tm = min(TM_A, seq)
    per_b = seq // tm
    w1 = jnp.concatenate(
        [w_in[:, _C_KA:_C_KA + 512], w_in[:, _C_KI:_C_KI + 72],
         jnp.zeros((D_MODEL, 56), F32)], axis=1).astype(BF16)
    w2t = jnp.concatenate(
        [w_in[:, _C_QA:_C_QA + 512], w_in[:, _C_VA:_C_VA + 512],
         w_in[:, _C_QI:_C_QI + 512]], axis=1).T.astype(BF16)
    hid = np.arange(WIDTH_A) // HEAD_DIM_A
    gsum = jnp.asarray(hid[:, None] == hid[None, :], BF16)
    gk = jnp.tile(k_norm_a.reshape(1, HEAD_DIM_A), (1, N_HEADS_A))
    gq = jnp.broadcast_to(q_norm_a.reshape(HEAD_DIM_A, 1), (HEAD_DIM_A, 128))

    full = lambda shape: pl.BlockSpec(shape, lambda i: (0,) * len(shape))
    t_spec = lambda rows: pl.BlockSpec((None, rows, tm), lambda i: (i // per_b, 0, i % per_b))
    return pl.pallas_call(
        _proj_a_kernel,
        grid=(n // tm,),
        in_specs=[
            pl.BlockSpec((tm, D_MODEL), lambda i: (i, 0)),
            full((1, D_MODEL)), full((D_MODEL, 640)), full((1536, D_MODEL)),
            full((WIDTH_A, WIDTH_A)), full((1, WIDTH_A)), full((HEAD_DIM_A, 128)),
        ],
        out_specs=[
            t_spec(WIDTH_A),
            pl.BlockSpec((tm, WIDTH_A), lambda i: (i, 0)),
            t_spec(WIDTH_A), t_spec(WIDTH_A),
            pl.BlockSpec((tm, 128), lambda i: (i, 0)),
            t_spec(N_IDX_HEADS),
        ],
        out_shape=[
            jax.ShapeDtypeStruct((batch, WIDTH_A, seq), BF16),
            jax.ShapeDtypeStruct((n, WIDTH_A), BF16),
            jax.ShapeDtypeStruct((batch, WIDTH_A, seq), BF16),
            jax.ShapeDtypeStruct((batch, WIDTH_A, seq), BF16),
            jax.ShapeDtypeStruct((n, 128), BF16),
            jax.ShapeDtypeStruct((batch, N_IDX_HEADS, seq), F32),
        ],
        compiler_params=pltpu.CompilerParams(
            dimension_semantics=("arbitrary",), vmem_limit_bytes=VMEM_LIMIT_BYTES),
        name="proj_a",
    )(x2, norm_g.reshape(1, D_MODEL), w1, w2t, gsum, gk, gq)


def _mem_kv_kernel(mem_ref, g_ref, w_ref, gk_ref, mk_ref, mv_ref):
    mn = (_rms_rows(mem_ref[...]) * g_ref[...]).astype(BF16)
    y = _dot(mn, w_ref[...])
    for h in range(N_HEADS_M):
        kh = y[:, h * HEAD_DIM_M:(h + 1) * HEAD_DIM_M]
        mk_ref[:, h * HEAD_DIM_M:(h + 1) * HEAD_DIM_M] = (_rms_rows(kh) * gk_ref[...]).astype(BF16)
    mv_ref[...] = y[:, WIDTH_M:].astype(BF16)


def _mem_kv(mem, mem_norm_g, w_mem_kv, k_norm_m):
    batch, m_len, _ = mem.shape
    full = lambda shape: pl.BlockSpec(shape, lambda b: (0,) * len(shape))
    return pl.pallas_call(
        _mem_kv_kernel,
        grid=(batch,),
        in_specs=[
            pl.BlockSpec((None, m_len, D_MODEL), lambda b: (b, 0, 0)),
            full((1, D_MODEL)), full((D_MODEL, 2 * WIDTH_M)), full((1, HEAD_DIM_M)),
        ],
        out_specs=[pl.BlockSpec((None, m_len, WIDTH_M), lambda b: (b, 0, 0))] * 2,
        out_shape=[jax.ShapeDtypeStruct((batch, m_len, WIDTH_M), BF16)] * 2,
        compiler_params=pltpu.CompilerParams(
            dimension_semantics=("arbitrary",), vmem_limit_bytes=VMEM_LIMIT_BYTES),
        name="mem_kv",
    )(mem, mem_norm_g.reshape(1, D_MODEL), w_mem_kv.astype(BF16), k_norm_m.reshape(1, HEAD_DIM_M))


def _alibi_slope(h):
    return float(2.0 ** (-8.0 * (h + 1) / N_HEADS_A))


def _dsa_attn_kernel(qat_ref, qit_ref, wt_ref, ka_ref, vat_ref, kib_ref, direct_ref, out_ref,
                     sc_ref, ot_ref, rhsi_ref, rhsa_ref, tri_ref, m_ref, d_ref, lga_ref, lgb_ref,
                     dmin_ref, *, seq, topk):
    tq = qat_ref.shape[1]
    ck = tq
    i = pl.program_id(1)
    t0 = i * tq
    nk = i + 1
    inf = jnp.float32(jnp.inf)
    kf = jnp.float32(topk)
    t_row = t0 + lax.broadcasted_iota(jnp.int32, (1, tq), 1)

    def fold(w, op):
        return op(w.reshape(ck // FOLD_ROWS, FOLD_ROWS, w.shape[1]), axis=0)

    def over_chunks(body, init):
        def step(c, carry):
            r0 = pl.multiple_of(c * ck, ck)
            return body(sc_ref[pl.ds(r0, ck), :], r0, carry)
        return lax.fori_loop(0, nk, step, init)

    def col(acc, op):
        return op(acc, axis=0, keepdims=True)

    facc = lambda v: jnp.full((FOLD_ROWS, tq), v, F32)

    zero_rows = jnp.zeros((128 - IDX_DIM, tq), BF16)
    for h in range(N_IDX_HEADS):
        rhsi_ref[h] = jnp.concatenate(
            [qit_ref[h * IDX_DIM:(h + 1) * IDX_DIM, :], zero_rows], axis=0)
    idx_scale = (IDX_DIM ** -0.5) * (N_IDX_HEADS ** -0.5)

    n_ahead = N_IDX_HEADS // 2

    def dots_ahead(kc):
        for h in range(n_ahead):
            lgb_ref[:, h * tq:(h + 1) * tq] = _dot(kc, rhsi_ref[h])

    dots_ahead(kib_ref[0:ck, :])

    def score_chunk(c, carry):
        lo_a, hi_a = carry
        r0 = pl.multiple_of(c * ck, ck)
        kc = kib_ref[pl.ds(r0, ck), :]
        k_next = kib_ref[pl.ds(pl.multiple_of(jnp.minimum(c + 1, nk - 1) * ck, ck), ck), :]
        acc = jnp.zeros((ck, tq), F32)
        for h in range(n_ahead):
            acc = acc + jnp.maximum(lgb_ref[:, h * tq:(h + 1) * tq], 0.0) * wt_ref[h:h + 1, :]
        for h in range(n_ahead, N_IDX_HEADS):
            d = _dot(kc, rhsi_ref[h])
            acc = acc + jnp.maximum(d, 0.0) * wt_ref[h:h + 1, :]
        dots_ahead(k_next)
        acc = acc * idx_scale
        causal = r0 + lax.broadcasted_iota(jnp.int32, (ck, tq), 0) <= t_row
        sc_ref[pl.ds(r0, ck), :] = jnp.where(causal, acc, -inf)
        return (jnp.minimum(lo_a, fold(jnp.where(causal, acc, inf), jnp.min)),
                jnp.maximum(hi_a, fold(jnp.where(causal, acc, -inf), jnp.max)))

    lo_a, hi_a = lax.fori_loop(0, nk, score_chunk, (facc(inf), facc(-inf)))

    def count_ge(v):
        acc = over_chunks(lambda x, r0, a: a + fold(jnp.where(x >= v, 1.0, 0.0), jnp.sum), facc(0.0))
        return col(acc, jnp.sum)

    lo_min = col(lo_a, jnp.min)
    hi_max = col(hi_a, jnp.max)
    n_causal = (t_row + 1).astype(F32)
    short = n_causal < kf

    def bis_step(lo, hi, cl):
        mid = lo * 0.5 + hi * 0.5
        c = count_ge(mid)
        ok = c >= kf
        return jnp.where(ok, mid, lo), jnp.where(ok, hi, mid), jnp.where(ok, c, cl)

    cl0 = jnp.where(short, kf, n_causal)
    lo, _, cl = lax.fori_loop(0, BISECT_STEPS, lambda _, s: bis_step(*s), (lo_min, hi_max, cl0))

    def walk_cond(st):
        return jnp.max(st[3]) > 0.0

    def walk_body(st):
        lo, thr, take, todo = st
        lo_e = col(over_chunks(
            lambda x, r0, a: jnp.minimum(a, fold(jnp.where(x >= lo, x, inf), jnp.min)), facc(inf)), jnp.min)

        def above(x, r0, carry):
            cgt_a, nxt_a = carry
            gt = x > lo_e
            return (cgt_a + fold(jnp.where(gt, 1.0, 0.0), jnp.sum),
                    jnp.minimum(nxt_a, fold(jnp.where(gt, x, inf), jnp.min)))

        cgt_a, nxt_a = over_chunks(above, (facc(0.0), facc(inf)))
        cgt, nxt = col(cgt_a, jnp.sum), col(nxt_a, jnp.min)
        active = todo > 0.0
        fin = active & (cgt < kf)
        thr = jnp.where(fin, lo_e, thr)
        take = jnp.where(fin, kf - cgt, take)
        lo = jnp.where(active & (cgt >= kf), nxt, lo)
        todo = jnp.where(fin, 0.0, todo)
        return lo, thr, take, todo

    thr0 = jnp.where(short, lo_min, lo)
    take0 = jnp.full((1, tq), float(seq), F32)
    todo0 = jnp.where(cl == kf, 0.0, 1.0)
    _, thr, take, _ = lax.while_loop(walk_cond, walk_body, (lo, thr0, take0, todo0))

    def alibi_dist(r0):
        s_idx = r0 + lax.broadcasted_iota(jnp.int32, (ck, tq), 0)
        return (t_row - s_idx).astype(F32)

    some_tie = jnp.max(jnp.where(take < float(seq), 1.0, 0.0)) > 0.0

    @pl.when(some_tie)
    def _():
        r_i = lax.broadcasted_iota(jnp.int32, (ck, ck), 0)
        c_i = lax.broadcasted_iota(jnp.int32, (ck, ck), 1)
        tri_ref[...] = jnp.where(c_i <= r_i, 1.0, 0.0).astype(BF16)

        def pen_chunk(c, carry):
            ties_before, dmin_a = carry
            r0 = pl.multiple_of(c * ck, ck)
            x = sc_ref[pl.ds(r0, ck), :]
            dist = alibi_dist(r0)
            eq = x == thr
            rank = _dot(tri_ref[...], jnp.where(eq, 1.0, 0.0).astype(BF16)) + ties_before
            tie = jnp.where(eq, jnp.where(rank <= take, dist, inf), inf)
            pen = jnp.where(x > thr, dist, tie)
            sc_ref[pl.ds(r0, ck), :] = pen
            return rank[ck - 1:ck, :], jnp.minimum(dmin_a, fold(pen, jnp.min))

        _, dmin_a = lax.fori_loop(0, nk, pen_chunk, (jnp.zeros((1, tq), F32), facc(inf)))
        dmin_ref[...] = jnp.broadcast_to(col(dmin_a, jnp.min), (8, tq))

    @pl.when(jnp.logical_not(some_tie))
    def _():
        def pen_chunk(c, dmin_a):
            r0 = pl.multiple_of(c * ck, ck)
            x = sc_ref[pl.ds(r0, ck), :]
            pen = jnp.where(x >= thr, alibi_dist(r0), inf)
            sc_ref[pl.ds(r0, ck), :] = pen
            return jnp.minimum(dmin_a, fold(pen, jnp.min))

        dmin_a = lax.fori_loop(0, nk, pen_chunk, facc(inf))
        dmin_ref[...] = jnp.broadcast_to(col(dmin_a, jnp.min), (8, tq))

    dmin = dmin_ref[0:1, :]

    def shift_chunk(c, carry):
        r0 = pl.multiple_of(c * ck, ck)
        sc_ref[pl.ds(r0, ck), :] = sc_ref[pl.ds(r0, ck), :] - dmin
        return carry

    lax.fori_loop(0, nk, shift_chunk, 0)

    row = lax.broadcasted_iota(jnp.int32, (128, tq), 0)
    for p in range(N_HEADS_A // 2):
        qpair = qat_ref[p * 128:(p + 1) * 128, :]
        rhsa_ref[p] = jnp.concatenate(
            [jnp.where(row < HEAD_DIM_A, qpair, jnp.zeros_like(qpair)),
             jnp.where(row >= HEAD_DIM_A, qpair, jnp.zeros_like(qpair))], axis=1)

    m_ref[...] = jnp.full((N_HEADS_A, tq), SOFTMAX_M_INIT, F32)
    d_ref[...] = jnp.zeros((N_HEADS_A, tq), F32)
    ot_ref[...] = jnp.zeros((WIDTH_A, tq), F32)

    def k_chunk(c):
        r0 = pl.multiple_of(c * ck, ck)
        return [ka_ref[pl.ds(r0, ck), p * 128:(p + 1) * 128] for p in range(N_HEADS_A // 2)]

    def qk_pair(buf, p, kc):
        buf[:, p * 2 * tq:(p + 1) * 2 * tq] = _dot(kc, rhsa_ref[p])

    def qk_into(buf, kcs):
        for p in range(N_HEADS_A // 2):
            qk_pair(buf, p, kcs[p])

    def step(cur, nxt, c, c_next):
        r0 = pl.multiple_of(c * ck, ck)
        pen = sc_ref[pl.ds(r0, ck), :]
        vts = [vat_ref[h * HEAD_DIM_A:(h + 1) * HEAD_DIM_A, pl.ds(r0, ck)]
               for h in range(N_HEADS_A)]
        kcs = k_chunk(c_next)
        qk_pair(nxt, 0, kcs[0])
        for p in range(N_HEADS_A // 2):
            m_seen = soft_pv(cur, pen, vts, (2 * p, 2 * p + 1))
            if p + 1 < N_HEADS_A // 2:
                tie = jnp.where(m_seen != m_seen, 1.0, 0.0).astype(BF16)
                qk_pair(nxt, p + 1, kcs[p + 1] + tie)

    def soft_pv(buf, pen, vts, heads):
        m_seen = None
        for h in heads:
            rows = slice(h * HEAD_DIM_A, (h + 1) * HEAD_DIM_A)
            alphas, probs = [], []
            for lt in range(tq // 128):
                ln = slice(lt * 128, (lt + 1) * 128)
                m = m_ref[h:h + 1, ln]
                lg = buf[:, h * tq + lt * 128:h * tq + (lt + 1) * 128] - _alibi_slope(h) * pen[:, ln]
                m_new = jnp.maximum(m, col(fold(lg, jnp.max), jnp.max))
                alpha = jnp.exp(m - m_new)
                pr = jnp.exp(lg - m_new)
                m_ref[h:h + 1, ln] = m_new
                d_ref[h:h + 1, ln] = d_ref[h:h + 1, ln] * alpha + col(fold(pr, jnp.sum), jnp.sum)
                alphas.append(alpha)
                probs.append(pr.astype(BF16))
                m_seen = m_new if m_seen is None else m_seen
            ot_ref[rows, :] = (ot_ref[rows, :] * jnp.concatenate(alphas, axis=1)
                               + _dot(vts[h], jnp.concatenate(probs, axis=1)))
        return m_seen

    last = nk - 1
    lg0_ref = lga_ref.at[:, 0:2 * tq]

    def direct_chunk(c, carry):
        r0 = pl.multiple_of(c * ck, ck)
        pen = sc_ref[pl.ds(r0, ck), :]
        vts = [vat_ref[h * HEAD_DIM_A:(h + 1) * HEAD_DIM_A, pl.ds(r0, ck)]
               for h in range(N_HEADS_A)]
        kcs = k_chunk(c)
        k_next = ka_ref[pl.ds(pl.multiple_of(jnp.minimum(c + 1, last) * ck, ck), ck), 0:128]
        lg2 = lg0_ref[...]
        for p in range(N_HEADS_A // 2):
            lg2_next = None
            for j in range(2):
                h = 2 * p + j
                rows = slice(h * HEAD_DIM_A, (h + 1) * HEAD_DIM_A)
                probs = []
                for lt in range(tq // 128):
                    ln = slice(lt * 128, (lt + 1) * 128)
                    e = jnp.exp(lg2[:, j * tq + lt * 128:j * tq + (lt + 1) * 128]
                                - _alibi_slope(h) * pen[:, ln])
                    esum = col(fold(e, jnp.sum), jnp.sum)
                    d_ref[h:h + 1, ln] = d_ref[h:h + 1, ln] + esum
                    probs.append(e.astype(BF16))
                    if j == 0 and lt == 0:
                        if p + 1 < N_HEADS_A // 2:
                            lg2_next = _dot(kcs[p + 1], rhsa_ref[p + 1])
                        if p + 2 == N_HEADS_A // 2:
                            lg0_ref[...] = _dot(k_next, rhsa_ref[0])
                ot_ref[rows, :] = ot_ref[rows, :] + _dot(vts[h], jnp.concatenate(probs, axis=1))
            lg2 = lg2_next
        return carry

    @pl.when(direct_ref[0] > 0)
    def _():
        lg0_ref[...] = _dot(ka_ref[0:ck, 0:128], rhsa_ref[0])
        lax.fori_loop(0, nk, direct_chunk, 0)

    @pl.when(direct_ref[0] <= 0)
    def _():
        qk_into(lga_ref, k_chunk(0))

        def attn_pair(cc, carry):
            c0 = 2 * cc
            step(lga_ref, lgb_ref, c0, jnp.minimum(c0 + 1, last))

            @pl.when(c0 + 1 < nk)
            def _():
                step(lgb_ref, lga_ref, c0 + 1, jnp.minimum(c0 + 2, last))

            return carry

        lax.fori_loop(0, (nk + 1) // 2, attn_pair, 0)

    for h in range(N_HEADS_A):
        rows = slice(h * HEAD_DIM_A, (h + 1) * HEAD_DIM_A)
        ot_ref[rows, :] = ot_ref[rows, :] / d_ref[h:h + 1, :]
    out_ref[...] = ot_ref[...].T


def _dsa_attn(qat, qit, wt, ka, vat, kib, q_norm_a, k_norm_a, batch, seq):
    tq = min(TQ, seq)
    topk = min(TOPK_MAX, seq // 4)
    qk_bound = HEAD_DIM_A ** 0.5 * jnp.max(jnp.abs(q_norm_a)) * jnp.max(jnp.abs(k_norm_a)) * 1.01
    direct = (qk_bound <= SOFTMAX_DIRECT_BOUND).astype(jnp.int32).reshape(1)
    kern = functools.partial(_dsa_attn_kernel, seq=seq, topk=topk)
    q_spec = lambda rows: pl.BlockSpec((None, rows, tq), lambda b, i: (b, 0, i))
    return pl.pallas_call(
        kern,
        grid=(batch, seq // tq),
        in_specs=[
            q_spec(WIDTH_A), q_spec(WIDTH_A), q_spec(N_IDX_HEADS),
            pl.BlockSpec((None, seq, WIDTH_A), lambda b, i: (b, 0, 0)),
            pl.BlockSpec((None, WIDTH_A, seq), lambda b, i: (b, 0, 0)),
            pl.BlockSpec((None, seq, 128), lambda b, i: (b, 0, 0)),
            pl.BlockSpec(memory_space=pltpu.SMEM),
        ],
        out_specs=pl.BlockSpec((None, tq, WIDTH_A), lambda b, i: (b, i, 0)),
        out_shape=jax.ShapeDtypeStruct((batch, seq, WIDTH_A), F32),
        scratch_shapes=[
            pltpu.VMEM((seq, tq), F32),
            pltpu.VMEM((WIDTH_A, tq), F32),
            pltpu.VMEM((N_IDX_HEADS, 128, tq), BF16),
            pltpu.VMEM((N_HEADS_A // 2, 128, 2 * tq), BF16),
            pltpu.VMEM((tq, tq), BF16),
            pltpu.VMEM((N_HEADS_A, tq), F32),
            pltpu.VMEM((N_HEADS_A, tq), F32),
            pltpu.VMEM((tq, N_HEADS_A * tq), F32),
            pltpu.VMEM((tq, N_HEADS_A * tq), F32),
            pltpu.VMEM((8, tq), F32),
        ],
        compiler_params=pltpu.CompilerParams(
            dimension_semantics=("arbitrary", "arbitrary"), vmem_limit_bytes=VMEM_LIMIT_BYTES),
        name="dsa_attn",
    )(qat, qit, wt, ka.reshape(batch, seq, WIDTH_A), vat, kib.reshape(batch, seq, 128), direct)


_M_ZA, _M_BG, _M_CG, _M_HB, _M_ZB, _M_QM, _M_ZM, _M_G = (
    0, 512, 1024, 1536, 2048, 2560, 3072, 3584)
_M_COLS = 3584 + N_BRANCHES * D_MODEL


def _merge_kernel(x_ref, attn_ref, g_ref, wc_ref, bg_ref, cw_ref, mk_ref, mv_ref,
                  gqm_ref, wa_ref, wb_ref, wm_ref, wo_ref, out_ref, utail_ref, *, per_b):
    tm = x_ref.shape[0]
    xf = x_ref[...]
    xn = (_rms_rows(xf) * g_ref[...]).astype(BF16)

    def proj(lhs, c0, width):
        return _dot(lhs, wc_ref[:, c0:c0 + width])

    za = proj(xn, _M_ZA, WIDTH_A)
    ya = _dot((attn_ref[...] * jax.nn.silu(za)).astype(BF16), wa_ref[...])

    u = proj(xn, _M_CG, WIDTH_B) * proj(xn, _M_HB, WIDTH_B)
    @pl.when((pl.program_id(0) % per_b) == 0)
    def _():
        utail_ref[...] = jnp.zeros((8, WIDTH_B), F32)

    uh = utail_ref[...]
    utail_ref[...] = u[tm - 8:, :]
    rows = lax.broadcasted_iota(jnp.int32, (tm, WIDTH_B), 0)
    u1 = jnp.where(rows == 0, uh[7:8, :], pltpu.roll(u, 1, 0))
    u2 = jnp.where(rows == 0, uh[6:7, :], jnp.where(rows == 1, uh[7:8, :], pltpu.roll(u, 2, 0)))
    conv = cw_ref[0:1, :] * u2 + cw_ref[1:2, :] * u1 + cw_ref[2:3, :] * u
    bgate = proj(xn, _M_BG, WIDTH_B)
    zb = proj(xn, _M_ZB, WIDTH_B)
    yb = _dot(((bgate * conv) * jax.nn.silu(zb)).astype(BF16), wb_ref[...])

    qm = proj(xn, _M_QM, WIDTH_M)
    zm = proj(xn, _M_ZM, WIDTH_M)
    heads = []
    for h in range(N_HEADS_M):
        sl = slice(h * HEAD_DIM_M, (h + 1) * HEAD_DIM_M)
        qh = (_rms_rows(qm[:, sl]) * gqm_ref[...]).astype(BF16)
        lg = _dot_nt(qh, mk_ref[:, sl]) * (HEAD_DIM_M ** -0.5)
        e = jnp.exp(lg - jnp.max(lg, axis=-1, keepdims=True))
        p = e / jnp.sum(e, axis=-1, keepdims=True)
        heads.append(_dot(p.astype(BF16), mv_ref[:, sl]))
    attn_m = jnp.concatenate(heads, axis=1)
    ym = _dot((attn_m * jax.nn.silu(zm)).astype(BF16), wm_ref[...])

    def gate(j):
        gj = proj(xn, _M_G + j * D_MODEL, D_MODEL) + bg_ref[:, j * D_MODEL:(j + 1) * D_MODEL]
        return jax.nn.sigmoid(gj)

    merged = gate(0) * ya + gate(1) * yb + gate(2) * ym
    out_ref[...] = xf + _dot(merged.astype(BF16), wo_ref[...])


def _merge(x2, attn2, mk, mv, norm_g, w_in, b_gate, conv_w, q_norm_m,
           w_out_a, w_out_b, w_out_m, w_o, batch, seq):
    n = x2.shape[0]
    tm = min(TM_C, seq)
    per_b = seq // tm
    m_len = mk.shape[1]
    wc = jnp.concatenate([w_in[:, _C_ZA:_C_ZA + 512], w_in[:, _C_REST:]], axis=1).astype(BF16)
    const = lambda shape: pl.BlockSpec(shape, lambda i: (0,) * len(shape),
                                       pipeline_mode=pl.Buffered(1))
    kern = functools.partial(_merge_kernel, per_b=per_b)
    return pl.pallas_call(
        kern,
        grid=(n // tm,),
        in_specs=[
            pl.BlockSpec((tm, D_MODEL), lambda i: (i, 0)),
            pl.BlockSpec((tm, WIDTH_A), lambda i: (i, 0)),
            const((1, D_MODEL)), const((D_MODEL, _M_COLS)), const((1, N_BRANCHES * D_MODEL)),
            const((CONV_WIDTH, WIDTH_B)),
            pl.BlockSpec((None, m_len, WIDTH_M), lambda i: (i // per_b, 0, 0)),
            pl.BlockSpec((None, m_len, WIDTH_M), lambda i: (i // per_b, 0, 0)),
            const((1, HEAD_DIM_M)),
            const((WIDTH_A, D_MODEL)), const((WIDTH_B, D_MODEL)), const((WIDTH_M, D_MODEL)),
            const((D_MODEL, D_MODEL)),
        ],
        out_specs=pl.BlockSpec((tm, D_MODEL), lambda i: (i, 0)),
        out_shape=jax.ShapeDtypeStruct((n, D_MODEL), F32),
        scratch_shapes=[pltpu.VMEM((8, WIDTH_B), F32)],
        compiler_params=pltpu.CompilerParams(
            dimension_semantics=("arbitrary",), vmem_limit_bytes=VMEM_LIMIT_BYTES),
        name="merge",
    )(x2, attn2, norm_g.reshape(1, D_MODEL), wc, b_gate.reshape(1, -1), conv_w,
      mk, mv, q_norm_m.reshape(1, HEAD_DIM_M),
      w_out_a.astype(BF16), w_out_b.astype(BF16), w_out_m.astype(BF16), w_o.astype(BF16))


def _layer(h, mem, norm_g, mem_norm_g, w_in, b_gate, w_mem_kv, q_norm_a, k_norm_a,
           q_norm_m, k_norm_m, conv_w, w_out_a, w_out_b, w_out_m, w_o):
    batch, seq, _ = h.shape
    x2 = h.reshape(batch * seq, D_MODEL)
    qat, ka, vat, qit, kib, wt = _proj_a(x2, norm_g, w_in, q_norm_a, k_norm_a, batch, seq)
    mk, mv = _mem_kv(mem, mem_norm_g, w_mem_kv, k_norm_m)
    attn = _dsa_attn(qat, qit, wt, ka, vat, kib, q_norm_a, k_norm_a, batch, seq)
    out = _merge(x2, attn.reshape(batch * seq, WIDTH_A), mk, mv, norm_g, w_in, b_gate, conv_w,
                 q_norm_m, w_out_a, w_out_b, w_out_m, w_o, batch, seq)
    return out.reshape(batch, seq, D_MODEL)


def kernel(x, mem, norm_g, mem_norm_g, w_in, b_gate, w_mem_kv, q_norm_a, k_norm_a,
           q_norm_m, k_norm_m, conv_w, w_out_a, w_out_b, w_out_m, w_o):
    h = x
    for l in range(norm_g.shape[0]):
        h = _layer(h, mem, norm_g[l], mem_norm_g[l], w_in[l], b_gate[l], w_mem_kv[l],
                   q_norm_a[l], k_norm_a[l], q_norm_m[l], k_norm_m[l], conv_w[l],
                   w_out_a[l], w_out_b[l], w_out_m[l], w_o[l])
    return h
```

```python
import functools

import jax
import jax.numpy as jnp
import numpy as np
from jax import lax
from jax.experimental import pallas as pl
from jax.experimental.pallas import tpu as pltpu

F32 = jnp.float32
BF16 = jnp.bfloat16

D_MODEL = 1024
N_HEADS_A = 8
HEAD_DIM_A = 64
WIDTH_A = 512
N_IDX_HEADS = 8
IDX_DIM = 64
TOPK_MAX = 256
WIDTH_B = 512
CONV_WIDTH = 3
N_HEADS_M = 4
HEAD_DIM_M = 128
WIDTH_M = 512
N_BRANCHES = 3
RMS_EPS = 1e-6

_C_QA, _C_KA, _C_VA, _C_ZA = 0, 512, 1024, 1536
_C_QI, _C_KI, _C_WI = 2048, 2560, 2624
_C_REST = 2632
_D_IN = 8776

VMEM_LIMIT_BYTES = 56 * 1024 * 1024

TM_A = 512
TM_C = 512
TQ = 256
FOLD_ROWS = 32
BISECT_STEPS = 20
SOFTMAX_DIRECT_BOUND = 60.0
SOFTMAX_M_INIT = -1e30


def _rms_rows(xf, eps=RMS_EPS):
    return xf * lax.rsqrt(jnp.mean(xf * xf, axis=-1, keepdims=True) + eps)


def _dot(a, b):
    return jnp.dot(a, b, preferred_element_type=F32)


def _dot_nt(a, b):
    return lax.dot_general(a, b, (((1,), (1,)), ((), ())), preferred_element_type=F32)


def _proj_a_kernel(x_ref, g_ref, w1_ref, w2t_ref, gsum_ref, gk_ref, gq_ref,
                   qat_ref, ka_ref, vat_ref, qit_ref, kib_ref, wt_ref):
    tm = x_ref.shape[0]
    xn = (_rms_rows(x_ref[...]) * g_ref[...]).astype(BF16)

    y1 = _dot(xn, w1_ref[...])
    ka_raw = y1[:, :WIDTH_A]
    sq = ka_raw * ka_raw
    sq_hi = sq.astype(BF16)
    sq_lo = (sq - sq_hi.astype(F32)).astype(BF16)
    ss = _dot(sq_hi, gsum_ref[...]) + _dot(sq_lo, gsum_ref[...])
    ka = ka_raw * lax.rsqrt(ss * (1.0 / HEAD_DIM_A) + RMS_EPS) * gk_ref[...]
    ka_ref[...] = ka.astype(BF16)
    kiwi = y1[:, WIDTH_A:]
    lane = lax.broadcasted_iota(jnp.int32, kiwi.shape, 1)
    kib_ref[...] = jnp.where(lane < IDX_DIM, kiwi, 0.0).astype(BF16)
    wt_ref[...] = kiwi.T[IDX_DIM:IDX_DIM + N_IDX_HEADS, :]

    yt = _dot_nt(w2t_ref[...], xn)
    gq = jnp.concatenate([gq_ref[...]] * (tm // 128), axis=1)
    for h in range(N_HEADS_A):
        qh = yt[h * HEAD_DIM_A:(h + 1) * HEAD_DIM_A, :]
        ms = jnp.mean(qh * qh, axis=0, keepdims=True)
        qn = qh * lax.rsqrt(ms + RMS_EPS) * gq
        qat_ref[h * HEAD_DIM_A:(h + 1) * HEAD_DIM_A, :] = (qn * (HEAD_DIM_A ** -0.5)).astype(BF16)
    vat_ref[...] = yt[WIDTH_A:2 * WIDTH_A, :].astype(BF16)
    qit_ref[...] = yt[2 * WIDTH_A:3 * WIDTH_A, :].astype(BF16)


def _proj_a(x2, norm_g, w_in, q_norm_a, k_norm_a, batch, seq):
    n = x2.shape[0]
    tm = min(TM_A, seq)
    per_b = seq // tm
    w1 = jnp.concatenate(
        [w_in[:, _C_KA:_C_KA + 512], w_in[:, _C_KI:_C_KI + 72],
         jnp.zeros((D_MODEL, 56), F32)], axis=1).astype(BF16)
    w2t = jnp.concatenate(
        [w_in[:, _C_QA:_C_QA + 512], w_in[:, _C_VA:_C_VA + 512],
         w_in[:, _C_QI:_C_QI + 512]], axis=1).T.astype(BF16)
    hid = np.arange(WIDTH_A) // HEAD_DIM_A
    gsum = jnp.asarray(hid[:, None] == hid[None, :], BF16)
    gk = jnp.tile(k_norm_a.reshape(1, HEAD_DIM_A), (1, N_HEADS_A))
    gq = jnp.broadcast_to(q_norm_a.reshape(HEAD_DIM_A, 1), (HEAD_DIM_A, 128))

    full = lambda shape: pl.BlockSpec(shape, lambda i: (0,) * len(shape))
    t_spec = lambda rows: pl.BlockSpec((None, rows, tm), lambda i: (i // per_b, 0, i % per_b))
    return pl.pallas_call(
        _proj_a_kernel,
        grid=(n // tm,),
        in_specs=[
            pl.BlockSpec((tm, D_MODEL), lambda i: (i, 0)),
            full((1, D_MODEL)), full((D_MODEL, 640)), full((1536, D_MODEL)),
            full((WIDTH_A, WIDTH_A)), full((1, WIDTH_A)), full((HEAD_DIM_A, 128)),
        ],
        out_specs=[
            t_spec(WIDTH_A),
            pl.BlockSpec((tm, WIDTH_A), lambda i: (i, 0)),
            t_spec(WIDTH_A), t_spec(WIDTH_A),
            pl.BlockSpec((tm, 128), lambda i: (i, 0)),
            t_spec(N_IDX_HEADS),
        ],
        out_shape=[
            jax.ShapeDtypeStruct((batch, WIDTH_A, seq), BF16),
            jax.ShapeDtypeStruct((n, WIDTH_A), BF16),
            jax.ShapeDtypeStruct((batch, WIDTH_A, seq), BF16),
            jax.ShapeDtypeStruct((batch, WIDTH_A, seq), BF16),
            jax.ShapeDtypeStruct((n, 128), BF16),
            jax.ShapeDtypeStruct((batch, N_IDX_HEADS, seq), F32),
        ],
        compiler_params=pltpu.CompilerParams(
            dimension_semantics=("arbitrary",), vmem_limit_bytes=VMEM_LIMIT_BYTES),
        name="proj_a",
    )(x2, norm_g.reshape(1, D_MODEL), w1, w2t, gsum, gk, gq)


def _mem_kv_kernel(mem_ref, g_ref, w_ref, gk_ref, mk_ref, mv_ref):
    mn = (_rms_rows(mem_ref[...]) * g_ref[...]).astype(BF16)
    y = _dot(mn, w_ref[...])
    for h in range(N_HEADS_M):
        kh = y[:, h * HEAD_DIM_M:(h + 1) * HEAD_DIM_M]
        mk_ref[:, h * HEAD_DIM_M:(h + 1) * HEAD_DIM_M] = (_rms_rows(kh) * gk_ref[...]).astype(BF16)
    mv_ref[...] = y[:, WIDTH_M:].astype(BF16)


def _mem_kv(mem, mem_norm_g, w_mem_kv, k_norm_m):
    batch, m_len, _ = mem.shape
    full = lambda shape: pl.BlockSpec(shape, lambda b: (0,) * len(shape))
    return pl.pallas_call(
        _mem_kv_kernel,
        grid=(batch,),
        in_specs=[
            pl.BlockSpec((None, m_len, D_MODEL), lambda b: (b, 0, 0)),
            full((1, D_MODEL)), full((D_MODEL, 2 * WIDTH_M)), full((1, HEAD_DIM_M)),
        ],
        out_specs=[pl.BlockSpec((None, m_len, WIDTH_M), lambda b: (b, 0, 0))] * 2,
        out_shape=[jax.ShapeDtypeStruct((batch, m_len, WIDTH_M), BF16)] * 2,
        compiler_params=pltpu.CompilerParams(
            dimension_semantics=("arbitrary",), vmem_limit_bytes=VMEM_LIMIT_BYTES),
        name="mem_kv",
    )(mem, mem_norm_g.reshape(1, D_MODEL), w_mem_kv.astype(BF16), k_norm_m.reshape(1, HEAD_DIM_M))


def _alibi_slope(h):
    return float(2.0 ** (-8.0 * (h + 1) / N_HEADS_A))


def _dsa_attn_kernel(qat_ref, qit_ref, wt_ref, ka_ref, vat_ref, kib_ref, direct_ref, out_ref,
                     sc_ref, ot_ref, rhsi_ref, rhsa_ref, tri_ref, m_ref, d_ref, lga_ref, lgb_ref,
                     dmin_ref, *, seq, topk):
    tq = qat_ref.shape[1]
    ck = tq
    i = pl.program_id(1)
    t0 = i * tq
    nk = i + 1
    inf = jnp.float32(jnp.inf)
    kf = jnp.float32(topk)
    t_row = t0 + lax.broadcasted_iota(jnp.int32, (1, tq), 1)

    def fold(w, op):
        return op(w.reshape(ck // FOLD_ROWS, FOLD_ROWS, w.shape[1]), axis=0)

    def over_chunks(body, init):
        def step(c, carry):
            r0 = pl.multiple_of(c * ck, ck)
            return body(sc_ref[pl.ds(r0, ck), :], r0, carry)
        return lax.fori_loop(0, nk, step, init)

    def col(acc, op):
        return op(acc, axis=0, keepdims=True)

    facc = lambda v: jnp.full((FOLD_ROWS, tq), v, F32)

    zero_rows = jnp.zeros((128 - IDX_DIM, tq), BF16)
    for h in range(N_IDX_HEADS):
        rhsi_ref[h] = jnp.concatenate(
            [qit_ref[h * IDX_DIM:(h + 1) * IDX_DIM, :], zero_rows], axis=0)
    idx_scale = (IDX_DIM ** -0.5) * (N_IDX_HEADS ** -0.5)

    n_ahead = N_IDX_HEADS // 2

    def dots_ahead(kc):
        for h in range(n_ahead):
            lgb_ref[:, h * tq:(h + 1) * tq] = _dot(kc, rhsi_ref[h])

    dots_ahead(kib_ref[0:ck, :])

    def score_chunk(c, carry):
        lo_a, hi_a = carry
        r0 = pl.multiple_of(c * ck, ck)
        kc = kib_ref[pl.ds(r0, ck), :]
        k_next = kib_ref[pl.ds(pl.multiple_of(jnp.minimum(c + 1, nk - 1) * ck, ck), ck), :]
        acc = jnp.zeros((ck, tq), F32)
        for h in range(n_ahead):
            acc = acc + jnp.maximum(lgb_ref[:, h * tq:(h + 1) * tq], 0.0) * wt_ref[h:h + 1, :]
        for h in range(n_ahead, N_IDX_HEADS):
            d = _dot(kc, rhsi_ref[h])
            acc = acc + jnp.maximum(d, 0.0) * wt_ref[h:h + 1, :]
        dots_ahead(k_next)
        acc = acc * idx_scale
        causal = r0 + lax.broadcasted_iota(jnp.int32, (ck, tq), 0) <= t_row
        sc_ref[pl.ds(r0, ck), :] = jnp.where(causal, acc, -inf)
        return (jnp.minimum(lo_a, fold(jnp.where(causal, acc, inf), jnp.min)),
                jnp.maximum(hi_a, fold(jnp.where(causal, acc, -inf), jnp.max)))

    lo_a, hi_a = lax.fori_loop(0, nk, score_chunk, (facc(inf), facc(-inf)))

    def count_ge(v):
        acc = over_chunks(lambda x, r0, a: a + fold(jnp.where(x >= v, 1.0, 0.0), jnp.sum), facc(0.0))
        return col(acc, jnp.sum)

    lo_min = col(lo_a, jnp.min)
    hi_max = col(hi_a, jnp.max)
    n_causal = (t_row + 1).astype(F32)
    short = n_causal < kf

    def bis_step(lo, hi, cl):
        mid = lo * 0.5 + hi * 0.5
        c = count_ge(mid)
        ok = c >= kf
        return jnp.where(ok, mid, lo), jnp.where(ok, hi, mid), jnp.where(ok, c, cl)

    cl0 = jnp.where(short, kf, n_causal)
    lo, _, cl = lax.fori_loop(0, BISECT_STEPS, lambda _, s: bis_step(*s), (lo_min, hi_max, cl0))

    def walk_cond(st):
        return jnp.max(st[3]) > 0.0

    def walk_body(st):
        lo, thr, take, todo = st
        lo_e = col(over_chunks(
            lambda x, r0, a: jnp.minimum(a, fold(jnp.where(x >= lo, x, inf), jnp.min)), facc(inf)), jnp.min)

        def above(x, r0, carry):
            cgt_a, nxt_a = carry
            gt = x > lo_e
            return (cgt_a + fold(jnp.where(gt, 1.0, 0.0), jnp.sum),
                    jnp.minimum(nxt_a, fold(jnp.where(gt, x, inf), jnp.min)))

        cgt_a, nxt_a = over_chunks(above, (facc(0.0), facc(inf)))
        cgt, nxt = col(cgt_a, jnp.sum), col(nxt_a, jnp.min)
        active = todo > 0.0
        fin = active & (cgt < kf)
        thr = jnp.where(fin, lo_e, thr)
        take = jnp.where(fin, kf - cgt, take)
        lo = jnp.where(active & (cgt >= kf), nxt, lo)
        todo = jnp.where(fin, 0.0, todo)
        return lo, thr, take, todo

    thr0 = jnp.where(short, lo_min, lo)
    take0 = jnp.full((1, tq), float(seq), F32)
    todo0 = jnp.where(cl == kf, 0.0, 1.0)
    _, thr, take, _ = lax.while_loop(walk_cond, walk_body, (lo, thr0, take0, todo0))

    def alibi_dist(r0):
        s_idx = r0 + lax.broadcasted_iota(jnp.int32, (ck, tq), 0)
        return (t_row - s_idx).astype(F32)

    some_tie = jnp.max(jnp.where(take < float(seq), 1.0, 0.0)) > 0.0

    @pl.when(some_tie)
    def _():
        r_i = lax.broadcasted_iota(jnp.int32, (ck, ck), 0)
        c_i = lax.broadcasted_iota(jnp.int32, (ck, ck), 1)
        tri_ref[...] = jnp.where(c_i <= r_i, 1.0, 0.0).astype(BF16)

        def ranked(x, r0, ties_before):
            dist = alibi_dist(r0)
            eq = x == thr
            rank = _dot(tri_ref[...], jnp.where(eq, 1.0, 0.0).astype(BF16)) + ties_before
            tie = jnp.where(eq, jnp.where(rank <= take, dist, inf), inf)
            return jnp.where(x > thr, dist, tie), rank[ck - 1:ck, :]

        def pen_pair(cc, carry):
            ties_before, dmin_a = carry
            c0 = 2 * cc
            paired = c0 + 1 < nk
            r0 = pl.multiple_of(c0 * ck, ck)
            r1 = pl.multiple_of(jnp.minimum(c0 + 1, nk - 1) * ck, ck)
            x0 = sc_ref[pl.ds(r0, ck), :]
            x1 = sc_ref[pl.ds(r1, ck), :]
            pen0, ties0 = ranked(x0, r0, ties_before)
            pen1, ties1 = ranked(x1, r1, ties0)
            sc_ref[pl.ds(r0, ck), :] = pen0

            @pl.when(paired)
            def _():
                sc_ref[pl.ds(r1, ck), :] = pen1

            dmin_a = jnp.minimum(dmin_a, fold(pen0, jnp.min))
            dmin_a = jnp.minimum(dmin_a, jnp.where(paired, fold(pen1, jnp.min), inf))
            return jnp.where(paired, ties1, ties0), dmin_a

        _, dmin_a = lax.fori_loop(0, (nk + 1) // 2, pen_pair,
                                  (jnp.zeros((1, tq), F32), facc(inf)))
        dmin_ref[...] = jnp.broadcast_to(col(dmin_a, jnp.min), (8, tq))

    @pl.when(jnp.logical_not(some_tie))
    def _():
        def pen_chunk(c, dmin_a):
            r0 = pl.multiple_of(c * ck, ck)
            x = sc_ref[pl.ds(r0, ck), :]
            pen = jnp.where(x >= thr, alibi_dist(r0), inf)
            sc_ref[pl.ds(r0, ck), :] = pen
            return jnp.minimum(dmin_a, fold(pen, jnp.min))

        dmin_a = lax.fori_loop(0, nk, pen_chunk, facc(inf))
        dmin_ref[...] = jnp.broadcast_to(col(dmin_a, jnp.min), (8, tq))

    dmin = dmin_ref[0:1, :]

    row = lax.broadcasted_iota(jnp.int32, (128, tq), 0)
    for p in range(N_HEADS_A // 2):
        qpair = qat_ref[p * 128:(p + 1) * 128, :]
        rhsa_ref[p] = jnp.concatenate(
            [jnp.where(row < HEAD_DIM_A, qpair, jnp.zeros_like(qpair)),
             jnp.where(row >= HEAD_DIM_A, qpair, jnp.zeros_like(qpair))], axis=1)

    m_ref[...] = jnp.full((N_HEADS_A, tq), SOFTMAX_M_INIT, F32)
    d_ref[...] = jnp.zeros((N_HEADS_A, tq), F32)
    ot_ref[...] = jnp.zeros((WIDTH_A, tq), F32)

    def k_chunk(c):
        r0 = pl.multiple_of(c * ck, ck)
        return [ka_ref[pl.ds(r0, ck), p * 128:(p + 1) * 128] for p in range(N_HEADS_A // 2)]

    def qk_pair(buf, p, kc):
        buf[:, p * 2 * tq:(p + 1) * 2 * tq] = _dot(kc, rhsa_ref[p])

    def qk_into(buf, kcs):
        for p in range(N_HEADS_A // 2):
            qk_pair(buf, p, kcs[p])

    def step(cur, nxt, c, c_next):
        r0 = pl.multiple_of(c * ck, ck)
        pen = sc_ref[pl.ds(r0, ck), :] - dmin
        vts = [vat_ref[h * HEAD_DIM_A:(h + 1) * HEAD_DIM_A, pl.ds(r0, ck)]
               for h in range(N_HEADS_A)]
        kcs = k_chunk(c_next)
        qk_pair(nxt, 0, kcs[0])
        for p in range(N_HEADS_A // 2):
            m_seen = soft_pv(cur, pen, vts, (2 * p, 2 * p + 1))
            if p + 1 < N_HEADS_A // 2:
                tie = jnp.where(m_seen != m_seen, 1.0, 0.0).astype(BF16)
                qk_pair(nxt, p + 1, kcs[p + 1] + tie)

    def soft_pv(buf, pen, vts, heads):
        m_seen = None
        for h in heads:
            rows = slice(h * HEAD_DIM_A, (h + 1) * HEAD_DIM_A)
            alphas, probs = [], []
            for lt in range(tq // 128):
                ln = slice(lt * 128, (lt + 1) * 128)
                m = m_ref[h:h + 1, ln]
                lg = buf[:, h * tq + lt * 128:h * tq + (lt + 1) * 128] - _alibi_slope(h) * pen[:, ln]
                m_new = jnp.maximum(m, col(fold(lg, jnp.max), jnp.max))
                alpha = jnp.exp(m - m_new)
                pr = jnp.exp(lg - m_new)
                m_ref[h:h + 1, ln] = m_new
                d_ref[h:h + 1, ln] = d_ref[h:h + 1, ln] * alpha + col(fold(pr, jnp.sum), jnp.sum)
                alphas.append(alpha)
                probs.append(pr.astype(BF16))
                m_seen = m_new if m_seen is None else m_seen
            ot_ref[rows, :] = (ot_ref[rows, :] * jnp.concatenate(alphas, axis=1)
                               + _dot(vts[h], jnp.concatenate(probs, axis=1)))
        return m_seen

    last = nk - 1
    lg0_ref = lga_ref.at[:, 0:2 * tq]

    def direct_chunk(c, carry):
        r0 = pl.multiple_of(c * ck, ck)
        pen = sc_ref[pl.ds(r0, ck), :] - dmin
        vts = [vat_ref[h * HEAD_DIM_A:(h + 1) * HEAD_DIM_A, pl.ds(r0, ck)]
               for h in range(N_HEADS_A)]
        kcs = k_chunk(c)
        k_next = ka_ref[pl.ds(pl.multiple_of(jnp.minimum(c + 1, last) * ck, ck), ck), 0:128]
        lg2 = lg0_ref[...]
        for p in range(N_HEADS_A // 2):
            lg2_next = None
            for j in range(2):
                h = 2 * p + j
                rows = slice(h * HEAD_DIM_A, (h + 1) * HEAD_DIM_A)
                probs = []
                for lt in range(tq // 128):
                    ln = slice(lt * 128, (lt + 1) * 128)
                    e = jnp.exp(lg2[:, j * tq + lt * 128:j * tq + (lt + 1) * 128]
                                - _alibi_slope(h) * pen[:, ln])
                    esum = col(fold(e, jnp.sum), jnp.sum)
                    d_ref[h:h + 1, ln] = d_ref[h:h + 1, ln] + esum
                    probs.append(e.astype(BF16))
                    if j == 0 and lt == 0:
                        if p + 1 < N_HEADS_A // 2:
                            lg2_next = _dot(kcs[p + 1], rhsa_ref[p + 1])
                        if p + 2 == N_HEADS_A // 2:
                            lg0_ref[...] = _dot(k_next, rhsa_ref[0])
                ot_ref[rows, :] = ot_ref[rows, :] + _dot(vts[h], jnp.concatenate(probs, axis=1))
            lg2 = lg2_next
        return carry

    @pl.when(direct_ref[0] > 0)
    def _():
        lg0_ref[...] = _dot(ka_ref[0:ck, 0:128], rhsa_ref[0])
        lax.fori_loop(0, nk, direct_chunk, 0)

    @pl.when(direct_ref[0] <= 0)
    def _():
        qk_into(lga_ref, k_chunk(0))

        def attn_pair(cc, carry):
            c0 = 2 * cc
            step(lga_ref, lgb_ref, c0, jnp.minimum(c0 + 1, last))

            @pl.when(c0 + 1 < nk)
            def _():
                step(lgb_ref, lga_ref, c0 + 1, jnp.minimum(c0 + 2, last))

            return carry

        lax.fori_loop(0, (nk + 1) // 2, attn_pair, 0)

    for h in range(N_HEADS_A):
        rows = slice(h * HEAD_DIM_A, (h + 1) * HEAD_DIM_A)
        ot_ref[rows, :] = ot_ref[rows, :] / d_ref[h:h + 1, :]
    out_ref[...] = ot_ref[...].T


def _dsa_attn(qat, qit, wt, ka, vat, kib, q_norm_a, k_norm_a, batch, seq):
    tq = min(TQ, seq)
    topk = min(TOPK_MAX, seq // 4)
    qk_bound = HEAD_DIM_A ** 0.5 * jnp.max(jnp.abs(q_norm_a)) * jnp.max(jnp.abs(k_norm_a)) * 1.01
    direct = (qk_bound <= SOFTMAX_DIRECT_BOUND).astype(jnp.int32).reshape(1)
    kern = functools.partial(_dsa_attn_kernel, seq=seq, topk=topk)
    q_spec = lambda rows: pl.BlockSpec((None, rows, tq), lambda b, i: (b, 0, i))
    return pl.pallas_call(
        kern,
        grid=(batch, seq // tq),
        in_specs=[
            q_spec(WIDTH_A), q_spec(WIDTH_A), q_spec(N_IDX_HEADS),
            pl.BlockSpec((None, seq, WIDTH_A), lambda b, i: (b, 0, 0)),
            pl.BlockSpec((None, WIDTH_A, seq), lambda b, i: (b, 0, 0)),
            pl.BlockSpec((None, seq, 128), lambda b, i: (b, 0, 0)),
            pl.BlockSpec(memory_space=pltpu.SMEM),
        ],
        out_specs=pl.BlockSpec((None, tq, WIDTH_A), lambda b, i: (b, i, 0)),
        out_shape=jax.ShapeDtypeStruct((batch, seq, WIDTH_A), F32),
        scratch_shapes=[
            pltpu.VMEM((seq, tq), F32),
            pltpu.VMEM((WIDTH_A, tq), F32),
            pltpu.VMEM((N_IDX_HEADS, 128, tq), BF16),
            pltpu.VMEM((N_HEADS_A // 2, 128, 2 * tq), BF16),
            pltpu.VMEM((tq, tq), BF16),
            pltpu.VMEM((N_HEADS_A, tq), F32),
            pltpu.VMEM((N_HEADS_A, tq), F32),
            pltpu.VMEM((tq, N_HEADS_A * tq), F32),
            pltpu.VMEM((tq, N_HEADS_A * tq), F32),
            pltpu.VMEM((8, tq), F32),
        ],
        compiler_params=pltpu.CompilerParams(
            dimension_semantics=("arbitrary", "arbitrary"), vmem_limit_bytes=VMEM_LIMIT_BYTES),
        name="dsa_attn",
    )(qat, qit, wt, ka.reshape(batch, seq, WIDTH_A), vat, kib.reshape(batch, seq, 128), direct)


_M_ZA, _M_BG, _M_CG, _M_HB, _M_ZB, _M_QM, _M_ZM, _M_G = (
    0, 512, 1024, 1536, 2048, 2560, 3072, 3584)
_M_COLS = 3584 + N_BRANCHES * D_MODEL


def _merge_kernel(x_ref, attn_ref, g_ref, wc_ref, bg_ref, cw_ref, mk_ref, mv_ref,
                  gqm_ref, wa_ref, wb_ref, wm_ref, wo_ref, out_ref, utail_ref, *, per_b):
    tm = x_ref.shape[0]
    xf = x_ref[...]
    xn = (_rms_rows(xf) * g_ref[...]).astype(BF16)

    def proj(lhs, c0, width):
        return _dot(lhs, wc_ref[:, c0:c0 + width])

    za = proj(xn, _M_ZA, WIDTH_A)
    ya = _dot((attn_ref[...] * jax.nn.silu(za)).astype(BF16), wa_ref[...])

    u = proj(xn, _M_CG, WIDTH_B) * proj(xn, _M_HB, WIDTH_B)
    @pl.when((pl.program_id(0) % per_b) == 0)
    def _():
        utail_ref[...] = jnp.zeros((8, WIDTH_B), F32)

    uh = utail_ref[...]
    utail_ref[...] = u[tm - 8:, :]
    rows = lax.broadcasted_iota(jnp.int32, (tm, WIDTH_B), 0)
    u1 = jnp.where(rows == 0, uh[7:8, :], pltpu.roll(u, 1, 0))
    u2 = jnp.where(rows == 0, uh[6:7, :], jnp.where(rows == 1, uh[7:8, :], pltpu.roll(u, 2, 0)))
    conv = cw_ref[0:1, :] * u2 + cw_ref[1:2, :] * u1 + cw_ref[2:3, :] * u
    bgate = proj(xn, _M_BG, WIDTH_B)
    zb = proj(xn, _M_ZB, WIDTH_B)
    yb = _dot(((bgate * conv) * jax.nn.silu(zb)).astype(BF16), wb_ref[...])

    qm = proj(xn, _M_QM, WIDTH_M)
    zm = proj(xn, _M_ZM, WIDTH_M)
    heads = []
    for h in range(N_HEADS_M):
        sl = slice(h * HEAD_DIM_M, (h + 1) * HEAD_DIM_M)
        qh = (_rms_rows(qm[:, sl]) * gqm_ref[...]).astype(BF16)
        lg = _dot_nt(qh, mk_ref[:, sl]) * (HEAD_DIM_M ** -0.5)
        e = jnp.exp(lg - jnp.max(lg, axis=-1, keepdims=True))
        p = e / jnp.sum(e, axis=-1, keepdims=True)
        heads.append(_dot(p.astype(BF16), mv_ref[:, sl]))
    attn_m = jnp.concatenate(heads, axis=1)
    ym = _dot((attn_m * jax.nn.silu(zm)).astype(BF16), wm_ref[...])

    def gate(j):
        gj = proj(xn, _M_G + j * D_MODEL, D_MODEL) + bg_ref[:, j * D_MODEL:(j + 1) * D_MODEL]
        return jax.nn.sigmoid(gj)

    merged = gate(0) * ya + gate(1) * yb + gate(2) * ym
    out_ref[...] = xf + _dot(merged.astype(BF16), wo_ref[...])


def _merge(x2, attn2, mk, mv, norm_g, w_in, b_gate, conv_w, q_norm_m,
           w_out_a, w_out_b, w_out_m, w_o, batch, seq):
    n = x2.shape[0]
    tm = min(TM_C, seq)
    per_b = seq // tm
    m_len = mk.shape[1]
    wc = jnp.concatenate([w_in[:, _C_ZA:_C_ZA + 512], w_in[:, _C_REST:]], axis=1).astype(BF16)
    const = lambda shape: pl.BlockSpec(shape, lambda i: (0,) * len(shape),
                                       pipeline_mode=pl.Buffered(1))
    kern = functools.partial(_merge_kernel, per_b=per_b)
    return pl.pallas_call(
        kern,
        grid=(n // tm,),
        in_specs=[
            pl.BlockSpec((tm, D_MODEL), lambda i: (i, 0)),
            pl.BlockSpec((tm, WIDTH_A), lambda i: (i, 0)),
            const((1, D_MODEL)), const((D_MODEL, _M_COLS)), const((1, N_BRANCHES * D_MODEL)),
            const((CONV_WIDTH, WIDTH_B)),
            pl.BlockSpec((None, m_len, WIDTH_M), lambda i: (i // per_b, 0, 0)),
            pl.BlockSpec((None, m_len, WIDTH_M), lambda i: (i // per_b, 0, 0)),
            const((1, HEAD_DIM_M)),
            const((WIDTH_A, D_MODEL)), const((WIDTH_B, D_MODEL)), const((WIDTH_M, D_MODEL)),
            const((D_MODEL, D_MODEL)),
        ],
        out_specs=pl.BlockSpec((tm, D_MODEL), lambda i: (i, 0)),
        out_shape=jax.ShapeDtypeStruct((n, D_MODEL), F32),
        scratch_shapes=[pltpu.VMEM((8, WIDTH_B), F32)],
        compiler_params=pltpu.CompilerParams(
            dimension_semantics=("arbitrary",), vmem_limit_bytes=VMEM_LIMIT_BYTES),
        name="merge",
    )(x2, attn2, norm_g.reshape(1, D_MODEL), wc, b_gate.reshape(1, -1), conv_w,
      mk, mv, q_norm_m.reshape(1, HEAD_DIM_M),
      w_out_a.astype(BF16), w_out_b.astype(BF16), w_out_m.astype(BF16), w_o.astype(BF16))


def _layer(h, mem, norm_g, mem_norm_g, w_in, b_gate, w_mem_kv, q_norm_a, k_norm_a,
           q_norm_m, k_norm_m, conv_w, w_out_a, w_out_b, w_out_m, w_o):
    batch, seq, _ = h.shape
    x2 = h.reshape(batch * seq, D_MODEL)
    qat, ka, vat, qit, kib, wt = _proj_a(x2, norm_g, w_in, q_norm_a, k_norm_a, batch, seq)
    mk, mv = _mem_kv(mem, mem_norm_g, w_mem_kv, k_norm_m)
    attn = _dsa_attn(qat, qit, wt, ka, vat, kib, q_norm_a, k_norm_a, batch, seq)
    out = _merge(x2, attn.reshape(batch * seq, WIDTH_A), mk, mv, norm_g, w_in, b_gate, conv_w,
                 q_norm_m, w_out_a, w_out_b, w_out_m, w_o, batch, seq)
    return out.reshape(batch, seq, D_MODEL)


def kernel(x, mem, norm_g, mem_norm_g, w_in, b_gate, w_mem_kv, q_norm_a, k_norm_a,
           q_norm_m, k_norm_m, conv_w, w_out_a, w_out_b, w_out_m, w_o):
    h = x
    for l in range(norm_g.shape[0]):
        h = _layer(h, mem, norm_g[l], mem_norm_g[l], w_in[l], b_gate[l], w_mem_kv[l],
                   q_norm_a[l], k_norm_a[l], q_norm_m[l], k_norm_m[l], conv_w[l],
                   w_out_a[l], w_out_b[l], w_out_m[l], w_o[l])
    return h
```

```python
import functools

import jax
import jax.numpy as jnp
import numpy as np
from jax import lax
from jax.experimental import pallas as pl
from jax.experimental.pallas import tpu as pltpu

F32 = jnp.float32
BF16 = jnp.bfloat16

D_MODEL = 1024
N_HEADS_A = 8
HEAD_DIM_A = 64
WIDTH_A = 512
N_IDX_HEADS = 8
IDX_DIM = 64
TOPK_MAX = 256
WIDTH_B = 512
CONV_WIDTH = 3
N_HEADS_M = 4
HEAD_DIM_M = 128
WIDTH_M = 512
N_BRANCHES = 3
RMS_EPS = 1e-6

_C_QA, _C_KA, _C_VA, _C_ZA = 0, 512, 1024, 1536
_C_QI, _C_KI, _C_WI = 2048, 2560, 2624
_C_REST = 2632
_D_IN = 8776

VMEM_LIMIT_BYTES = 56 * 1024 * 1024

TM_A = 512
TM_C = 512
TQ = 256
FOLD_ROWS = 32
BISECT_STEPS = 20
SOFTMAX_DIRECT_BOUND = 60.0
SOFTMAX_M_INIT = -1e30


def _rms_rows(xf, eps=RMS_EPS):
    return xf * lax.rsqrt(jnp.mean(xf * xf, axis=-1, keepdims=True) + eps)


def _dot(a, b):
    return jnp.dot(a, b, preferred_element_type=F32)


def _dot_nt(a, b):
    return lax.dot_general(a, b, (((1,), (1,)), ((), ())), preferred_element_type=F32)


def _proj_a_kernel(x_ref, g_ref, w1_ref, w2t_ref, gsum_ref, gk_ref, gq_ref,
                   qat_ref, ka_ref, vat_ref, qit_ref, kib_ref, wt_ref):
    tm = x_ref.shape[0]
    xn = (_rms_rows(x_ref[...]) * g_ref[...]).astype(BF16)

    y1 = _dot(xn, w1_ref[...])
    ka_raw = y1[:, :WIDTH_A]
    sq = ka_raw * ka_raw
    sq_hi = sq.astype(BF16)
    sq_lo = (sq - sq_hi.astype(F32)).astype(BF16)
    ss = _dot(sq_hi, gsum_ref[...]) + _dot(sq_lo, gsum_ref[...])
    ka = ka_raw * lax.rsqrt(ss * (1.0 / HEAD_DIM_A) + RMS_EPS) * gk_ref[...]
    ka_ref[...] = ka.astype(BF16)
    kiwi = y1[:, WIDTH_A:]
    lane = lax.broadcasted_iota(jnp.int32, kiwi.shape, 1)
    kib_ref[...] = jnp.where(lane < IDX_DIM, kiwi, 0.0).astype(BF16)
    wt_ref[...] = kiwi.T[IDX_DIM:IDX_DIM + N_IDX_HEADS, :]

    yt = _dot_nt(w2t_ref[...], xn)
    gq = jnp.concatenate([gq_ref[...]] * (tm // 128), axis=1)
    for h in range(N_HEADS_A):
        qh = yt[h * HEAD_DIM_A:(h + 1) * HEAD_DIM_A, :]
        ms = jnp.mean(qh * qh, axis=0, keepdims=True)
        qn = qh * lax.rsqrt(ms + RMS_EPS) * gq
        qat_ref[h * HEAD_DIM_A:(h + 1) * HEAD_DIM_A, :] = (qn * (HEAD_DIM_A ** -0.5)).astype(BF16)
    vat_ref[...] = yt[WIDTH_A:2 * WIDTH_A, :].astype(BF16)
    qit_ref[...] = yt[2 * WIDTH_A:3 * WIDTH_A, :].astype(BF16)


def _proj_a(x2, norm_g, w_in, q_norm_a, k_norm_a, batch, seq):
    n = x2.shape[0]
    tm = min(TM_A, seq)
    per_b = seq // tm
    w1 = jnp.concatenate(
        [w_in[:, _C_KA:_C_KA + 512], w_in[:, _C_KI:_C_KI + 72],
         jnp.zeros((D_MODEL, 56), F32)], axis=1).astype(BF16)
    w2t = jnp.concatenate(
        [w_in[:, _C_QA:_C_QA + 512], w_in[:, _C_VA:_C_VA + 512],
         w_in[:, _C_QI:_C_QI + 512]], axis=1).T.astype(BF16)
    hid = np.arange(WIDTH_A) // HEAD_DIM_A
    gsum = jnp.asarray(hid[:, None] == hid[None, :], BF16)
    gk = jnp.tile(k_norm_a.reshape(1, HEAD_DIM_A), (1, N_HEADS_A))
    gq = jnp.broadcast_to(q_norm_a.reshape(HEAD_DIM_A, 1), (HEAD_DIM_A, 128))

    full = lambda shape: pl.BlockSpec(shape, lambda i: (0,) * len(shape))
    t_spec = lambda rows: pl.BlockSpec((None, rows, tm), lambda i: (i // per_b, 0, i % per_b))
    return pl.pallas_call(
        _proj_a_kernel,
        grid=(n // tm,),
        in_specs=[
            pl.BlockSpec((tm, D_MODEL), lambda i: (i, 0)),
            full((1, D_MODEL)), full((D_MODEL, 640)), full((1536, D_MODEL)),
            full((WIDTH_A, WIDTH_A)), full((1, WIDTH_A)), full((HEAD_DIM_A, 128)),
        ],
        out_specs=[
            t_spec(WIDTH_A),
            pl.BlockSpec((tm, WIDTH_A), lambda i: (i, 0)),
            t_spec(WIDTH_A), t_spec(WIDTH_A),
            pl.BlockSpec((tm, 128), lambda i: (i, 0)),
            t_spec(N_IDX_HEADS),
        ],
        out_shape=[
            jax.ShapeDtypeStruct((batch, WIDTH_A, seq), BF16),
            jax.ShapeDtypeStruct((n, WIDTH_A), BF16),
            jax.ShapeDtypeStruct((batch, WIDTH_A, seq), BF16),
            jax.ShapeDtypeStruct((batch, WIDTH_A, seq), BF16),
            jax.ShapeDtypeStruct((n, 128), BF16),
            jax.ShapeDtypeStruct((batch, N_IDX_HEADS, seq), F32),
        ],
        compiler_params=pltpu.CompilerParams(
            dimension_semantics=("arbitrary",), vmem_limit_bytes=VMEM_LIMIT_BYTES),
        name="proj_a",
    )(x2, norm_g.reshape(1, D_MODEL), w1, w2t, gsum, gk, gq)


def _mem_kv_kernel(mem_ref, g_ref, w_ref, gk_ref, mk_ref, mv_ref):
    mn = (_rms_rows(mem_ref[...]) * g_ref[...]).astype(BF16)
    y = _dot(mn, w_ref[...])
    for h in range(N_HEADS_M):
        kh = y[:, h * HEAD_DIM_M:(h + 1) * HEAD_DIM_M]
        mk_ref[:, h * HEAD_DIM_M:(h + 1) * HEAD_DIM_M] = (_rms_rows(kh) * gk_ref[...]).astype(BF16)
    mv_ref[...] = y[:, WIDTH_M:].astype(BF16)


def _mem_kv(mem, mem_norm_g, w_mem_kv, k_norm_m):
    batch, m_len, _ = mem.shape
    full = lambda shape: pl.BlockSpec(shape, lambda b: (0,) * len(shape))
    return pl.pallas_call(
        _mem_kv_kernel,
        grid=(batch,),
        in_specs=[
            pl.BlockSpec((None, m_len, D_MODEL), lambda b: (b, 0, 0)),
            full((1, D_MODEL)), full((D_MODEL, 2 * WIDTH_M)), full((1, HEAD_DIM_M)),
        ],
        out_specs=[pl.BlockSpec((None, m_len, WIDTH_M), lambda b: (b, 0, 0))] * 2,
        out_shape=[jax.ShapeDtypeStruct((batch, m_len, WIDTH_M), BF16)] * 2,
        compiler_params=pltpu.CompilerParams(
            dimension_semantics=("arbitrary",), vmem_limit_bytes=VMEM_LIMIT_BYTES),
        name="mem_kv",
    )(mem, mem_norm_g.reshape(1, D_MODEL), w_mem_kv.astype(BF16), k_norm_m.reshape(1, HEAD_DIM_M))


def _alibi_slope(h):
    return float(2.0 ** (-8.0 * (h + 1) / N_HEADS_A))


def _dsa_attn_kernel(qat_ref, qit_ref, wt_ref, ka_ref, vat_ref, kib_ref, direct_ref, out_ref,
                     sc_ref, ot_ref, rhsi_ref, rhsa_ref, tri_ref, m_ref, d_ref, lga_ref, lgb_ref,
                     zc_ref, zb_ref, *, seq, topk):
    tq = qat_ref.shape[1]
    ck = tq
    i = pl.program_id(1)
    t0 = i * tq
    nk = i + 1
    inf = jnp.float32(jnp.inf)
    kf = jnp.float32(topk)
    t_row = t0 + lax.broadcasted_iota(jnp.int32, (1, tq), 1)

    def fold(w, op):
        return op(w.reshape(ck // FOLD_ROWS, FOLD_ROWS, w.shape[1]), axis=0)

    def over_chunks(body, init):
        def step(c, carry):
            r0 = pl.multiple_of(c * ck, ck)
            return body(sc_ref[pl.ds(r0, ck), :], r0, carry)
        return lax.fori_loop(0, nk, step, init)

    def col(acc, op):
        return op(acc, axis=0, keepdims=True)

    facc = lambda v: jnp.full((FOLD_ROWS, tq), v, F32)

    zero_rows = jnp.zeros((128 - IDX_DIM, tq), BF16)
    for h in range(N_IDX_HEADS):
        rhsi_ref[h] = jnp.concatenate(
            [qit_ref[h * IDX_DIM:(h + 1) * IDX_DIM, :], zero_rows], axis=0)
    idx_scale = (IDX_DIM ** -0.5) * (N_IDX_HEADS ** -0.5)

    n_ahead = N_IDX_HEADS // 2

    def dots_ahead(kc):
        for h in range(n_ahead):
            lgb_ref[:, h * tq:(h + 1) * tq] = _dot(kc, rhsi_ref[h])

    dots_ahead(kib_ref[0:ck, :])

    def score_chunk(c, carry):
        lo_a, hi_a, pos_a, nonneg_a = carry
        r0 = pl.multiple_of(c * ck, ck)
        kc = kib_ref[pl.ds(r0, ck), :]
        k_next = kib_ref[pl.ds(pl.multiple_of(jnp.minimum(c + 1, nk - 1) * ck, ck), ck), :]
        acc = jnp.zeros((ck, tq), F32)
        for h in range(n_ahead):
            acc = acc + jnp.maximum(lgb_ref[:, h * tq:(h + 1) * tq], 0.0) * wt_ref[h:h + 1, :]
        for h in range(n_ahead, N_IDX_HEADS):
            d = _dot(kc, rhsi_ref[h])
            acc = acc + jnp.maximum(d, 0.0) * wt_ref[h:h + 1, :]
        dots_ahead(k_next)
        acc = acc * idx_scale
        causal = r0 + lax.broadcasted_iota(jnp.int32, (ck, tq), 0) <= t_row
        masked = jnp.where(causal, acc, -inf)
        sc_ref[pl.ds(r0, ck), :] = masked
        pos_c = fold(jnp.where(masked > 0.0, 1.0, 0.0), jnp.sum)
        nonneg_c = fold(jnp.where(masked >= 0.0, 1.0, 0.0), jnp.sum)
        zc_ref[c] = jnp.broadcast_to(col(nonneg_c - pos_c, jnp.sum), (8, tq))
        return (jnp.minimum(lo_a, fold(jnp.where(causal, acc, inf), jnp.min)),
                jnp.maximum(hi_a, fold(masked, jnp.max)),
                pos_a + pos_c, nonneg_a + nonneg_c)

    lo_a, hi_a, pos_a, nonneg_a = lax.fori_loop(
        0, nk, score_chunk, (facc(inf), facc(-inf), facc(0.0), facc(0.0)))

    def count_ge(v):
        acc = over_chunks(lambda x, r0, a: a + fold(jnp.where(x >= v, 1.0, 0.0), jnp.sum), facc(0.0))
        return col(acc, jnp.sum)

    lo_min = col(lo_a, jnp.min)
    hi_max = col(hi_a, jnp.max)
    n_causal = (t_row + 1).astype(F32)
    short = n_causal < kf

    def bis_step(lo, hi, cl):
        mid = lo * 0.5 + hi * 0.5
        c = count_ge(mid)
        ok = c >= kf
        return jnp.where(ok, mid, lo), jnp.where(ok, hi, mid), jnp.where(ok, c, cl)

    cl0 = jnp.where(short, kf, n_causal)
    lo, _, cl = lax.fori_loop(0, BISECT_STEPS, lambda _, s: bis_step(*s), (lo_min, hi_max, cl0))

    def walk_cond(st):
        return jnp.max(st[3]) > 0.0

    def walk_body(st):
        lo, thr, take, todo = st
        lo_e = col(over_chunks(
            lambda x, r0, a: jnp.minimum(a, fold(jnp.where(x >= lo, x, inf), jnp.min)), facc(inf)), jnp.min)

        def above(x, r0, carry):
            cgt_a, nxt_a = carry
            gt = x > lo_e
            return (cgt_a + fold(jnp.where(gt, 1.0, 0.0), jnp.sum),
                    jnp.minimum(nxt_a, fold(jnp.where(gt, x, inf), jnp.min)))

        cgt_a, nxt_a = over_chunks(above, (facc(0.0), facc(inf)))
        cgt, nxt = col(cgt_a, jnp.sum), col(nxt_a, jnp.min)
        active = todo > 0.0
        fin = active & (cgt < kf)
        thr = jnp.where(fin, lo_e, thr)
        take = jnp.where(fin, kf - cgt, take)
        lo = jnp.where(active & (cgt >= kf), nxt, lo)
        todo = jnp.where(fin, 0.0, todo)
        return lo, thr, take, todo

    n_pos = col(pos_a, jnp.sum)
    zero_tie = jnp.logical_not(short) & (n_pos < kf) & (col(nonneg_a, jnp.sum) >= kf)
    thr0 = jnp.where(zero_tie, 0.0, jnp.where(short, lo_min, lo))
    take0 = jnp.where(zero_tie, kf - n_pos, float(seq))
    todo0 = jnp.where(zero_tie | (cl == kf), 0.0, 1.0)
    _, thr, take, _ = lax.while_loop(walk_cond, walk_body, (lo, thr0, take0, todo0))

    def alibi_dist(r0):
        s_idx = r0 + lax.broadcasted_iota(jnp.int32, (ck, tq), 0)
        return (t_row - s_idx).astype(F32)

    tied = take < float(seq)
    far = jnp.float32(seq)

    def locate(c, carry):
        zeros_before, cut_chunk = carry
        zb_ref[c] = jnp.broadcast_to(zeros_before, (8, tq))
        zeros_upto = zeros_before + zc_ref[c][0:1, :]
        hit = zero_tie & (cut_chunk >= far) & (zeros_upto >= take)
        return zeros_upto, jnp.where(hit, c.astype(F32), cut_chunk)

    _, cut_chunk = lax.fori_loop(0, nk, locate, (jnp.zeros((1, tq), F32), jnp.full((1, tq), far)))
    other_tie = jnp.max(jnp.where(tied & jnp.logical_not(zero_tie), 1.0, 0.0)) > 0.0
    first_cut = jnp.min(cut_chunk).astype(jnp.int32)
    last_cut = jnp.max(jnp.where(cut_chunk < far, cut_chunk, -1.0)).astype(jnp.int32)
    ca = jnp.where(other_tie, 0, jnp.minimum(first_cut, nk))
    cb = jnp.where(other_tie, nk - 1, last_cut)
    ties_at_ca = jnp.where(other_tie, 0.0, zb_ref[jnp.minimum(ca, nk - 1)][0:1, :])

    r_i = lax.broadcasted_iota(jnp.int32, (ck, ck), 0)
    c_i = lax.broadcasted_iota(jnp.int32, (ck, ck), 1)
    tri_ref[...] = jnp.where(c_i <= r_i, 1.0, 0.0).astype(BF16)

    def keep_all(c, dmin_a):
        r0 = pl.multiple_of(c * ck, ck)
        x = sc_ref[pl.ds(r0, ck), :]
        pen = jnp.where(x >= thr, alibi_dist(r0), inf)
        sc_ref[pl.ds(r0, ck), :] = pen
        return jnp.minimum(dmin_a, fold(pen, jnp.min))

    def ranked(c, carry):
        ties_before, dmin_a = carry
        r0 = pl.multiple_of(c * ck, ck)
        x = sc_ref[pl.ds(r0, ck), :]
        dist = alibi_dist(r0)
        eq = x == thr
        rank = _dot(tri_ref[...], jnp.where(eq, 1.0, 0.0).astype(BF16)) + ties_before
        tie = jnp.where(eq, jnp.where(rank <= take, dist, inf), inf)
        pen = jnp.where(x > thr, dist, tie)
        sc_ref[pl.ds(r0, ck), :] = pen
        return rank[ck - 1:ck, :], jnp.minimum(dmin_a, fold(pen, jnp.min))

    def keep_untied(c, dmin_a):
        r0 = pl.multiple_of(c * ck, ck)
        x = sc_ref[pl.ds(r0, ck), :]
        dist = alibi_dist(r0)
        tie = jnp.where(x == thr, jnp.where(tied, inf, dist), inf)
        pen = jnp.where(x > thr, dist, tie)
        sc_ref[pl.ds(r0, ck), :] = pen
        return jnp.minimum(dmin_a, fold(pen, jnp.min))

    dmin_a = lax.fori_loop(0, ca, keep_all, facc(inf))
    _, dmin_a = lax.fori_loop(ca, cb + 1, ranked, (ties_at_ca, dmin_a))
    dmin_a = lax.fori_loop(jnp.maximum(cb + 1, ca), nk, keep_untied, dmin_a)

    dmin = col(dmin_a, jnp.min)

    row = lax.broadcasted_iota(jnp.int32, (128, tq), 0)
    for p in range(N_HEADS_A // 2):
        qpair = qat_ref[p * 128:(p + 1) * 128, :]
        rhsa_ref[p] = jnp.concatenate(
            [jnp.where(row < HEAD_DIM_A, qpair, jnp.zeros_like(qpair)),
             jnp.where(row >= HEAD_DIM_A, qpair, jnp.zeros_like(qpair))], axis=1)

    m_ref[...] = jnp.full((N_HEADS_A, tq), SOFTMAX_M_INIT, F32)
    d_ref[...] = jnp.zeros((N_HEADS_A, tq), F32)
    ot_ref[...] = jnp.zeros((WIDTH_A, tq), F32)

    def k_chunk(c):
        r0 = pl.multiple_of(c * ck, ck)
        return [ka_ref[pl.ds(r0, ck), p * 128:(p + 1) * 128] for p in range(N_HEADS_A // 2)]

    def qk_pair(buf, p, kc):
        buf[:, p * 2 * tq:(p + 1) * 2 * tq] = _dot(kc, rhsa_ref[p])

    def qk_into(buf, kcs):
        for p in range(N_HEADS_A // 2):
            qk_pair(buf, p, kcs[p])

    def step(cur, nxt, c, c_next):
        r0 = pl.multiple_of(c * ck, ck)
        pen = sc_ref[pl.ds(r0, ck), :] - dmin
        vts = [vat_ref[h * HEAD_DIM_A:(h + 1) * HEAD_DIM_A, pl.ds(r0, ck)]
               for h in range(N_HEADS_A)]
        kcs = k_chunk(c_next)
        qk_pair(nxt, 0, kcs[0])
        for p in range(N_HEADS_A // 2):
            m_seen = soft_pv(cur, pen, vts, (2 * p, 2 * p + 1))
            if p + 1 < N_HEADS_A // 2:
                tie = jnp.where(m_seen != m_seen, 1.0, 0.0).astype(BF16)
                qk_pair(nxt, p + 1, kcs[p + 1] + tie)

    def soft_pv(buf, pen, vts, heads):
        m_seen = None
        for h in heads:
            rows = slice(h * HEAD_DIM_A, (h + 1) * HEAD_DIM_A)
            alphas, probs = [], []
            for lt in range(tq // 128):
                ln = slice(lt * 128, (lt + 1) * 128)
                m = m_ref[h:h + 1, ln]
                lg = buf[:, h * tq + lt * 128:h * tq + (lt + 1) * 128] - _alibi_slope(h) * pen[:, ln]
                m_new = jnp.maximum(m, col(fold(lg, jnp.max), jnp.max))
                alpha = jnp.exp(m - m_new)
                pr = jnp.exp(lg - m_new)
                m_ref[h:h + 1, ln] = m_new
                d_ref[h:h + 1, ln] = d_ref[h:h + 1, ln] * alpha + col(fold(pr, jnp.sum), jnp.sum)
                alphas.append(alpha)
                probs.append(pr.astype(BF16))
                m_seen = m_new if m_seen is None else m_seen
            ot_ref[rows, :] = (ot_ref[rows, :] * jnp.concatenate(alphas, axis=1)
                               + _dot(vts[h], jnp.concatenate(probs, axis=1)))
        return m_seen

    last = nk - 1
    lg0_ref = lga_ref.at[:, 0:2 * tq]

    def direct_chunk(c, carry):
        r0 = pl.multiple_of(c * ck, ck)
        pen = sc_ref[pl.ds(r0, ck), :] - dmin
        vts = [vat_ref[h * HEAD_DIM_A:(h + 1) * HEAD_DIM_A, pl.ds(r0, ck)]
               for h in range(N_HEADS_A)]
        kcs = k_chunk(c)
        k_next = ka_ref[pl.ds(pl.multiple_of(jnp.minimum(c + 1, last) * ck, ck), ck), 0:128]
        lg2 = lg0_ref[...]
        for p in range(N_HEADS_A // 2):
            lg2_next = None
            for j in range(2):
                h = 2 * p + j
                rows = slice(h * HEAD_DIM_A, (h + 1) * HEAD_DIM_A)
                probs = []
                for lt in range(tq // 128):
                    ln = slice(lt * 128, (lt + 1) * 128)
                    e = jnp.exp(lg2[:, j * tq + lt * 128:j * tq + (lt + 1) * 128]
                                - _alibi_slope(h) * pen[:, ln])
                    esum = col(fold(e, jnp.sum), jnp.sum)
                    d_ref[h:h + 1, ln] = d_ref[h:h + 1, ln] + esum
                    probs.append(e.astype(BF16))
                    if j == 0 and lt == 0:
                        if p + 1 < N_HEADS_A // 2:
                            lg2_next = _dot(kcs[p + 1], rhsa_ref[p + 1])
                        if p + 2 == N_HEADS_A // 2:
                            lg0_ref[...] = _dot(k_next, rhsa_ref[0])
                ot_ref[rows, :] = ot_ref[rows, :] + _dot(vts[h], jnp.concatenate(probs, axis=1))
            lg2 = lg2_next
        return carry

    @pl.when(direct_ref[0] > 0)
    def _():
        lg0_ref[...] = _dot(ka_ref[0:ck, 0:128], rhsa_ref[0])
        lax.fori_loop(0, nk, direct_chunk, 0)

    @pl.when(direct_ref[0] <= 0)
    def _():
        qk_into(lga_ref, k_chunk(0))

        def attn_pair(cc, carry):
            c0 = 2 * cc
            step(lga_ref, lgb_ref, c0, jnp.minimum(c0 + 1, last))

            @pl.when(c0 + 1 < nk)
            def _():
                step(lgb_ref, lga_ref, c0 + 1, jnp.minimum(c0 + 2, last))

            return carry

        lax.fori_loop(0, (nk + 1) // 2, attn_pair, 0)

    for h in range(N_HEADS_A):
        rows = slice(h * HEAD_DIM_A, (h + 1) * HEAD_DIM_A)
        ot_ref[rows, :] = ot_ref[rows, :] / d_ref[h:h + 1, :]
    out_ref[...] = ot_ref[...].T


def _dsa_attn(qat, qit, wt, ka, vat, kib, q_norm_a, k_norm_a, batch, seq):
    tq = min(TQ, seq)
    topk = min(TOPK_MAX, seq // 4)
    qk_bound = HEAD_DIM_A ** 0.5 * jnp.max(jnp.abs(q_norm_a)) * jnp.max(jnp.abs(k_norm_a)) * 1.01
    direct = (qk_bound <= SOFTMAX_DIRECT_BOUND).astype(jnp.int32).reshape(1)
    kern = functools.partial(_dsa_attn_kernel, seq=seq, topk=topk)
    q_spec = lambda rows: pl.BlockSpec((None, rows, tq), lambda b, i: (b, 0, i))
    return pl.pallas_call(
        kern,
        grid=(batch, seq // tq),
        in_specs=[
            q_spec(WIDTH_A), q_spec(WIDTH_A), q_spec(N_IDX_HEADS),
            pl.BlockSpec((None, seq, WIDTH_A), lambda b, i: (b, 0, 0)),
            pl.BlockSpec((None, WIDTH_A, seq), lambda b, i: (b, 0, 0)),
            pl.BlockSpec((None, seq, 128), lambda b, i: (b, 0, 0)),
            pl.BlockSpec(memory_space=pltpu.SMEM),
        ],
        out_specs=pl.BlockSpec((None, tq, WIDTH_A), lambda b, i: (b, i, 0)),
        out_shape=jax.ShapeDtypeStruct((batch, seq, WIDTH_A), F32),
        scratch_shapes=[
            pltpu.VMEM((seq, tq), F32),
            pltpu.VMEM((WIDTH_A, tq), F32),
            pltpu.VMEM((N_IDX_HEADS, 128, tq), BF16),
            pltpu.VMEM((N_HEADS_A // 2, 128, 2 * tq), BF16),
            pltpu.VMEM((tq, tq), BF16),
            pltpu.VMEM((N_HEADS_A, tq), F32),
            pltpu.VMEM((N_HEADS_A, tq), F32),
            pltpu.VMEM((tq, N_HEADS_A * tq), F32),
            pltpu.VMEM((tq, N_HEADS_A * tq), F32),
            pltpu.VMEM((seq // tq, 8, tq), F32),
            pltpu.VMEM((seq // tq, 8, tq), F32),
        ],
        compiler_params=pltpu.CompilerParams(
            dimension_semantics=("arbitrary", "arbitrary"), vmem_limit_bytes=VMEM_LIMIT_BYTES),
        name="dsa_attn",
    )(qat, qit, wt, ka.reshape(batch, seq, WIDTH_A), vat, kib.reshape(batch, seq, 128), direct)


_M_ZA, _M_BG, _M_CG, _M_HB, _M_ZB, _M_QM, _M_ZM, _M_G = (
    0, 512, 1024, 1536, 2048, 2560, 3072, 3584)
_M_COLS = 3584 + N_BRANCHES * D_MODEL


def _merge_kernel(x_ref, attn_ref, g_ref, wc_ref, bg_ref, cw_ref, mk_ref, mv_ref,
                  gqm_ref, wa_ref, wb_ref, wm_ref, wo_ref, out_ref, utail_ref, *, per_b):
    tm = x_ref.shape[0]
    xf = x_ref[...]
    xn = (_rms_rows(xf) * g_ref[...]).astype(BF16)

    def proj(lhs, c0, width):
        return _dot(lhs, wc_ref[:, c0:c0 + width])

    za = proj(xn, _M_ZA, WIDTH_A)
    ya = _dot((attn_ref[...] * jax.nn.silu(za)).astype(BF16), wa_ref[...])

    u = proj(xn, _M_CG, WIDTH_B) * proj(xn, _M_HB, WIDTH_B)
    @pl.when((pl.program_id(0) % per_b) == 0)
    def _():
        utail_ref[...] = jnp.zeros((8, WIDTH_B), F32)

    uh = utail_ref[...]
    utail_ref[...] = u[tm - 8:, :]
    rows = lax.broadcasted_iota(jnp.int32, (tm, WIDTH_B), 0)
    u1 = jnp.where(rows == 0, uh[7:8, :], pltpu.roll(u, 1, 0))
    u2 = jnp.where(rows == 0, uh[6:7, :], jnp.where(rows == 1, uh[7:8, :], pltpu.roll(u, 2, 0)))
    conv = cw_ref[0:1, :] * u2 + cw_ref[1:2, :] * u1 + cw_ref[2:3, :] * u
    bgate = proj(xn, _M_BG, WIDTH_B)
    zb = proj(xn, _M_ZB, WIDTH_B)
    yb = _dot(((bgate * conv) * jax.nn.silu(zb)).astype(BF16), wb_ref[...])

    qm = proj(xn, _M_QM, WIDTH_M)
    zm = proj(xn, _M_ZM, WIDTH_M)
    heads = []
    for h in range(N_HEADS_M):
        sl = slice(h * HEAD_DIM_M, (h + 1) * HEAD_DIM_M)
        qh = (_rms_rows(qm[:, sl]) * gqm_ref[...]).astype(BF16)
        lg = _dot_nt(qh, mk_ref[:, sl]) * (HEAD_DIM_M ** -0.5)
        e = jnp.exp(lg - jnp.max(lg, axis=-1, keepdims=True))
        p = e / jnp.sum(e, axis=-1, keepdims=True)
        heads.append(_dot(p.astype(BF16), mv_ref[:, sl]))
    attn_m = jnp.concatenate(heads, axis=1)
    ym = _dot((attn_m * jax.nn.silu(zm)).astype(BF16), wm_ref[...])

    def gate(j):
        gj = proj(xn, _M_G + j * D_MODEL, D_MODEL) + bg_ref[:, j * D_MODEL:(j + 1) * D_MODEL]
        return jax.nn.sigmoid(gj)

    merged = gate(0) * ya + gate(1) * yb + gate(2) * ym
    out_ref[...] = xf + _dot(merged.astype(BF16), wo_ref[...])


def _merge(x2, attn2, mk, mv, norm_g, w_in, b_gate, conv_w, q_norm_m,
           w_out_a, w_out_b, w_out_m, w_o, batch, seq):
    n = x2.shape[0]
    tm = min(TM_C, seq)
    per_b = seq // tm
    m_len = mk.shape[1]
    wc = jnp.concatenate([w_in[:, _C_ZA:_C_ZA + 512], w_in[:, _C_REST:]], axis=1).astype(BF16)
    const = lambda shape: pl.BlockSpec(shape, lambda i: (0,) * len(shape),
                                       pipeline_mode=pl.Buffered(1))
    kern = functools.partial(_merge_kernel, per_b=per_b)
    return pl.pallas_call(
        kern,
        grid=(n // tm,),
        in_specs=[
            pl.BlockSpec((tm, D_MODEL), lambda i: (i, 0)),
            pl.BlockSpec((tm, WIDTH_A), lambda i: (i, 0)),
            const((1, D_MODEL)), const((D_MODEL, _M_COLS)), const((1, N_BRANCHES * D_MODEL)),
            const((CONV_WIDTH, WIDTH_B)),
            pl.BlockSpec((None, m_len, WIDTH_M), lambda i: (i // per_b, 0, 0)),
            pl.BlockSpec((None, m_len, WIDTH_M), lambda i: (i // per_b, 0, 0)),
            const((1, HEAD_DIM_M)),
            const((WIDTH_A, D_MODEL)), const((WIDTH_B, D_MODEL)), const((WIDTH_M, D_MODEL)),
            const((D_MODEL, D_MODEL)),
        ],
        out_specs=pl.BlockSpec((tm, D_MODEL), lambda i: (i, 0)),
        out_shape=jax.ShapeDtypeStruct((n, D_MODEL), F32),
        scratch_shapes=[pltpu.VMEM((8, WIDTH_B), F32)],
        compiler_params=pltpu.CompilerParams(
            dimension_semantics=("arbitrary",), vmem_limit_bytes=VMEM_LIMIT_BYTES),
        name="merge",
    )(x2, attn2, norm_g.reshape(1, D_MODEL), wc, b_gate.reshape(1, -1), conv_w,
      mk, mv, q_norm_m.reshape(1, HEAD_DIM_M),
      w_out_a.astype(BF16), w_out_b.astype(BF16), w_out_m.astype(BF16), w_o.astype(BF16))


def _layer(h, mem, norm_g, mem_norm_g, w_in, b_gate, w_mem_kv, q_norm_a, k_norm_a,
           q_norm_m, k_norm_m, conv_w, w_out_a, w_out_b, w_out_m, w_o):
    batch, seq, _ = h.shape
    x2 = h.reshape(batch * seq, D_MODEL)
    qat, ka, vat, qit, kib, wt = _proj_a(x2, norm_g, w_in, q_norm_a, k_norm_a, batch, seq)
    mk, mv = _mem_kv(mem, mem_norm_g, w_mem_kv, k_norm_m)
    attn = _dsa_attn(qat, qit, wt, ka, vat, kib, q_norm_a, k_norm_a, batch, seq)
    out = _merge(x2, attn.reshape(batch * seq, WIDTH_A), mk, mv, norm_g, w_in, b_gate, conv_w,
                 q_norm_m, w_out_a, w_out_b, w_out_m, w_o, batch, seq)
    return out.reshape(batch, seq, D_MODEL)


def kernel(x, mem, norm_g, mem_norm_g, w_in, b_gate, w_mem_kv, q_norm_a, k_norm_a,
           q_norm_m, k_norm_m, conv_w, w_out_a, w_out_b, w_out_m, w_o):
    h = x
    for l in range(norm_g.shape[0]):
        h = _layer(h, mem, norm_g[l], mem_norm_g[l], w_in[l], b_gate[l], w_mem_kv[l],
                   q_norm_a[l], k_norm_a[l], q_norm_m[l], k_norm_m[l], conv_w[l],
                   w_out_a[l], w_out_b[l], w_out_m[l], w_o[l])
    return h
```

```python
import functools

import jax
import jax.numpy as jnp
import numpy as np
from jax import lax
from jax.experimental import pallas as pl
from jax.experimental.pallas import tpu as pltpu

F32 = jnp.float32
BF16 = jnp.bfloat16

D_MODEL = 1024
N_HEADS_A = 8
HEAD_DIM_A = 64
WIDTH_A = 512
N_IDX_HEADS = 8
IDX_DIM = 64
TOPK_MAX = 256
WIDTH_B = 512
CONV_WIDTH = 3
N_HEADS_M = 4
HEAD_DIM_M = 128
WIDTH_M = 512
N_BRANCHES = 3
RMS_EPS = 1e-6

_C_QA, _C_KA, _C_VA, _C_ZA = 0, 512, 1024, 1536
_C_QI, _C_KI, _C_WI = 2048, 2560, 2624
_C_REST = 2632
_D_IN = 8776

VMEM_LIMIT_BYTES = 56 * 1024 * 1024

TM_A = 512
TM_C = 512
TQ = 256
FOLD_ROWS = 32
BISECT_STEPS = 19
SOFTMAX_DIRECT_BOUND = 60.0
SOFTMAX_M_INIT = -1e30


def _rms_rows(xf, eps=RMS_EPS):
    return xf * lax.rsqrt(jnp.mean(xf * xf, axis=-1, keepdims=True) + eps)


def _dot(a, b):
    return jnp.dot(a, b, preferred_element_type=F32)


def _dot_nt(a, b):
    return lax.dot_general(a, b, (((1,), (1,)), ((), ())), preferred_element_type=F32)


def _proj_a_kernel(x_ref, g_ref, w1_ref, w2t_ref, gsum_ref, gk_ref, gq_ref,
                   qat_ref, ka_ref, vat_ref, qit_ref, kib_ref, wt_ref):
    tm = x_ref.shape[0]
    xn = (_rms_rows(x_ref[...]) * g_ref[...]).astype(BF16)

    y1 = _dot(xn, w1_ref[...])
    ka_raw = y1[:, :WIDTH_A]
    sq = ka_raw * ka_raw
    sq_hi = sq.astype(BF16)
    sq_lo = (sq - sq_hi.astype(F32)).astype(BF16)
    ss = _dot(sq_hi, gsum_ref[...]) + _dot(sq_lo, gsum_ref[...])
    ka = ka_raw * lax.rsqrt(ss * (1.0 / HEAD_DIM_A) + RMS_EPS) * gk_ref[...]
    ka_ref[...] = ka.astype(BF16)
    kiwi = y1[:, WIDTH_A:]
    lane = lax.broadcasted_iota(jnp.int32, kiwi.shape, 1)
    kib_ref[...] = jnp.where(lane < IDX_DIM, kiwi, 0.0).astype(BF16)
    wt_ref[...] = kiwi.T[IDX_DIM:IDX_DIM + N_IDX_HEADS, :]

    yt = _dot_nt(w2t_ref[...], xn)
    gq = jnp.concatenate([gq_ref[...]] * (tm // 128), axis=1)
    for h in range(N_HEADS_A):
        qh = yt[h * HEAD_DIM_A:(h + 1) * HEAD_DIM_A, :]
        ms = jnp.mean(qh * qh, axis=0, keepdims=True)
        qn = qh * lax.rsqrt(ms + RMS_EPS) * gq
        qat_ref[h * HEAD_DIM_A:(h + 1) * HEAD_DIM_A, :] = (qn * (HEAD_DIM_A ** -0.5)).astype(BF16)
    vat_ref[...] = yt[WIDTH_A:2 * WIDTH_A, :].astype(BF16)
    qit_ref[...] = yt[2 * WIDTH_A:3 * WIDTH_A, :].astype(BF16)


def _proj_a(x2, norm_g, w_in, q_norm_a, k_norm_a, batch, seq):
    n = x2.shape[0]
    tm = min(TM_A, seq)
    per_b = seq // tm
    w1 = jnp.concatenate(
        [w_in[:, _C_KA:_C_KA + 512], w_in[:, _C_KI:_C_KI + 72],
         jnp.zeros((D_MODEL, 56), F32)], axis=1).astype(BF16)
    w2t = jnp.concatenate(
        [w_in[:, _C_QA:_C_QA + 512], w_in[:, _C_VA:_C_VA + 512],
         w_in[:, _C_QI:_C_QI + 512]], axis=1).T.astype(BF16)
    hid = np.arange(WIDTH_A) // HEAD_DIM_A
    gsum = jnp.asarray(hid[:, None] == hid[None, :], BF16)
    gk = jnp.tile(k_norm_a.reshape(1, HEAD_DIM_A), (1, N_HEADS_A))
    gq = jnp.broadcast_to(q_norm_a.reshape(HEAD_DIM_A, 1), (HEAD_DIM_A, 128))

    full = lambda shape: pl.BlockSpec(shape, lambda i: (0,) * len(shape))
    t_spec = lambda rows: pl.BlockSpec((None, rows, tm), lambda i: (i // per_b, 0, i % per_b))
    return pl.pallas_call(
        _proj_a_kernel,
        grid=(n // tm,),
        in_specs=[
            pl.BlockSpec((tm, D_MODEL), lambda i: (i, 0)),
            full((1, D_MODEL)), full((D_MODEL, 640)), full((1536, D_MODEL)),
            full((WIDTH_A, WIDTH_A)), full((1, WIDTH_A)), full((HEAD_DIM_A, 128)),
        ],
        out_specs=[
            t_spec(WIDTH_A),
            pl.BlockSpec((tm, WIDTH_A), lambda i: (i, 0)),
            t_spec(WIDTH_A), t_spec(WIDTH_A),
            pl.BlockSpec((tm, 128), lambda i: (i, 0)),
            t_spec(N_IDX_HEADS),
        ],
        out_shape=[
            jax.ShapeDtypeStruct((batch, WIDTH_A, seq), BF16),
            jax.ShapeDtypeStruct((n, WIDTH_A), BF16),
            jax.ShapeDtypeStruct((batch, WIDTH_A, seq), BF16),
            jax.ShapeDtypeStruct((batch, WIDTH_A, seq), BF16),
            jax.ShapeDtypeStruct((n, 128), BF16),
            jax.ShapeDtypeStruct((batch, N_IDX_HEADS, seq), F32),
        ],
        compiler_params=pltpu.CompilerParams(
            dimension_semantics=("arbitrary",), vmem_limit_bytes=VMEM_LIMIT_BYTES),
        name="proj_a",
    )(x2, norm_g.reshape(1, D_MODEL), w1, w2t, gsum, gk, gq)


def _mem_kv_kernel(mem_ref, g_ref, w_ref, gk_ref, mk_ref, mv_ref):
    mn = (_rms_rows(mem_ref[...]) * g_ref[...]).astype(BF16)
    y = _dot(mn, w_ref[...])
    for h in range(N_HEADS_M):
        kh = y[:, h * HEAD_DIM_M:(h + 1) * HEAD_DIM_M]
        mk_ref[:, h * HEAD_DIM_M:(h + 1) * HEAD_DIM_M] = (_rms_rows(kh) * gk_ref[...]).astype(BF16)
    mv_ref[...] = y[:, WIDTH_M:].astype(BF16)


def _mem_kv(mem, mem_norm_g, w_mem_kv, k_norm_m):
    batch, m_len, _ = mem.shape
    full = lambda shape: pl.BlockSpec(shape, lambda b: (0,) * len(shape))
    return pl.pallas_call(
        _mem_kv_kernel,
        grid=(batch,),
        in_specs=[
            pl.BlockSpec((None, m_len, D_MODEL), lambda b: (b, 0, 0)),
            full((1, D_MODEL)), full((D_MODEL, 2 * WIDTH_M)), full((1, HEAD_DIM_M)),
        ],
        out_specs=[pl.BlockSpec((None, m_len, WIDTH_M), lambda b: (b, 0, 0))] * 2,
        out_shape=[jax.ShapeDtypeStruct((batch, m_len, WIDTH_M), BF16)] * 2,
        compiler_params=pltpu.CompilerParams(
            dimension_semantics=("arbitrary",), vmem_limit_bytes=VMEM_LIMIT_BYTES),
        name="mem_kv",
    )(mem, mem_norm_g.reshape(1, D_MODEL), w_mem_kv.astype(BF16), k_norm_m.reshape(1, HEAD_DIM_M))


def _alibi_slope(h):
    return float(2.0 ** (-8.0 * (h + 1) / N_HEADS_A))


def _dsa_attn_kernel(qat_ref, qit_ref, wt_ref, ka_ref, vat_ref, kib_ref, direct_ref, out_ref,
                     sc_ref, ot_ref, rhsi_ref, rhsa_ref, tri_ref, m_ref, d_ref, lga_ref, lgb_ref,
                     zc_ref, zb_ref, *, seq, topk):
    tq = qat_ref.shape[1]
    ck = tq
    i = pl.program_id(1)
    t0 = i * tq
    nk = i + 1
    inf = jnp.float32(jnp.inf)
    kf = jnp.float32(topk)
    t_row = t0 + lax.broadcasted_iota(jnp.int32, (1, tq), 1)

    def fold(w, op):
        return op(w.reshape(ck // FOLD_ROWS, FOLD_ROWS, w.shape[1]), axis=0)

    def over_chunks(body, init):
        def step(c, carry):
            r0 = pl.multiple_of(c * ck, ck)
            return body(sc_ref[pl.ds(r0, ck), :], r0, carry)
        return lax.fori_loop(0, nk, step, init)

    def col(acc, op):
        return op(acc, axis=0, keepdims=True)

    facc = lambda v: jnp.full((FOLD_ROWS, tq), v, F32)

    zero_rows = jnp.zeros((128 - IDX_DIM, tq), BF16)
    for h in range(N_IDX_HEADS):
        rhsi_ref[h] = jnp.concatenate(
            [qit_ref[h * IDX_DIM:(h + 1) * IDX_DIM, :], zero_rows], axis=0)
    idx_scale = (IDX_DIM ** -0.5) * (N_IDX_HEADS ** -0.5)

    n_ahead = N_IDX_HEADS // 2

    def dots_ahead(kc):
        for h in range(n_ahead):
            lgb_ref[:, h * tq:(h + 1) * tq] = _dot(kc, rhsi_ref[h])

    dots_ahead(kib_ref[0:ck, :])

    def score_chunk(c, carry):
        lo_a, hi_a, pos_a, nonneg_a = carry
        r0 = pl.multiple_of(c * ck, ck)
        kc = kib_ref[pl.ds(r0, ck), :]
        k_next = kib_ref[pl.ds(pl.multiple_of(jnp.minimum(c + 1, nk - 1) * ck, ck), ck), :]
        acc = jnp.zeros((ck, tq), F32)
        for h in range(n_ahead):
            acc = acc + jnp.maximum(lgb_ref[:, h * tq:(h + 1) * tq], 0.0) * wt_ref[h:h + 1, :]
        for h in range(n_ahead, N_IDX_HEADS):
            d = _dot(kc, rhsi_ref[h])
            acc = acc + jnp.maximum(d, 0.0) * wt_ref[h:h + 1, :]
        dots_ahead(k_next)
        acc = acc * idx_scale
        causal = r0 + lax.broadcasted_iota(jnp.int32, (ck, tq), 0) <= t_row
        masked = jnp.where(causal, acc, -inf)
        sc_ref[pl.ds(r0, ck), :] = masked
        pos_c = fold(jnp.where(masked > 0.0, 1.0, 0.0), jnp.sum)
        nonneg_c = fold(jnp.where(masked >= 0.0, 1.0, 0.0), jnp.sum)
        zc_ref[c] = jnp.broadcast_to(col(nonneg_c - pos_c, jnp.sum), (8, tq))
        return (jnp.minimum(lo_a, fold(jnp.where(causal, acc, inf), jnp.min)),
                jnp.maximum(hi_a, fold(masked, jnp.max)),
                pos_a + pos_c, nonneg_a + nonneg_c)

    lo_a, hi_a, pos_a, nonneg_a = lax.fori_loop(
        0, nk, score_chunk, (facc(inf), facc(-inf), facc(0.0), facc(0.0)))

    def count_ge(v):
        acc = over_chunks(lambda x, r0, a: a + fold(jnp.where(x >= v, 1.0, 0.0), jnp.sum), facc(0.0))
        return col(acc, jnp.sum)

    lo_min = col(lo_a, jnp.min)
    hi_max = col(hi_a, jnp.max)
    n_causal = (t_row + 1).astype(F32)
    short = n_causal < kf

    def bis_step(lo, hi, cl):
        mid = lo * 0.5 + hi * 0.5
        c = count_ge(mid)
        ok = c >= kf
        return jnp.where(ok, mid, lo), jnp.where(ok, hi, mid), jnp.where(ok, c, cl)

    n_pos = col(pos_a, jnp.sum)
    n_nonneg = col(nonneg_a, jnp.sum)
    above_zero = n_nonneg >= kf
    lo0 = jnp.where(above_zero, 0.0, lo_min)
    hi0 = jnp.where(above_zero, hi_max, 0.0)
    cl0 = jnp.where(short, kf, jnp.where(above_zero, n_nonneg, n_causal))
    lo, _, cl = lax.fori_loop(0, BISECT_STEPS, lambda _, s: bis_step(*s), (lo0, hi0, cl0))

    def walk_cond(st):
        return jnp.max(st[3]) > 0.0

    def walk_body(st):
        lo, thr, take, todo = st
        lo_e = col(over_chunks(
            lambda x, r0, a: jnp.minimum(a, fold(jnp.where(x >= lo, x, inf), jnp.min)), facc(inf)), jnp.min)

        def above(x, r0, carry):
            cgt_a, cge_a, nxt_a = carry
            gt = x > lo_e
            return (cgt_a + fold(jnp.where(gt, 1.0, 0.0), jnp.sum),
                    cge_a + fold(jnp.where(x >= lo_e, 1.0, 0.0), jnp.sum),
                    jnp.minimum(nxt_a, fold(jnp.where(gt, x, inf), jnp.min)))

        cgt_a, cge_a, nxt_a = over_chunks(above, (facc(0.0), facc(0.0), facc(inf)))
        cgt, cge, nxt = col(cgt_a, jnp.sum), col(cge_a, jnp.sum), col(nxt_a, jnp.min)
        active = todo > 0.0
        fin = active & (cgt < kf)
        thr = jnp.where(fin, lo_e, thr)
        take = jnp.where(fin & (cge > kf), kf - cgt, take)
        lo = jnp.where(active & (cgt >= kf), nxt, lo)
        todo = jnp.where(fin, 0.0, todo)
        return lo, thr, take, todo

    zero_tie = jnp.logical_not(short) & (n_pos < kf) & above_zero
    thr0 = jnp.where(zero_tie, 0.0, jnp.where(short, lo_min, lo))
    take0 = jnp.where(zero_tie, kf - n_pos, float(seq))
    todo0 = jnp.where(zero_tie | (cl == kf), 0.0, 1.0)
    _, thr, take, _ = lax.while_loop(walk_cond, walk_body, (lo, thr0, take0, todo0))

    def alibi_dist(r0):
        s_idx = r0 + lax.broadcasted_iota(jnp.int32, (ck, tq), 0)
        return (t_row - s_idx).astype(F32)

    tied = take < float(seq)
    far = jnp.float32(seq)

    def locate(c, carry):
        zeros_before, cut_chunk = carry
        zb_ref[c] = jnp.broadcast_to(zeros_before, (8, tq))
        zeros_upto = zeros_before + zc_ref[c][0:1, :]
        hit = zero_tie & (cut_chunk >= far) & (zeros_upto >= take)
        return zeros_upto, jnp.where(hit, c.astype(F32), cut_chunk)

    _, cut_chunk = lax.fori_loop(0, nk, locate, (jnp.zeros((1, tq), F32), jnp.full((1, tq), far)))
    other_tie = jnp.max(jnp.where(tied & jnp.logical_not(zero_tie), 1.0, 0.0)) > 0.0
    first_cut = jnp.min(cut_chunk).astype(jnp.int32)
    last_cut = jnp.max(jnp.where(cut_chunk < far, cut_chunk, -1.0)).astype(jnp.int32)
    ca = jnp.where(other_tie, 0, jnp.minimum(first_cut, nk))
    cb = jnp.where(other_tie, nk - 1, last_cut)
    ties_at_ca = jnp.where(other_tie, 0.0, zb_ref[jnp.minimum(ca, nk - 1)][0:1, :])

    r_i = lax.broadcasted_iota(jnp.int32, (ck, ck), 0)
    c_i = lax.broadcasted_iota(jnp.int32, (ck, ck), 1)
    tri_ref[...] = jnp.where(c_i <= r_i, 1.0, 0.0).astype(BF16)

    def keep_all(c, dmin_a):
        r0 = pl.multiple_of(c * ck, ck)
        x = sc_ref[pl.ds(r0, ck), :]
        pen = jnp.where(x >= thr, alibi_dist(r0), inf)
        sc_ref[pl.ds(r0, ck), :] = pen
        return jnp.minimum(dmin_a, fold(pen, jnp.min))

    def ranked(c, carry):
        ties_before, dmin_a = carry
        r0 = pl.multiple_of(c * ck, ck)
        x = sc_ref[pl.ds(r0, ck), :]
        dist = alibi_dist(r0)
        eq = x == thr
        rank = _dot(tri_ref[...], jnp.where(eq, 1.0, 0.0).astype(BF16)) + ties_before
        tie = jnp.where(eq, jnp.where(rank <= take, dist, inf), inf)
        pen = jnp.where(x > thr, dist, tie)
        sc_ref[pl.ds(r0, ck), :] = pen
        return rank[ck - 1:ck, :], jnp.minimum(dmin_a, fold(pen, jnp.min))

    def keep_untied(c, dmin_a):
        r0 = pl.multiple_of(c * ck, ck)
        x = sc_ref[pl.ds(r0, ck), :]
        dist = alibi_dist(r0)
        tie = jnp.where(x == thr, jnp.where(tied, inf, dist), inf)
        pen = jnp.where(x > thr, dist, tie)
        sc_ref[pl.ds(r0, ck), :] = pen
        return jnp.minimum(dmin_a, fold(pen, jnp.min))

    dmin_a = lax.fori_loop(0, ca, keep_all, facc(inf))
    _, dmin_a = lax.fori_loop(ca, cb + 1, ranked, (ties_at_ca, dmin_a))
    dmin_a = lax.fori_loop(jnp.maximum(cb + 1, ca), nk, keep_untied, dmin_a)

    dmin = col(dmin_a, jnp.min)

    row = lax.broadcasted_iota(jnp.int32, (128, tq), 0)
    for p in range(N_HEADS_A // 2):
        qpair = qat_ref[p * 128:(p + 1) * 128, :]
        rhsa_ref[p] = jnp.concatenate(
            [jnp.where(row < HEAD_DIM_A, qpair, jnp.zeros_like(qpair)),
             jnp.where(row >= HEAD_DIM_A, qpair, jnp.zeros_like(qpair))], axis=1)

    m_ref[...] = jnp.full((N_HEADS_A, tq), SOFTMAX_M_INIT, F32)
    d_ref[...] = jnp.zeros((N_HEADS_A, tq), F32)
    ot_ref[...] = jnp.zeros((WIDTH_A, tq), F32)

    def k_chunk(c):
        r0 = pl.multiple_of(c * ck, ck)
        return [ka_ref[pl.ds(r0, ck), p * 128:(p + 1) * 128] for p in range(N_HEADS_A // 2)]

    def qk_pair(buf, p, kc):
        buf[:, p * 2 * tq:(p + 1) * 2 * tq] = _dot(kc, rhsa_ref[p])

    def qk_into(buf, kcs):
        for p in range(N_HEADS_A // 2):
            qk_pair(buf, p, kcs[p])

    def step(cur, nxt, c, c_next):
        r0 = pl.multiple_of(c * ck, ck)
        pen = sc_ref[pl.ds(r0, ck), :] - dmin
        vts = [vat_ref[h * HEAD_DIM_A:(h + 1) * HEAD_DIM_A, pl.ds(r0, ck)]
               for h in range(N_HEADS_A)]
        kcs = k_chunk(c_next)
        qk_pair(nxt, 0, kcs[0])
        for p in range(N_HEADS_A // 2):
            m_seen = soft_pv(cur, pen, vts, (2 * p, 2 * p + 1))
            if p + 1 < N_HEADS_A // 2:
                tie = jnp.where(m_seen != m_seen, 1.0, 0.0).astype(BF16)
                qk_pair(nxt, p + 1, kcs[p + 1] + tie)

    def soft_pv(buf, pen, vts, heads):
        m_seen = None
        for h in heads:
            rows = slice(h * HEAD_DIM_A, (h + 1) * HEAD_DIM_A)
            alphas, probs = [], []
            for lt in range(tq // 128):
                ln = slice(lt * 128, (lt + 1) * 128)
                m = m_ref[h:h + 1, ln]
                lg = buf[:, h * tq + lt * 128:h * tq + (lt + 1) * 128] - _alibi_slope(h) * pen[:, ln]
                m_new = jnp.maximum(m, col(fold(lg, jnp.max), jnp.max))
                alpha = jnp.exp(m - m_new)
                pr = jnp.exp(lg - m_new)
                m_ref[h:h + 1, ln] = m_new
                d_ref[h:h + 1, ln] = d_ref[h:h + 1, ln] * alpha + col(fold(pr, jnp.sum), jnp.sum)
                alphas.append(alpha)
                probs.append(pr.astype(BF16))
                m_seen = m_new if m_seen is None else m_seen
            ot_ref[rows, :] = (ot_ref[rows, :] * jnp.concatenate(alphas, axis=1)
                               + _dot(vts[h], jnp.concatenate(probs, axis=1)))
        return m_seen

    last = nk - 1
    lg0_ref = lga_ref.at[:, 0:2 * tq]

    def direct_chunk(c, carry):
        r0 = pl.multiple_of(c * ck, ck)
        pen = sc_ref[pl.ds(r0, ck), :] - dmin
        vts = [vat_ref[h * HEAD_DIM_A:(h + 1) * HEAD_DIM_A, pl.ds(r0, ck)]
               for h in range(N_HEADS_A)]
        kcs = k_chunk(c)
        k_next = ka_ref[pl.ds(pl.multiple_of(jnp.minimum(c + 1, last) * ck, ck), ck), 0:128]
        lg2 = lg0_ref[...]
        for p in range(N_HEADS_A // 2):
            lg2_next = None
            for j in range(2):
                h = 2 * p + j
                rows = slice(h * HEAD_DIM_A, (h + 1) * HEAD_DIM_A)
                probs = []
                for lt in range(tq // 128):
                    ln = slice(lt * 128, (lt + 1) * 128)
                    e = jnp.exp(lg2[:, j * tq + lt * 128:j * tq + (lt + 1) * 128]
                                - _alibi_slope(h) * pen[:, ln])
                    esum = col(fold(e, jnp.sum), jnp.sum)
                    d_ref[h:h + 1, ln] = d_ref[h:h + 1, ln] + esum
                    probs.append(e.astype(BF16))
                    if j == 0 and lt == 0:
                        if p + 1 < N_HEADS_A // 2:
                            lg2_next = _dot(kcs[p + 1], rhsa_ref[p + 1])
                        if p + 2 == N_HEADS_A // 2:
                            lg0_ref[...] = _dot(k_next, rhsa_ref[0])
                ot_ref[rows, :] = ot_ref[rows, :] + _dot(vts[h], jnp.concatenate(probs, axis=1))
            lg2 = lg2_next
        return carry

    @pl.when(direct_ref[0] > 0)
    def _():
        lg0_ref[...] = _dot(ka_ref[0:ck, 0:128], rhsa_ref[0])
        lax.fori_loop(0, nk, direct_chunk, 0)

    @pl.when(direct_ref[0] <= 0)
    def _():
        qk_into(lga_ref, k_chunk(0))

        def attn_pair(cc, carry):
            c0 = 2 * cc
            step(lga_ref, lgb_ref, c0, jnp.minimum(c0 + 1, last))

            @pl.when(c0 + 1 < nk)
            def _():
                step(lgb_ref, lga_ref, c0 + 1, jnp.minimum(c0 + 2, last))

            return carry

        lax.fori_loop(0, (nk + 1) // 2, attn_pair, 0)

    for h in range(N_HEADS_A):
        rows = slice(h * HEAD_DIM_A, (h + 1) * HEAD_DIM_A)
        ot_ref[rows, :] = ot_ref[rows, :] / d_ref[h:h + 1, :]
    out_ref[...] = ot_ref[...].T


def _dsa_attn(qat, qit, wt, ka, vat, kib, q_norm_a, k_norm_a, batch, seq):
    tq = min(TQ, seq)
    topk = min(TOPK_MAX, seq // 4)
    qk_bound = HEAD_DIM_A ** 0.5 * jnp.max(jnp.abs(q_norm_a)) * jnp.max(jnp.abs(k_norm_a)) * 1.01
    direct = (qk_bound <= SOFTMAX_DIRECT_BOUND).astype(jnp.int32).reshape(1)
    kern = functools.partial(_dsa_attn_kernel, seq=seq, topk=topk)
    q_spec = lambda rows: pl.BlockSpec((None, rows, tq), lambda b, i: (b, 0, i))
    return pl.pallas_call(
        kern,
        grid=(batch, seq // tq),
        in_specs=[
            q_spec(WIDTH_A), q_spec(WIDTH_A), q_spec(N_IDX_HEADS),
            pl.BlockSpec((None, seq, WIDTH_A), lambda b, i: (b, 0, 0)),
            pl.BlockSpec((None, WIDTH_A, seq), lambda b, i: (b, 0, 0)),
            pl.BlockSpec((None, seq, 128), lambda b, i: (b, 0, 0)),
            pl.BlockSpec(memory_space=pltpu.SMEM),
        ],
        out_specs=pl.BlockSpec((None, tq, WIDTH_A), lambda b, i: (b, i, 0)),
        out_shape=jax.ShapeDtypeStruct((batch, seq, WIDTH_A), F32),
        scratch_shapes=[
            pltpu.VMEM((seq, tq), F32),
            pltpu.VMEM((WIDTH_A, tq), F32),
            pltpu.VMEM((N_IDX_HEADS, 128, tq), BF16),
            pltpu.VMEM((N_HEADS_A // 2, 128, 2 * tq), BF16),
            pltpu.VMEM((tq, tq), BF16),
            pltpu.VMEM((N_HEADS_A, tq), F32),
            pltpu.VMEM((N_HEADS_A, tq), F32),
            pltpu.VMEM((tq, N_HEADS_A * tq), F32),
            pltpu.VMEM((tq, N_HEADS_A * tq), F32),
            pltpu.VMEM((seq // tq, 8, tq), F32),
            pltpu.VMEM((seq // tq, 8, tq), F32),
        ],
        compiler_params=pltpu.CompilerParams(
            dimension_semantics=("arbitrary", "arbitrary"), vmem_limit_bytes=VMEM_LIMIT_BYTES),
        name="dsa_attn",
    )(qat, qit, wt, ka.reshape(batch, seq, WIDTH_A), vat, kib.reshape(batch, seq, 128), direct)


_M_ZA, _M_BG, _M_CG, _M_HB, _M_ZB, _M_QM, _M_ZM, _M_G = (
    0, 512, 1024, 1536, 2048, 2560, 3072, 3584)
_M_COLS = 3584 + N_BRANCHES * D_MODEL


def _merge_kernel(x_ref, attn_ref, g_ref, wc_ref, bg_ref, cw_ref, mk_ref, mv_ref,
                  gqm_ref, wa_ref, wb_ref, wm_ref, wo_ref, out_ref, utail_ref, *, per_b):
    tm = x_ref.shape[0]
    xf = x_ref[...]
    xn = (_rms_rows(xf) * g_ref[...]).astype(BF16)

    def proj(lhs, c0, width):
        return _dot(lhs, wc_ref[:, c0:c0 + width])

    za = proj(xn, _M_ZA, WIDTH_A)
    ya = _dot((attn_ref[...] * jax.nn.silu(za)).astype(BF16), wa_ref[...])

    u = proj(xn, _M_CG, WIDTH_B) * proj(xn, _M_HB, WIDTH_B)
    @pl.when((pl.program_id(0) % per_b) == 0)
    def _():
        utail_ref[...] = jnp.zeros((8, WIDTH_B), F32)

    uh = utail_ref[...]
    utail_ref[...] = u[tm - 8:, :]
    rows = lax.broadcasted_iota(jnp.int32, (tm, WIDTH_B), 0)
    u1 = jnp.where(rows == 0, uh[7:8, :], pltpu.roll(u, 1, 0))
    u2 = jnp.where(rows == 0, uh[6:7, :], jnp.where(rows == 1, uh[7:8, :], pltpu.roll(u, 2, 0)))
    conv = cw_ref[0:1, :] * u2 + cw_ref[1:2, :] * u1 + cw_ref[2:3, :] * u
    bgate = proj(xn, _M_BG, WIDTH_B)
    zb = proj(xn, _M_ZB, WIDTH_B)
    yb = _dot(((bgate * conv) * jax.nn.silu(zb)).astype(BF16), wb_ref[...])

    qm = proj(xn, _M_QM, WIDTH_M)
    zm = proj(xn, _M_ZM, WIDTH_M)
    heads = []
    for h in range(N_HEADS_M):
        sl = slice(h * HEAD_DIM_M, (h + 1) * HEAD_DIM_M)
        qh = (_rms_rows(qm[:, sl]) * gqm_ref[...]).astype(BF16)
        lg = _dot_nt(qh, mk_ref[:, sl]) * (HEAD_DIM_M ** -0.5)
        e = jnp.exp(lg - jnp.max(lg, axis=-1, keepdims=True))
        p = e / jnp.sum(e, axis=-1, keepdims=True)
        heads.append(_dot(p.astype(BF16), mv_ref[:, sl]))
    attn_m = jnp.concatenate(heads, axis=1)
    ym = _dot((attn_m * jax.nn.silu(zm)).astype(BF16), wm_ref[...])

    def gate(j):
        gj = proj(xn, _M_G + j * D_MODEL, D_MODEL) + bg_ref[:, j * D_MODEL:(j + 1) * D_MODEL]
        return jax.nn.sigmoid(gj)

    merged = gate(0) * ya + gate(1) * yb + gate(2) * ym
    out_ref[...] = xf + _dot(merged.astype(BF16), wo_ref[...])


def _merge(x2, attn2, mk, mv, norm_g, w_in, b_gate, conv_w, q_norm_m,
           w_out_a, w_out_b, w_out_m, w_o, batch, seq):
    n = x2.shape[0]
    tm = min(TM_C, seq)
    per_b = seq // tm
    m_len = mk.shape[1]
    wc = jnp.concatenate([w_in[:, _C_ZA:_C_ZA + 512], w_in[:, _C_REST:]], axis=1).astype(BF16)
    const = lambda shape: pl.BlockSpec(shape, lambda i: (0,) * len(shape),
                                       pipeline_mode=pl.Buffered(1))
    kern = functools.partial(_merge_kernel, per_b=per_b)
    return pl.pallas_call(
        kern,
        grid=(n // tm,),
        in_specs=[
            pl.BlockSpec((tm, D_MODEL), lambda i: (i, 0)),
            pl.BlockSpec((tm, WIDTH_A), lambda i: (i, 0)),
            const((1, D_MODEL)), const((D_MODEL, _M_COLS)), const((1, N_BRANCHES * D_MODEL)),
            const((CONV_WIDTH, WIDTH_B)),
            pl.BlockSpec((None, m_len, WIDTH_M), lambda i: (i // per_b, 0, 0)),
            pl.BlockSpec((None, m_len, WIDTH_M), lambda i: (i // per_b, 0, 0)),
            const((1, HEAD_DIM_M)),
            const((WIDTH_A, D_MODEL)), const((WIDTH_B, D_MODEL)), const((WIDTH_M, D_MODEL)),
            const((D_MODEL, D_MODEL)),
        ],
        out_specs=pl.BlockSpec((tm, D_MODEL), lambda i: (i, 0)),
        out_shape=jax.ShapeDtypeStruct((n, D_MODEL), F32),
        scratch_shapes=[pltpu.VMEM((8, WIDTH_B), F32)],
        compiler_params=pltpu.CompilerParams(
            dimension_semantics=("arbitrary",), vmem_limit_bytes=VMEM_LIMIT_BYTES),
        name="merge",
    )(x2, attn2, norm_g.reshape(1, D_MODEL), wc, b_gate.reshape(1, -1), conv_w,
      mk, mv, q_norm_m.reshape(1, HEAD_DIM_M),
      w_out_a.astype(BF16), w_out_b.astype(BF16), w_out_m.astype(BF16), w_o.astype(BF16))


def _layer(h, mem, norm_g, mem_norm_g, w_in, b_gate, w_mem_kv, q_norm_a, k_norm_a,
           q_norm_m, k_norm_m, conv_w, w_out_a, w_out_b, w_out_m, w_o):
    batch, seq, _ = h.shape
    x2 = h.reshape(batch * seq, D_MODEL)
    qat, ka, vat, qit, kib, wt = _proj_a(x2, norm_g, w_in, q_norm_a, k_norm_a, batch, seq)
    mk, mv = _mem_kv(mem, mem_norm_g, w_mem_kv, k_norm_m)
    attn = _dsa_attn(qat, qit, wt, ka, vat, kib, q_norm_a, k_norm_a, batch, seq)
    out = _merge(x2, attn.reshape(batch * seq, WIDTH_A), mk, mv, norm_g, w_in, b_gate, conv_w,
                 q_norm_m, w_out_a, w_out_b, w_out_m, w_o, batch, seq)
    return out.reshape(batch, seq, D_MODEL)


def kernel(x, mem, norm_g, mem_norm_g, w_in, b_gate, w_mem_kv, q_norm_a, k_norm_a,
           q_norm_m, k_norm_m, conv_w, w_out_a, w_out_b, w_out_m, w_o):
    h = x
    for l in range(norm_g.shape[0]):
        h = _layer(h, mem, norm_g[l], mem_norm_g[l], w_in[l], b_gate[l], w_mem_kv[l],
                   q_norm_a[l], k_norm_a[l], q_norm_m[l], k_norm_m[l], conv_w[l],
                   w_out_a[l], w_out_b[l], w_out_m[l], w_o[l])
    return h
```

```python
import functools

import jax
import jax.numpy as jnp
import numpy as np
from jax import lax
from jax.experimental import pallas as pl
from jax.experimental.pallas import tpu as pltpu

F32 = jnp.float32
BF16 = jnp.bfloat16

D_MODEL = 1024
N_HEADS_A = 8
HEAD_DIM_A = 64
WIDTH_A = 512
N_IDX_HEADS = 8
IDX_DIM = 64
TOPK_MAX = 256
WIDTH_B = 512
CONV_WIDTH = 3
N_HEADS_M = 4
HEAD_DIM_M = 128
WIDTH_M = 512
N_BRANCHES = 3
RMS_EPS = 1e-6

_C_QA, _C_KA, _C_VA, _C_ZA = 0, 512, 1024, 1536
_C_QI, _C_KI, _C_WI = 2048, 2560, 2624
_C_REST = 2632
_D_IN = 8776

VMEM_LIMIT_BYTES = 56 * 1024 * 1024

TM_A = 512
TM_C = 512
TQ = 256
FOLD_ROWS = 32
BISECT_STEPS = 19
SOFTMAX_DIRECT_BOUND = 60.0
SOFTMAX_M_INIT = -1e30


def _rms_rows(xf, eps=RMS_EPS):
    return xf * lax.rsqrt(jnp.mean(xf * xf, axis=-1, keepdims=True) + eps)


def _dot(a, b):
    return jnp.dot(a, b, preferred_element_type=F32)


def _dot_nt(a, b):
    return lax.dot_general(a, b, (((1,), (1,)), ((), ())), preferred_element_type=F32)


def _proj_a_kernel(x_ref, g_ref, w1_ref, w2t_ref, gsum_ref, gk_ref, gq_ref,
                   qat_ref, ka_ref, vat_ref, qit_ref, kib_ref, wt_ref):
    tm = x_ref.shape[0]
    xn = (_rms_rows(x_ref[...]) * g_ref[...]).astype(BF16)

    y1 = _dot(xn, w1_ref[...])
    ka_raw = y1[:, :WIDTH_A]
    sq = ka_raw * ka_raw
    sq_hi = sq.astype(BF16)
    sq_lo = (sq - sq_hi.astype(F32)).astype(BF16)
    ss = _dot(sq_hi, gsum_ref[...]) + _dot(sq_lo, gsum_ref[...])
    ka = ka_raw * lax.rsqrt(ss * (1.0 / HEAD_DIM_A) + RMS_EPS) * gk_ref[...]
    ka_ref[...] = ka.astype(BF16)
    kiwi = y1[:, WIDTH_A:]
    lane = lax.broadcasted_iota(jnp.int32, kiwi.shape, 1)
    kib_ref[...] = jnp.where(lane < IDX_DIM, kiwi, 0.0).astype(BF16)
    wt_ref[...] = kiwi.T[IDX_DIM:IDX_DIM + N_IDX_HEADS, :]

    yt = _dot_nt(w2t_ref[...], xn)
    gq = jnp.concatenate([gq_ref[...]] * (tm // 128), axis=1)
    for h in range(N_HEADS_A):
        qh = yt[h * HEAD_DIM_A:(h + 1) * HEAD_DIM_A, :]
        ms = jnp.mean(qh * qh, axis=0, keepdims=True)
        qn = qh * lax.rsqrt(ms + RMS_EPS) * gq
        qat_ref[h * HEAD_DIM_A:(h + 1) * HEAD_DIM_A, :] = (qn * (HEAD_DIM_A ** -0.5)).astype(BF16)
    vat_ref[...] = yt[WIDTH_A:2 * WIDTH_A, :].astype(BF16)
    qit_ref[...] = yt[2 * WIDTH_A:3 * WIDTH_A, :].astype(BF16)


def _proj_a(x2, norm_g, w_in, q_norm_a, k_norm_a, batch, seq):
    n = x2.shape[0]
    tm = min(TM_A, seq)
    per_b = seq // tm
    w1 = jnp.concatenate(
        [w_in[:, _C_KA:_C_KA + 512], w_in[:, _C_KI:_C_KI + 72],
         jnp.zeros((D_MODEL, 56), F32)], axis=1).astype(BF16)
    w2t = jnp.concatenate(
        [w_in[:, _C_QA:_C_QA + 512], w_in[:, _C_VA:_C_VA + 512],
         w_in[:, _C_QI:_C_QI + 512]], axis=1).T.astype(BF16)
    hid = np.arange(WIDTH_A) // HEAD_DIM_A
    gsum = jnp.asarray(hid[:, None] == hid[None, :], BF16)
    gk = jnp.tile(k_norm_a.reshape(1, HEAD_DIM_A), (1, N_HEADS_A))
    gq = jnp.broadcast_to(q_norm_a.reshape(HEAD_DIM_A, 1), (HEAD_DIM_A, 128))

    full = lambda shape: pl.BlockSpec(shape, lambda i: (0,) * len(shape))
    t_spec = lambda rows: pl.BlockSpec((None, rows, tm), lambda i: (i // per_b, 0, i % per_b))
    return pl.pallas_call(
        _proj_a_kernel,
        grid=(n // tm,),
        in_specs=[
            pl.BlockSpec((tm, D_MODEL), lambda i: (i, 0)),
            full((1, D_MODEL)), full((D_MODEL, 640)), full((1536, D_MODEL)),
            full((WIDTH_A, WIDTH_A)), full((1, WIDTH_A)), full((HEAD_DIM_A, 128)),
        ],
        out_specs=[
            t_spec(WIDTH_A),
            pl.BlockSpec((tm, WIDTH_A), lambda i: (i, 0)),
            t_spec(WIDTH_A), t_spec(WIDTH_A),
            pl.BlockSpec((tm, 128), lambda i: (i, 0)),
            t_spec(N_IDX_HEADS),
        ],
        out_shape=[
            jax.ShapeDtypeStruct((batch, WIDTH_A, seq), BF16),
            jax.ShapeDtypeStruct((n, WIDTH_A), BF16),
            jax.ShapeDtypeStruct((batch, WIDTH_A, seq), BF16),
            jax.ShapeDtypeStruct((batch, WIDTH_A, seq), BF16),
            jax.ShapeDtypeStruct((n, 128), BF16),
            jax.ShapeDtypeStruct((batch, N_IDX_HEADS, seq), F32),
        ],
        compiler_params=pltpu.CompilerParams(
            dimension_semantics=("arbitrary",), vmem_limit_bytes=VMEM_LIMIT_BYTES),
        name="proj_a",
    )(x2, norm_g.reshape(1, D_MODEL), w1, w2t, gsum, gk, gq)


def _mem_kv_kernel(mem_ref, g_ref, w_ref, gk_ref, mk_ref, mv_ref):
    mn = (_rms_rows(mem_ref[...]) * g_ref[...]).astype(BF16)
    y = _dot(mn, w_ref[...])
    for h in range(N_HEADS_M):
        kh = y[:, h * HEAD_DIM_M:(h + 1) * HEAD_DIM_M]
        mk_ref[:, h * HEAD_DIM_M:(h + 1) * HEAD_DIM_M] = (_rms_rows(kh) * gk_ref[...]).astype(BF16)
    mv_ref[...] = y[:, WIDTH_M:].astype(BF16)


def _mem_kv(mem, mem_norm_g, w_mem_kv, k_norm_m):
    batch, m_len, _ = mem.shape
    full = lambda shape: pl.BlockSpec(shape, lambda b: (0,) * len(shape))
    return pl.pallas_call(
        _mem_kv_kernel,
        grid=(batch,),
        in_specs=[
            pl.BlockSpec((None, m_len, D_MODEL), lambda b: (b, 0, 0)),
            full((1, D_MODEL)), full((D_MODEL, 2 * WIDTH_M)), full((1, HEAD_DIM_M)),
        ],
        out_specs=[pl.BlockSpec((None, m_len, WIDTH_M), lambda b: (b, 0, 0))] * 2,
        out_shape=[jax.ShapeDtypeStruct((batch, m_len, WIDTH_M), BF16)] * 2,
        compiler_params=pltpu.CompilerParams(
            dimension_semantics=("arbitrary",), vmem_limit_bytes=VMEM_LIMIT_BYTES),
        name="mem_kv",
    )(mem, mem_norm_g.reshape(1, D_MODEL), w_mem_kv.astype(BF16), k_norm_m.reshape(1, HEAD_DIM_M))


def _alibi_slope(h):
    return float(2.0 ** (-8.0 * (h + 1) / N_HEADS_A))


def _dsa_attn_kernel(qat_ref, qit_ref, wt_ref, ka_ref, vat_ref, kib_ref, direct_ref, out_ref,
                     sc_ref, ot_ref, rhsi_ref, rhsa_ref, tri_ref, m_ref, d_ref, lga_ref, lgb_ref,
                     zc_ref, zb_ref, *, seq, topk):
    tq = qat_ref.shape[1]
    ck = tq
    i = pl.program_id(1)
    t0 = i * tq
    nk = i + 1
    inf = jnp.float32(jnp.inf)
    kf = jnp.float32(topk)
    t_row = t0 + lax.broadcasted_iota(jnp.int32, (1, tq), 1)

    def fold(w, op):
        return op(w.reshape(ck // FOLD_ROWS, FOLD_ROWS, w.shape[1]), axis=0)

    def over_chunks(body, init):
        def step(c, carry):
            r0 = pl.multiple_of(c * ck, ck)
            return body(sc_ref[pl.ds(r0, ck), :], r0, carry)
        return lax.fori_loop(0, nk, step, init)

    def col(acc, op):
        return op(acc, axis=0, keepdims=True)

    facc = lambda v: jnp.full((FOLD_ROWS, tq), v, F32)

    zero_rows = jnp.zeros((128 - IDX_DIM, tq), BF16)
    for h in range(N_IDX_HEADS):
        rhsi_ref[h] = jnp.concatenate(
            [qit_ref[h * IDX_DIM:(h + 1) * IDX_DIM, :], zero_rows], axis=0)
    idx_scale = (IDX_DIM ** -0.5) * (N_IDX_HEADS ** -0.5)

    n_ahead = N_IDX_HEADS // 2

    def dots_ahead(kc):
        for h in range(n_ahead):
            lgb_ref[:, h * tq:(h + 1) * tq] = _dot(kc, rhsi_ref[h])

    dots_ahead(kib_ref[0:ck, :])

    def score_chunk(c, carry):
        lo_a, hi_a, pos_a, nonneg_a = carry
        r0 = pl.multiple_of(c * ck, ck)
        kc = kib_ref[pl.ds(r0, ck), :]
        k_next = kib_ref[pl.ds(pl.multiple_of(jnp.minimum(c + 1, nk - 1) * ck, ck), ck), :]
        acc = jnp.zeros((ck, tq), F32)
        for h in range(n_ahead):
            acc = acc + jnp.maximum(lgb_ref[:, h * tq:(h + 1) * tq], 0.0) * wt_ref[h:h + 1, :]
        for h in range(n_ahead, N_IDX_HEADS):
            d = _dot(kc, rhsi_ref[h])
            acc = acc + jnp.maximum(d, 0.0) * wt_ref[h:h + 1, :]
        dots_ahead(k_next)
        acc = acc * idx_scale
        causal = r0 + lax.broadcasted_iota(jnp.int32, (ck, tq), 0) <= t_row
        masked = jnp.where(causal, acc, -inf)
        sc_ref[pl.ds(r0, ck), :] = masked
        pos_c = fold(jnp.where(masked > 0.0, 1.0, 0.0), jnp.sum)
        nonneg_c = fold(jnp.where(masked >= 0.0, 1.0, 0.0), jnp.sum)
        zc_ref[c] = jnp.broadcast_to(col(nonneg_c - pos_c, jnp.sum), (8, tq))
        return (jnp.minimum(lo_a, fold(jnp.where(causal, acc, inf), jnp.min)),
                jnp.maximum(hi_a, fold(masked, jnp.max)),
                pos_a + pos_c, nonneg_a + nonneg_c)

    lo_a, hi_a, pos_a, nonneg_a = lax.fori_loop(
        0, nk, score_chunk, (facc(inf), facc(-inf), facc(0.0), facc(0.0)))

    def count_ge(v):
        acc = over_chunks(lambda x, r0, a: a + fold(jnp.where(x >= v, 1.0, 0.0), jnp.sum), facc(0.0))
        return col(acc, jnp.sum)

    lo_min = col(lo_a, jnp.min)
    hi_max = col(hi_a, jnp.max)
    n_causal = (t_row + 1).astype(F32)
    short = n_causal < kf

    def bis_step(lo, hi, cl):
        mid = lo * 0.5 + hi * 0.5
        c = count_ge(mid)
        ok = c >= kf
        return jnp.where(ok, mid, lo), jnp.where(ok, hi, mid), jnp.where(ok, c, cl)

    n_pos = col(pos_a, jnp.sum)
    n_nonneg = col(nonneg_a, jnp.sum)
    above_zero = n_nonneg >= kf
    lo0 = jnp.where(above_zero, 0.0, lo_min)
    hi0 = jnp.where(above_zero, hi_max, 0.0)
    cl0 = jnp.where(short, kf, jnp.where(above_zero, n_nonneg, n_causal))
    lo, _, cl = lax.fori_loop(0, BISECT_STEPS, lambda _, s: bis_step(*s), (lo0, hi0, cl0))

    def walk_cond(st):
        return jnp.max(st[3]) > 0.0

    def walk_body(st):
        lo, thr, take, todo = st
        lo_e = col(over_chunks(
            lambda x, r0, a: jnp.minimum(a, fold(jnp.where(x >= lo, x, inf), jnp.min)), facc(inf)), jnp.min)

        def above(x, r0, carry):
            cgt_a, cge_a, nxt_a = carry
            gt = x > lo_e
            return (cgt_a + fold(jnp.where(gt, 1.0, 0.0), jnp.sum),
                    cge_a + fold(jnp.where(x >= lo_e, 1.0, 0.0), jnp.sum),
                    jnp.minimum(nxt_a, fold(jnp.where(gt, x, inf), jnp.min)))

        cgt_a, cge_a, nxt_a = over_chunks(above, (facc(0.0), facc(0.0), facc(inf)))
        cgt, cge, nxt = col(cgt_a, jnp.sum), col(cge_a, jnp.sum), col(nxt_a, jnp.min)
        active = todo > 0.0
        fin = active & (cgt < kf)
        thr = jnp.where(fin, lo_e, thr)
        take = jnp.where(fin & (cge > kf), kf - cgt, take)
        lo = jnp.where(active & (cgt >= kf), nxt, lo)
        todo = jnp.where(fin, 0.0, todo)
        return lo, thr, take, todo

    zero_tie = jnp.logical_not(short) & (n_pos < kf) & above_zero
    thr0 = jnp.where(zero_tie, 0.0, jnp.where(short, lo_min, lo))
    take0 = jnp.where(zero_tie, kf - n_pos, float(seq))
    todo0 = jnp.where(zero_tie | (cl == kf), 0.0, 1.0)
    _, thr, take, _ = lax.while_loop(walk_cond, walk_body, (lo, thr0, take0, todo0))

    def alibi_dist(r0):
        s_idx = r0 + lax.broadcasted_iota(jnp.int32, (ck, tq), 0)
        return (t_row - s_idx).astype(F32)

    tied = take < float(seq)
    far = jnp.float32(seq)

    def locate(c, carry):
        zeros_before, cut_chunk = carry
        zb_ref[c] = jnp.broadcast_to(zeros_before, (8, tq))
        zeros_upto = zeros_before + zc_ref[c][0:1, :]
        hit = zero_tie & (cut_chunk >= far) & (zeros_upto >= take)
        return zeros_upto, jnp.where(hit, c.astype(F32), cut_chunk)

    _, cut_chunk = lax.fori_loop(0, nk, locate, (jnp.zeros((1, tq), F32), jnp.full((1, tq), far)))
    other_tie = jnp.max(jnp.where(tied & jnp.logical_not(zero_tie), 1.0, 0.0)) > 0.0
    first_cut = jnp.min(cut_chunk).astype(jnp.int32)
    last_cut = jnp.max(jnp.where(cut_chunk < far, cut_chunk, -1.0)).astype(jnp.int32)
    ca = jnp.where(other_tie, 0, jnp.minimum(first_cut, nk))
    cb = jnp.where(other_tie, nk - 1, last_cut)
    ties_at_ca = jnp.where(other_tie, 0.0, zb_ref[jnp.minimum(ca, nk - 1)][0:1, :])

    @pl.when((pl.program_id(0) == 0) & (i == 0))
    def _():
        r_i = lax.broadcasted_iota(jnp.int32, (ck, ck), 0)
        c_i = lax.broadcasted_iota(jnp.int32, (ck, ck), 1)
        tri_ref[...] = jnp.where(c_i <= r_i, 1.0, 0.0).astype(BF16)

    def keep_all(c, dmin_a):
        r0 = pl.multiple_of(c * ck, ck)
        x = sc_ref[pl.ds(r0, ck), :]
        pen = jnp.where(x >= thr, alibi_dist(r0), inf)
        sc_ref[pl.ds(r0, ck), :] = pen
        return jnp.minimum(dmin_a, fold(pen, jnp.min))

    def ranked(c, carry):
        ties_before, dmin_a = carry
        r0 = pl.multiple_of(c * ck, ck)
        x = sc_ref[pl.ds(r0, ck), :]
        dist = alibi_dist(r0)
        eq = x == thr
        rank = _dot(tri_ref[...], jnp.where(eq, 1.0, 0.0).astype(BF16)) + ties_before
        tie = jnp.where(eq, jnp.where(rank <= take, dist, inf), inf)
        pen = jnp.where(x > thr, dist, tie)
        sc_ref[pl.ds(r0, ck), :] = pen
        return rank[ck - 1:ck, :], jnp.minimum(dmin_a, fold(pen, jnp.min))

    def keep_untied(c, dmin_a):
        r0 = pl.multiple_of(c * ck, ck)
        x = sc_ref[pl.ds(r0, ck), :]
        dist = alibi_dist(r0)
        tie = jnp.where(x == thr, jnp.where(tied, inf, dist), inf)
        pen = jnp.where(x > thr, dist, tie)
        sc_ref[pl.ds(r0, ck), :] = pen
        return jnp.minimum(dmin_a, fold(pen, jnp.min))

    dmin_a = lax.fori_loop(0, ca, keep_all, facc(inf))
    _, dmin_a = lax.fori_loop(ca, cb + 1, ranked, (ties_at_ca, dmin_a))
    dmin_a = lax.fori_loop(jnp.maximum(cb + 1, ca), nk, keep_untied, dmin_a)

    dmin = col(dmin_a, jnp.min)

    row = lax.broadcasted_iota(jnp.int32, (128, tq), 0)
    for p in range(N_HEADS_A // 2):
        qpair = qat_ref[p * 128:(p + 1) * 128, :]
        rhsa_ref[p] = jnp.concatenate(
            [jnp.where(row < HEAD_DIM_A, qpair, jnp.zeros_like(qpair)),
             jnp.where(row >= HEAD_DIM_A, qpair, jnp.zeros_like(qpair))], axis=1)

    m_ref[...] = jnp.full((N_HEADS_A, tq), SOFTMAX_M_INIT, F32)
    d_ref[...] = jnp.zeros((N_HEADS_A, tq), F32)
    ot_ref[...] = jnp.zeros((WIDTH_A, tq), F32)

    def k_chunk(c):
        r0 = pl.multiple_of(c * ck, ck)
        return [ka_ref[pl.ds(r0, ck), p * 128:(p + 1) * 128] for p in range(N_HEADS_A // 2)]

    def qk_pair(buf, p, kc):
        buf[:, p * 2 * tq:(p + 1) * 2 * tq] = _dot(kc, rhsa_ref[p])

    def qk_into(buf, kcs):
        for p in range(N_HEADS_A // 2):
            qk_pair(buf, p, kcs[p])

    def step(cur, nxt, c, c_next):
        r0 = pl.multiple_of(c * ck, ck)
        pen = sc_ref[pl.ds(r0, ck), :] - dmin
        vts = [vat_ref[h * HEAD_DIM_A:(h + 1) * HEAD_DIM_A, pl.ds(r0, ck)]
               for h in range(N_HEADS_A)]
        kcs = k_chunk(c_next)
        qk_pair(nxt, 0, kcs[0])
        for p in range(N_HEADS_A // 2):
            m_seen = soft_pv(cur, pen, vts, (2 * p, 2 * p + 1))
            if p + 1 < N_HEADS_A // 2:
                tie = jnp.where(m_seen != m_seen, 1.0, 0.0).astype(BF16)
                qk_pair(nxt, p + 1, kcs[p + 1] + tie)

    def soft_pv(buf, pen, vts, heads):
        m_seen = None
        for h in heads:
            rows = slice(h * HEAD_DIM_A, (h + 1) * HEAD_DIM_A)
            alphas, probs = [], []
            for lt in range(tq // 128):
                ln = slice(lt * 128, (lt + 1) * 128)
                m = m_ref[h:h + 1, ln]
                lg = buf[:, h * tq + lt * 128:h * tq + (lt + 1) * 128] - _alibi_slope(h) * pen[:, ln]
                m_new = jnp.maximum(m, col(fold(lg, jnp.max), jnp.max))
                alpha = jnp.exp(m - m_new)
                pr = jnp.exp(lg - m_new)
                m_ref[h:h + 1, ln] = m_new
                d_ref[h:h + 1, ln] = d_ref[h:h + 1, ln] * alpha + col(fold(pr, jnp.sum), jnp.sum)
                alphas.append(alpha)
                probs.append(pr.astype(BF16))
                m_seen = m_new if m_seen is None else m_seen
            ot_ref[rows, :] = (ot_ref[rows, :] * jnp.concatenate(alphas, axis=1)
                               + _dot(vts[h], jnp.concatenate(probs, axis=1)))
        return m_seen

    last = nk - 1
    lg0_ref = lga_ref.at[:, 0:2 * tq]

    def direct_chunk(c, carry):
        r0 = pl.multiple_of(c * ck, ck)
        pen = sc_ref[pl.ds(r0, ck), :] - dmin
        vts = [vat_ref[h * HEAD_DIM_A:(h + 1) * HEAD_DIM_A, pl.ds(r0, ck)]
               for h in range(N_HEADS_A)]
        kcs = k_chunk(c)
        k_next = ka_ref[pl.ds(pl.multiple_of(jnp.minimum(c + 1, last) * ck, ck), ck), 0:128]
        lg2 = lg0_ref[...]
        for p in range(N_HEADS_A // 2):
            lg2_next = None
            for j in range(2):
                h = 2 * p + j
                rows = slice(h * HEAD_DIM_A, (h + 1) * HEAD_DIM_A)
                probs = []
                for lt in range(tq // 128):
                    ln = slice(lt * 128, (lt + 1) * 128)
                    e = jnp.exp(lg2[:, j * tq + lt * 128:j * tq + (lt + 1) * 128]
                                - _alibi_slope(h) * pen[:, ln])
                    esum = col(fold(e, jnp.sum), jnp.sum)
                    d_ref[h:h + 1, ln] = d_ref[h:h + 1, ln] + esum
                    probs.append(e.astype(BF16))
                    if j == 0 and lt == 0:
                        if p + 1 < N_HEADS_A // 2:
                            lg2_next = _dot(kcs[p + 1], rhsa_ref[p + 1])
                        if p + 2 == N_HEADS_A // 2:
                            lg0_ref[...] = _dot(k_next, rhsa_ref[0])
                ot_ref[rows, :] = ot_ref[rows, :] + _dot(vts[h], jnp.concatenate(probs, axis=1))
            lg2 = lg2_next
        return carry

    @pl.when(direct_ref[0] > 0)
    def _():
        lg0_ref[...] = _dot(ka_ref[0:ck, 0:128], rhsa_ref[0])
        lax.fori_loop(0, nk, direct_chunk, 0)

    @pl.when(direct_ref[0] <= 0)
    def _():
        qk_into(lga_ref, k_chunk(0))

        def attn_pair(cc, carry):
            c0 = 2 * cc
            step(lga_ref, lgb_ref, c0, jnp.minimum(c0 + 1, last))

            @pl.when(c0 + 1 < nk)
            def _():
                step(lgb_ref, lga_ref, c0 + 1, jnp.minimum(c0 + 2, last))

            return carry

        lax.fori_loop(0, (nk + 1) // 2, attn_pair, 0)

    for h in range(N_HEADS_A):
        rows = slice(h * HEAD_DIM_A, (h + 1) * HEAD_DIM_A)
        ot_ref[rows, :] = ot_ref[rows, :] / d_ref[h:h + 1, :]
    out_ref[...] = ot_ref[...].T


def _dsa_attn(qat, qit, wt, ka, vat, kib, q_norm_a, k_norm_a, batch, seq):
    tq = min(TQ, seq)
    topk = min(TOPK_MAX, seq // 4)
    qk_bound = HEAD_DIM_A ** 0.5 * jnp.max(jnp.abs(q_norm_a)) * jnp.max(jnp.abs(k_norm_a)) * 1.01
    direct = (qk_bound <= SOFTMAX_DIRECT_BOUND).astype(jnp.int32).reshape(1)
    kern = functools.partial(_dsa_attn_kernel, seq=seq, topk=topk)
    q_spec = lambda rows: pl.BlockSpec((None, rows, tq), lambda b, i: (b, 0, i))
    return pl.pallas_call(
        kern,
        grid=(batch, seq // tq),
        in_specs=[
            q_spec(WIDTH_A), q_spec(WIDTH_A), q_spec(N_IDX_HEADS),
            pl.BlockSpec((None, seq, WIDTH_A), lambda b, i: (b, 0, 0)),
            pl.BlockSpec((None, WIDTH_A, seq), lambda b, i: (b, 0, 0)),
            pl.BlockSpec((None, seq, 128), lambda b, i: (b, 0, 0)),
            pl.BlockSpec(memory_space=pltpu.SMEM),
        ],
        out_specs=pl.BlockSpec((None, tq, WIDTH_A), lambda b, i: (b, i, 0)),
        out_shape=jax.ShapeDtypeStruct((batch, seq, WIDTH_A), F32),
        scratch_shapes=[
            pltpu.VMEM((seq, tq), F32),
            pltpu.VMEM((WIDTH_A, tq), F32),
            pltpu.VMEM((N_IDX_HEADS, 128, tq), BF16),
            pltpu.VMEM((N_HEADS_A // 2, 128, 2 * tq), BF16),
            pltpu.VMEM((tq, tq), BF16),
            pltpu.VMEM((N_HEADS_A, tq), F32),
            pltpu.VMEM((N_HEADS_A, tq), F32),
            pltpu.VMEM((tq, N_HEADS_A * tq), F32),
            pltpu.VMEM((tq, N_HEADS_A * tq), F32),
            pltpu.VMEM((seq // tq, 8, tq), F32),
            pltpu.VMEM((seq // tq, 8, tq), F32),
        ],
        compiler_params=pltpu.CompilerParams(
            dimension_semantics=("arbitrary", "arbitrary"), vmem_limit_bytes=VMEM_LIMIT_BYTES),
        name="dsa_attn",
    )(qat, qit, wt, ka.reshape(batch, seq, WIDTH_A), vat, kib.reshape(batch, seq, 128), direct)


_M_BG, _M_CG, _M_HB, _M_ZB, _M_QM, _M_ZM, _M_G = (0, 512, 1024, 1536, 2048, 2560, 3072)
_M_COLS = 3072 + N_BRANCHES * D_MODEL


def _merge_kernel(x_ref, attn_ref, g_ref, wza_ref, wc_ref, bg_ref, cw_ref, mk_ref, mv_ref,
                  gqm_ref, wa_ref, wb_ref, wm_ref, wo_ref, out_ref, utail_ref, *, per_b):
    tm = x_ref.shape[0]
    xf = x_ref[...]
    xn = (_rms_rows(xf) * g_ref[...]).astype(BF16)

    def proj(lhs, c0, width):
        return _dot(lhs, wc_ref[:, c0:c0 + width])

    za = _dot(xn, wza_ref[...])
    ya = _dot((attn_ref[...] * jax.nn.silu(za)).astype(BF16), wa_ref[...])

    u = proj(xn, _M_CG, WIDTH_B) * proj(xn, _M_HB, WIDTH_B)
    @pl.when((pl.program_id(0) % per_b) == 0)
    def _():
        utail_ref[...] = jnp.zeros((8, WIDTH_B), F32)

    uh = utail_ref[...]
    utail_ref[...] = u[tm - 8:, :]
    rows = lax.broadcasted_iota(jnp.int32, (tm, WIDTH_B), 0)
    u1 = jnp.where(rows == 0, uh[7:8, :], pltpu.roll(u, 1, 0))
    u2 = jnp.where(rows == 0, uh[6:7, :], jnp.where(rows == 1, uh[7:8, :], pltpu.roll(u, 2, 0)))
    conv = cw_ref[0:1, :] * u2 + cw_ref[1:2, :] * u1 + cw_ref[2:3, :] * u
    bgate = proj(xn, _M_BG, WIDTH_B)
    zb = proj(xn, _M_ZB, WIDTH_B)
    yb = _dot(((bgate * conv) * jax.nn.silu(zb)).astype(BF16), wb_ref[...])

    qm = proj(xn, _M_QM, WIDTH_M)
    zm = proj(xn, _M_ZM, WIDTH_M)
    heads = []
    for h in range(N_HEADS_M):
        sl = slice(h * HEAD_DIM_M, (h + 1) * HEAD_DIM_M)
        qh = (_rms_rows(qm[:, sl]) * gqm_ref[...]).astype(BF16)
        lg = _dot_nt(qh, mk_ref[:, sl]) * (HEAD_DIM_M ** -0.5)
        e = jnp.exp(lg - jnp.max(lg, axis=-1, keepdims=True))
        p = e / jnp.sum(e, axis=-1, keepdims=True)
        heads.append(_dot(p.astype(BF16), mv_ref[:, sl]))
    attn_m = jnp.concatenate(heads, axis=1)
    ym = _dot((attn_m * jax.nn.silu(zm)).astype(BF16), wm_ref[...])

    def gate(j):
        gj = proj(xn, _M_G + j * D_MODEL, D_MODEL) + bg_ref[:, j * D_MODEL:(j + 1) * D_MODEL]
        return jax.nn.sigmoid(gj)

    merged = gate(0) * ya + gate(1) * yb + gate(2) * ym
    out_ref[...] = xf + _dot(merged.astype(BF16), wo_ref[...])


def _merge(x2, attn2, mk, mv, norm_g, w_in, b_gate, conv_w, q_norm_m,
           w_out_a, w_out_b, w_out_m, w_o, batch, seq):
    n = x2.shape[0]
    tm = min(TM_C, seq)
    per_b = seq // tm
    m_len = mk.shape[1]
    wza = w_in[:, _C_ZA:_C_ZA + WIDTH_A].astype(BF16)
    wc = w_in[:, _C_REST:].astype(BF16)
    const = lambda shape: pl.BlockSpec(shape, lambda i: (0,) * len(shape),
                                       pipeline_mode=pl.Buffered(1))
    kern = functools.partial(_merge_kernel, per_b=per_b)
    return pl.pallas_call(
        kern,
        grid=(n // tm,),
        in_specs=[
            pl.BlockSpec((tm, D_MODEL), lambda i: (i, 0)),
            pl.BlockSpec((tm, WIDTH_A), lambda i: (i, 0)),
            const((1, D_MODEL)), const((D_MODEL, WIDTH_A)), const((D_MODEL, _M_COLS)),
            const((1, N_BRANCHES * D_MODEL)),
            const((CONV_WIDTH, WIDTH_B)),
            pl.BlockSpec((None, m_len, WIDTH_M), lambda i: (i // per_b, 0, 0)),
            pl.BlockSpec((None, m_len, WIDTH_M), lambda i: (i // per_b, 0, 0)),
            const((1, HEAD_DIM_M)),
            const((WIDTH_A, D_MODEL)), const((WIDTH_B, D_MODEL)), const((WIDTH_M, D_MODEL)),
            const((D_MODEL, D_MODEL)),
        ],
        out_specs=pl.BlockSpec((tm, D_MODEL), lambda i: (i, 0)),
        out_shape=jax.ShapeDtypeStruct((n, D_MODEL), F32),
        scratch_shapes=[pltpu.VMEM((8, WIDTH_B), F32)],
        compiler_params=pltpu.CompilerParams(
            dimension_semantics=("arbitrary",), vmem_limit_bytes=VMEM_LIMIT_BYTES),
        name="merge",
    )(x2, attn2, norm_g.reshape(1, D_MODEL), wza, wc, b_gate.reshape(1, -1), conv_w,
      mk, mv, q_norm_m.reshape(1, HEAD_DIM_M),
      w_out_a.astype(BF16), w_out_b.astype(BF16), w_out_m.astype(BF16), w_o.astype(BF16))


def _layer(h, mem, norm_g, mem_norm_g, w_in, b_gate, w_mem_kv, q_norm_a, k_norm_a,
           q_norm_m, k_norm_m, conv_w, w_out_a, w_out_b, w_out_m, w_o):
    batch, seq, _ = h.shape
    x2 = h.reshape(batch * seq, D_MODEL)
    qat, ka, vat, qit, kib, wt = _proj_a(x2, norm_g, w_in, q_norm_a, k_norm_a, batch, seq)
    mk, mv = _mem_kv(mem, mem_norm_g, w_mem_kv, k_norm_m)
    attn = _dsa_attn(qat, qit, wt, ka, vat, kib, q_norm_a, k_norm_a, batch, seq)
    out = _merge(x2, attn.reshape(batch * seq, WIDTH_A), mk, mv, norm_g, w_in, b_gate, conv_w,
                 q_norm_m, w_out_a, w_out_b, w_out_m, w_o, batch, seq)
    return out.reshape(batch, seq, D_MODEL)


def kernel(x, mem, norm_g, mem_norm_g, w_in, b_gate, w_mem_kv, q_norm_a, k_norm_a,
           q_norm_m, k_norm_m, conv_w, w_out_a, w_out_b, w_out_m, w_o):
    h = x
    for l in range(norm_g.shape[0]):
        h = _layer(h, mem, norm_g[l], mem_norm_g[l], w_in[l], b_gate[l], w_mem_kv[l],
                   q_norm_a[l], k_norm_a[l], q_norm_m[l], k_norm_m[l], conv_w[l],
                   w_out_a[l], w_out_b[l], w_out_m[l], w_o[l])
    return h
```

```python
import functools

import jax
import jax.numpy as jnp
import numpy as np
from jax import lax
from jax.experimental import pallas as pl
from jax.experimental.pallas import tpu as pltpu

F32 = jnp.float32
BF16 = jnp.bfloat16

D_MODEL = 1024
N_HEADS_A = 8
HEAD_DIM_A = 64
WIDTH_A = 512
N_IDX_HEADS = 8
IDX_DIM = 64
TOPK_MAX = 256
WIDTH_B = 512
CONV_WIDTH = 3
N_HEADS_M = 4
HEAD_DIM_M = 128
WIDTH_M = 512
N_BRANCHES = 3
RMS_EPS = 1e-6

_C_QA, _C_KA, _C_VA, _C_ZA = 0, 512, 1024, 1536
_C_QI, _C_KI, _C_WI = 2048, 2560, 2624
_C_REST = 2632
_D_IN = 8776

VMEM_LIMIT_BYTES = 56 * 1024 * 1024

TM_A = 512
TM_C = 512
TQ = 256
FOLD_ROWS = 32
BISECT_STEPS = 19
SOFTMAX_DIRECT_BOUND = 60.0
SOFTMAX_M_INIT = -1e30


def _rms_rows(xf, eps=RMS_EPS):
    return xf * lax.rsqrt(jnp.mean(xf * xf, axis=-1, keepdims=True) + eps)


def _dot(a, b):
    return jnp.dot(a, b, preferred_element_type=F32)


def _dot_nt(a, b):
    return lax.dot_general(a, b, (((1,), (1,)), ((), ())), preferred_element_type=F32)


def _proj_a_kernel(x_ref, g_ref, w1_ref, w2t_ref, gsum_ref, gk_ref, gq_ref,
                   qat_ref, ka_ref, vat_ref, qit_ref, kib_ref, wt_ref):
    tm = x_ref.shape[0]
    xn = (_rms_rows(x_ref[...]) * g_ref[...]).astype(BF16)

    y1 = _dot(xn, w1_ref[...])
    ka_raw = y1[:, :WIDTH_A]
    sq = ka_raw * ka_raw
    sq_hi = sq.astype(BF16)
    sq_lo = (sq - sq_hi.astype(F32)).astype(BF16)
    ss = _dot(sq_hi, gsum_ref[...]) + _dot(sq_lo, gsum_ref[...])
    ka = ka_raw * lax.rsqrt(ss * (1.0 / HEAD_DIM_A) + RMS_EPS) * gk_ref[...]
    ka_ref[...] = ka.astype(BF16)
    kiwi = y1[:, WIDTH_A:]
    lane = lax.broadcasted_iota(jnp.int32, kiwi.shape, 1)
    kib_ref[...] = jnp.where(lane < IDX_DIM, kiwi, 0.0).astype(BF16)
    wt_ref[...] = kiwi.T[IDX_DIM:IDX_DIM + N_IDX_HEADS, :]

    yt = _dot_nt(w2t_ref[...], xn)
    gq = jnp.concatenate([gq_ref[...]] * (tm // 128), axis=1)
    for h in range(N_HEADS_A):
        qh = yt[h * HEAD_DIM_A:(h + 1) * HEAD_DIM_A, :]
        ms = jnp.mean(qh * qh, axis=0, keepdims=True)
        qn = qh * lax.rsqrt(ms + RMS_EPS) * gq
        qat_ref[h * HEAD_DIM_A:(h + 1) * HEAD_DIM_A, :] = (qn * (HEAD_DIM_A ** -0.5)).astype(BF16)
    vat_ref[...] = yt[WIDTH_A:2 * WIDTH_A, :].astype(BF16)
    qit_ref[...] = yt[2 * WIDTH_A:3 * WIDTH_A, :].astype(BF16)


def _proj_a(x2, norm_g, w_in, q_norm_a, k_norm_a, batch, seq):
    n = x2.shape[0]
    tm = min(TM_A, seq)
    per_b = seq // tm
    w1 = jnp.concatenate(
        [w_in[:, _C_KA:_C_KA + 512], w_in[:, _C_KI:_C_KI + 72],
         jnp.zeros((D_MODEL, 56), F32)], axis=1).astype(BF16)
    w2t = jnp.concatenate(
        [w_in[:, _C_QA:_C_QA + 512], w_in[:, _C_VA:_C_VA + 512],
         w_in[:, _C_QI:_C_QI + 512]], axis=1).T.astype(BF16)
    hid = np.arange(WIDTH_A) // HEAD_DIM_A
    gsum = jnp.asarray(hid[:, None] == hid[None, :], BF16)
    gk = jnp.tile(k_norm_a.reshape(1, HEAD_DIM_A), (1, N_HEADS_A))
    gq = jnp.broadcast_to(q_norm_a.reshape(HEAD_DIM_A, 1), (HEAD_DIM_A, 128))

    full = lambda shape: pl.BlockSpec(shape, lambda i: (0,) * len(shape))
    t_spec = lambda rows: pl.BlockSpec((None, rows, tm), lambda i: (i // per_b, 0, i % per_b))
    return pl.pallas_call(
        _proj_a_kernel,
        grid=(n // tm,),
        in_specs=[
            pl.BlockSpec((tm, D_MODEL), lambda i: (i, 0)),
            full((1, D_MODEL)), full((D_MODEL, 640)), full((1536, D_MODEL)),
            full((WIDTH_A, WIDTH_A)), full((1, WIDTH_A)), full((HEAD_DIM_A, 128)),
        ],
        out_specs=[
            t_spec(WIDTH_A),
            pl.BlockSpec((tm, WIDTH_A), lambda i: (i, 0)),
            t_spec(WIDTH_A), t_spec(WIDTH_A),
            pl.BlockSpec((tm, 128), lambda i: (i, 0)),
            t_spec(N_IDX_HEADS),
        ],
        out_shape=[
            jax.ShapeDtypeStruct((batch, WIDTH_A, seq), BF16),
            jax.ShapeDtypeStruct((n, WIDTH_A), BF16),
            jax.ShapeDtypeStruct((batch, WIDTH_A, seq), BF16),
            jax.ShapeDtypeStruct((batch, WIDTH_A, seq), BF16),
            jax.ShapeDtypeStruct((n, 128), BF16),
            jax.ShapeDtypeStruct((batch, N_IDX_HEADS, seq), F32),
        ],
        compiler_params=pltpu.CompilerParams(
            dimension_semantics=("arbitrary",), vmem_limit_bytes=VMEM_LIMIT_BYTES),
        name="proj_a",
    )(x2, norm_g.reshape(1, D_MODEL), w1, w2t, gsum, gk, gq)


def _mem_kv_kernel(mem_ref, g_ref, w_ref, gk_ref, mk_ref, mv_ref):
    mn = (_rms_rows(mem_ref[...]) * g_ref[...]).astype(BF16)
    y = _dot(mn, w_ref[...])
    for h in range(N_HEADS_M):
        kh = y[:, h * HEAD_DIM_M:(h + 1) * HEAD_DIM_M]
        mk_ref[:, h * HEAD_DIM_M:(h + 1) * HEAD_DIM_M] = (_rms_rows(kh) * gk_ref[...]).astype(BF16)
    mv_ref[...] = y[:, WIDTH_M:].astype(BF16)


def _mem_kv(mem, mem_norm_g, w_mem_kv, k_norm_m):
    batch, m_len, _ = mem.shape
    full = lambda shape: pl.BlockSpec(shape, lambda b: (0,) * len(shape))
    return pl.pallas_call(
        _mem_kv_kernel,
        grid=(batch,),
        in_specs=[
            pl.BlockSpec((None, m_len, D_MODEL), lambda b: (b, 0, 0)),
            full((1, D_MODEL)), full((D_MODEL, 2 * WIDTH_M)), full((1, HEAD_DIM_M)),
        ],
        out_specs=[pl.BlockSpec((None, m_len, WIDTH_M), lambda b: (b, 0, 0))] * 2,
        out_shape=[jax.ShapeDtypeStruct((batch, m_len, WIDTH_M), BF16)] * 2,
        compiler_params=pltpu.CompilerParams(
            dimension_semantics=("arbitrary",), vmem_limit_bytes=VMEM_LIMIT_BYTES),
        name="mem_kv",
    )(mem, mem_norm_g.reshape(1, D_MODEL), w_mem_kv.astype(BF16), k_norm_m.reshape(1, HEAD_DIM_M))


def _alibi_slope(h):
    return float(2.0 ** (-8.0 * (h + 1) / N_HEADS_A))


def _dsa_attn_kernel(qat_ref, qit_ref, wt_ref, ka_ref, vat_ref, kib_ref, direct_ref, out_ref,
                     sc_ref, ot_ref, rhsi_ref, rhsa_ref, tri_ref, m_ref, d_ref, lga_ref, lgb_ref,
                     zc_ref, zb_ref, *, seq, topk):
    tq = qat_ref.shape[1]
    ck = tq
    i = pl.program_id(1)
    t0 = i * tq
    nk = i + 1
    inf = jnp.float32(jnp.inf)
    kf = jnp.float32(topk)
    t_row = t0 + lax.broadcasted_iota(jnp.int32, (1, tq), 1)

    def fold(w, op):
        return op(w.reshape(ck // FOLD_ROWS, FOLD_ROWS, w.shape[1]), axis=0)

    def over_chunks(body, init):
        def step(c, carry):
            r0 = pl.multiple_of(c * ck, ck)
            return body(sc_ref[pl.ds(r0, ck), :], r0, carry)
        return lax.fori_loop(0, nk, step, init)

    def col(acc, op):
        return op(acc, axis=0, keepdims=True)

    facc = lambda v: jnp.full((FOLD_ROWS, tq), v, F32)

    zero_rows = jnp.zeros((128 - IDX_DIM, tq), BF16)
    for h in range(N_IDX_HEADS):
        rhsi_ref[h] = jnp.concatenate(
            [qit_ref[h * IDX_DIM:(h + 1) * IDX_DIM, :], zero_rows], axis=0)
    idx_scale = (IDX_DIM ** -0.5) * (N_IDX_HEADS ** -0.5)

    n_ahead = N_IDX_HEADS // 2

    def dots_ahead(kc):
        for h in range(n_ahead):
            lgb_ref[:, h * tq:(h + 1) * tq] = _dot(kc, rhsi_ref[h])

    dots_ahead(kib_ref[0:ck, :])

    def score_one(c, c_next, carry):
        lo_a, hi_a, pos_a, nonneg_a = carry
        r0 = pl.multiple_of(c * ck, ck)
        kc = kib_ref[pl.ds(r0, ck), :]
        k_next = kib_ref[pl.ds(pl.multiple_of(c_next * ck, ck), ck), :]
        acc = jnp.zeros((ck, tq), F32)
        for h in range(n_ahead):
            acc = acc + jnp.maximum(lgb_ref[:, h * tq:(h + 1) * tq], 0.0) * wt_ref[h:h + 1, :]
        for h in range(n_ahead, N_IDX_HEADS):
            d = _dot(kc, rhsi_ref[h])
            acc = acc + jnp.maximum(d, 0.0) * wt_ref[h:h + 1, :]
        dots_ahead(k_next)
        acc = acc * idx_scale
        causal = r0 + lax.broadcasted_iota(jnp.int32, (ck, tq), 0) <= t_row
        masked = jnp.where(causal, acc, -inf)
        sc_ref[pl.ds(r0, ck), :] = masked
        pos_c = fold(jnp.where(masked > 0.0, 1.0, 0.0), jnp.sum)
        nonneg_c = fold(jnp.where(masked >= 0.0, 1.0, 0.0), jnp.sum)
        zc_ref[c] = jnp.broadcast_to(col(nonneg_c - pos_c, jnp.sum), (8, tq))
        return (jnp.minimum(lo_a, fold(jnp.where(causal, acc, inf), jnp.min)),
                jnp.maximum(hi_a, fold(masked, jnp.max)),
                pos_a + pos_c, nonneg_a + nonneg_c)

    odd = nk % 2
    stats0 = (facc(inf), facc(-inf), facc(0.0), facc(0.0))
    stats0 = lax.cond(odd == 1, lambda st: score_one(0, jnp.minimum(1, nk - 1), st),
                      lambda st: st, stats0)

    def score_pair(cc, carry):
        c0 = odd + 2 * cc
        carry = score_one(c0, c0 + 1, carry)
        return score_one(c0 + 1, jnp.minimum(c0 + 2, nk - 1), carry)

    lo_a, hi_a, pos_a, nonneg_a = lax.fori_loop(0, nk // 2, score_pair, stats0)

    def count_ge(v):
        acc = over_chunks(lambda x, r0, a: a + fold(jnp.where(x >= v, 1.0, 0.0), jnp.sum), facc(0.0))
        return col(acc, jnp.sum)

    lo_min = col(lo_a, jnp.min)
    hi_max = col(hi_a, jnp.max)
    n_causal = (t_row + 1).astype(F32)
    short = n_causal < kf

    def bis_step(lo, hi, cl):
        mid = lo * 0.5 + hi * 0.5
        c = count_ge(mid)
        ok = c >= kf
        return jnp.where(ok, mid, lo), jnp.where(ok, hi, mid), jnp.where(ok, c, cl)

    n_pos = col(pos_a, jnp.sum)
    n_nonneg = col(nonneg_a, jnp.sum)
    above_zero = n_nonneg >= kf
    lo0 = jnp.where(above_zero, 0.0, lo_min)
    hi0 = jnp.where(above_zero, hi_max, 0.0)
    cl0 = jnp.where(short, kf, jnp.where(above_zero, n_nonneg, n_causal))
    lo, _, cl = lax.fori_loop(0, BISECT_STEPS, lambda _, s: bis_step(*s), (lo0, hi0, cl0))

    def walk_cond(st):
        return jnp.max(st[3]) > 0.0

    def walk_body(st):
        lo, thr, take, todo = st
        lo_e = col(over_chunks(
            lambda x, r0, a: jnp.minimum(a, fold(jnp.where(x >= lo, x, inf), jnp.min)), facc(inf)), jnp.min)

        def above(x, r0, carry):
            cgt_a, cge_a, nxt_a = carry
            gt = x > lo_e
            return (cgt_a + fold(jnp.where(gt, 1.0, 0.0), jnp.sum),
                    cge_a + fold(jnp.where(x >= lo_e, 1.0, 0.0), jnp.sum),
                    jnp.minimum(nxt_a, fold(jnp.where(gt, x, inf), jnp.min)))

        cgt_a, cge_a, nxt_a = over_chunks(above, (facc(0.0), facc(0.0), facc(inf)))
        cgt, cge, nxt = col(cgt_a, jnp.sum), col(cge_a, jnp.sum), col(nxt_a, jnp.min)
        active = todo > 0.0
        fin = active & (cgt < kf)
        thr = jnp.where(fin, lo_e, thr)
        take = jnp.where(fin & (cge > kf), kf - cgt, take)
        lo = jnp.where(active & (cgt >= kf), nxt, lo)
        todo = jnp.where(fin, 0.0, todo)
        return lo, thr, take, todo

    zero_tie = jnp.logical_not(short) & (n_pos < kf) & above_zero
    thr0 = jnp.where(zero_tie, 0.0, jnp.where(short, lo_min, lo))
    take0 = jnp.where(zero_tie, kf - n_pos, float(seq))
    todo0 = jnp.where(zero_tie | (cl == kf), 0.0, 1.0)
    _, thr, take, _ = lax.while_loop(walk_cond, walk_body, (lo, thr0, take0, todo0))

    def alibi_dist(r0):
        s_idx = r0 + lax.broadcasted_iota(jnp.int32, (ck, tq), 0)
        return (t_row - s_idx).astype(F32)

    tied = take < float(seq)
    far = jnp.float32(seq)

    def locate(c, carry):
        zeros_before, cut_chunk = carry
        zb_ref[c] = jnp.broadcast_to(zeros_before, (8, tq))
        zeros_upto = zeros_before + zc_ref[c][0:1, :]
        hit = zero_tie & (cut_chunk >= far) & (zeros_upto >= take)
        return zeros_upto, jnp.where(hit, c.astype(F32), cut_chunk)

    _, cut_chunk = lax.fori_loop(0, nk, locate, (jnp.zeros((1, tq), F32), jnp.full((1, tq), far)))
    other_tie = jnp.max(jnp.where(tied & jnp.logical_not(zero_tie), 1.0, 0.0)) > 0.0
    first_cut = jnp.min(cut_chunk).astype(jnp.int32)
    last_cut = jnp.max(jnp.where(cut_chunk < far, cut_chunk, -1.0)).astype(jnp.int32)
    ca = jnp.where(other_tie, 0, jnp.minimum(first_cut, nk))
    cb = jnp.where(other_tie, nk - 1, last_cut)
    ties_at_ca = jnp.where(other_tie, 0.0, zb_ref[jnp.minimum(ca, nk - 1)][0:1, :])

    @pl.when((pl.program_id(0) == 0) & (i == 0))
    def _():
        r_i = lax.broadcasted_iota(jnp.int32, (ck, ck), 0)
        c_i = lax.broadcasted_iota(jnp.int32, (ck, ck), 1)
        tri_ref[...] = jnp.where(c_i <= r_i, 1.0, 0.0).astype(BF16)

    def keep_all(c, dmin_a):
        r0 = pl.multiple_of(c * ck, ck)
        x = sc_ref[pl.ds(r0, ck), :]
        pen = jnp.where(x >= thr, alibi_dist(r0), inf)
        sc_ref[pl.ds(r0, ck), :] = pen
        return jnp.minimum(dmin_a, fold(pen, jnp.min))

    def ranked(c, carry):
        ties_before, dmin_a = carry
        r0 = pl.multiple_of(c * ck, ck)
        x = sc_ref[pl.ds(r0, ck), :]
        dist = alibi_dist(r0)
        eq = x == thr
        rank = _dot(tri_ref[...], jnp.where(eq, 1.0, 0.0).astype(BF16)) + ties_before
        tie = jnp.where(eq, jnp.where(rank <= take, dist, inf), inf)
        pen = jnp.where(x > thr, dist, tie)
        sc_ref[pl.ds(r0, ck), :] = pen
        return rank[ck - 1:ck, :], jnp.minimum(dmin_a, fold(pen, jnp.min))

    def keep_untied(c, dmin_a):
        r0 = pl.multiple_of(c * ck, ck)
        x = sc_ref[pl.ds(r0, ck), :]
        dist = alibi_dist(r0)
        tie = jnp.where(x == thr, jnp.where(tied, inf, dist), inf)
        pen = jnp.where(x > thr, dist, tie)
        sc_ref[pl.ds(r0, ck), :] = pen
        return jnp.minimum(dmin_a, fold(pen, jnp.min))

    dmin_a = lax.fori_loop(0, ca, keep_all, facc(inf))
    _, dmin_a = lax.fori_loop(ca, cb + 1, ranked, (ties_at_ca, dmin_a))
    dmin_a = lax.fori_loop(jnp.maximum(cb + 1, ca), nk, keep_untied, dmin_a)

    dmin = col(dmin_a, jnp.min)

    row = lax.broadcasted_iota(jnp.int32, (128, tq), 0)
    for p in range(N_HEADS_A // 2):
        qpair = qat_ref[p * 128:(p + 1) * 128, :]
        rhsa_ref[p] = jnp.concatenate(
            [jnp.where(row < HEAD_DIM_A, qpair, jnp.zeros_like(qpair)),
             jnp.where(row >= HEAD_DIM_A, qpair, jnp.zeros_like(qpair))], axis=1)

    m_ref[...] = jnp.full((N_HEADS_A, tq), SOFTMAX_M_INIT, F32)
    d_ref[...] = jnp.zeros((N_HEADS_A, tq), F32)
    ot_ref[...] = jnp.zeros((WIDTH_A, tq), F32)

    def k_chunk(c):
        r0 = pl.multiple_of(c * ck, ck)
        return [ka_ref[pl.ds(r0, ck), p * 128:(p + 1) * 128] for p in range(N_HEADS_A // 2)]

    def qk_pair(buf, p, kc):
        buf[:, p * 2 * tq:(p + 1) * 2 * tq] = _dot(kc, rhsa_ref[p])

    def qk_into(buf, kcs):
        for p in range(N_HEADS_A // 2):
            qk_pair(buf, p, kcs[p])

    def step(cur, nxt, c, c_next):
        r0 = pl.multiple_of(c * ck, ck)
        pen = sc_ref[pl.ds(r0, ck), :] - dmin
        vts = [vat_ref[h * HEAD_DIM_A:(h + 1) * HEAD_DIM_A, pl.ds(r0, ck)]
               for h in range(N_HEADS_A)]
        kcs = k_chunk(c_next)
        qk_pair(nxt, 0, kcs[0])
        for p in range(N_HEADS_A // 2):
            m_seen = soft_pv(cur, pen, vts, (2 * p, 2 * p + 1))
            if p + 1 < N_HEADS_A // 2:
                tie = jnp.where(m_seen != m_seen, 1.0, 0.0).astype(BF16)
                qk_pair(nxt, p + 1, kcs[p + 1] + tie)

    def soft_pv(buf, pen, vts, heads):
        m_seen = None
        for h in heads:
            rows = slice(h * HEAD_DIM_A, (h + 1) * HEAD_DIM_A)
            alphas, probs = [], []
            for lt in range(tq // 128):
                ln = slice(lt * 128, (lt + 1) * 128)
                m = m_ref[h:h + 1, ln]
                lg = buf[:, h * tq + lt * 128:h * tq + (lt + 1) * 128] - _alibi_slope(h) * pen[:, ln]
                m_new = jnp.maximum(m, col(fold(lg, jnp.max), jnp.max))
                alpha = jnp.exp(m - m_new)
                pr = jnp.exp(lg - m_new)
                m_ref[h:h + 1, ln] = m_new
                d_ref[h:h + 1, ln] = d_ref[h:h + 1, ln] * alpha + col(fold(pr, jnp.sum), jnp.sum)
                alphas.append(alpha)
                probs.append(pr.astype(BF16))
                m_seen = m_new if m_seen is None else m_seen
            ot_ref[rows, :] = (ot_ref[rows, :] * jnp.concatenate(alphas, axis=1)
                               + _dot(vts[h], jnp.concatenate(probs, axis=1)))
        return m_seen

    last = nk - 1
    lg0_ref = lga_ref.at[:, 0:2 * tq]

    def direct_one(c):
        r0 = pl.multiple_of(c * ck, ck)
        pen = sc_ref[pl.ds(r0, ck), :] - dmin
        vts = [vat_ref[h * HEAD_DIM_A:(h + 1) * HEAD_DIM_A, pl.ds(r0, ck)]
               for h in range(N_HEADS_A)]
        kcs = k_chunk(c)
        k_next = ka_ref[pl.ds(pl.multiple_of(jnp.minimum(c + 1, last) * ck, ck), ck), 0:128]
        lg2 = lg0_ref[...]
        for p in range(N_HEADS_A // 2):
            lg2_next = None
            for j in range(2):
                h = 2 * p + j
                rows = slice(h * HEAD_DIM_A, (h + 1) * HEAD_DIM_A)
                probs = []
                for lt in range(tq // 128):
                    ln = slice(lt * 128, (lt + 1) * 128)
                    e = jnp.exp(lg2[:, j * tq + lt * 128:j * tq + (lt + 1) * 128]
                                - _alibi_slope(h) * pen[:, ln])
                    esum = col(fold(e, jnp.sum), jnp.sum)
                    d_ref[h:h + 1, ln] = d_ref[h:h + 1, ln] + esum
                    probs.append(e.astype(BF16))
                    if j == 0 and lt == 0:
                        if p + 1 < N_HEADS_A // 2:
                            lg2_next = _dot(kcs[p + 1], rhsa_ref[p + 1])
                        if p + 2 == N_HEADS_A // 2:
                            lg0_ref[...] = _dot(k_next, rhsa_ref[0])
                ot_ref[rows, :] = ot_ref[rows, :] + _dot(vts[h], jnp.concatenate(probs, axis=1))
            lg2 = lg2_next

    @pl.when(direct_ref[0] > 0)
    def _():
        lg0_ref[...] = _dot(ka_ref[0:ck, 0:128], rhsa_ref[0])

        @pl.when(odd == 1)
        def _():
            direct_one(0)

        def direct_pair(cc, carry):
            c0 = odd + 2 * cc
            direct_one(c0)
            direct_one(c0 + 1)
            return carry

        lax.fori_loop(0, nk // 2, direct_pair, 0)

    @pl.when(direct_ref[0] <= 0)
    def _():
        qk_into(lga_ref, k_chunk(0))

        def attn_pair(cc, carry):
            c0 = 2 * cc
            step(lga_ref, lgb_ref, c0, jnp.minimum(c0 + 1, last))

            @pl.when(c0 + 1 < nk)
            def _():
                step(lgb_ref, lga_ref, c0 + 1, jnp.minimum(c0 + 2, last))

            return carry

        lax.fori_loop(0, (nk + 1) // 2, attn_pair, 0)

    for h in range(N_HEADS_A):
        rows = slice(h * HEAD_DIM_A, (h + 1) * HEAD_DIM_A)
        ot_ref[rows, :] = ot_ref[rows, :] / d_ref[h:h + 1, :]
    out_ref[...] = ot_ref[...].T


def _dsa_attn(qat, qit, wt, ka, vat, kib, q_norm_a, k_norm_a, batch, seq):
    tq = min(TQ, seq)
    topk = min(TOPK_MAX, seq // 4)
    qk_bound = HEAD_DIM_A ** 0.5 * jnp.max(jnp.abs(q_norm_a)) * jnp.max(jnp.abs(k_norm_a)) * 1.01
    direct = (qk_bound <= SOFTMAX_DIRECT_BOUND).astype(jnp.int32).reshape(1)
    kern = functools.partial(_dsa_attn_kernel, seq=seq, topk=topk)
    q_spec = lambda rows: pl.BlockSpec((None, rows, tq), lambda b, i: (b, 0, i))
    return pl.pallas_call(
        kern,
        grid=(batch, seq // tq),
        in_specs=[
            q_spec(WIDTH_A), q_spec(WIDTH_A), q_spec(N_IDX_HEADS),
            pl.BlockSpec((None, seq, WIDTH_A), lambda b, i: (b, 0, 0)),
            pl.BlockSpec((None, WIDTH_A, seq), lambda b, i: (b, 0, 0)),
            pl.BlockSpec((None, seq, 128), lambda b, i: (b, 0, 0)),
            pl.BlockSpec(memory_space=pltpu.SMEM),
        ],
        out_specs=pl.BlockSpec((None, tq, WIDTH_A), lambda b, i: (b, i, 0)),
        out_shape=jax.ShapeDtypeStruct((batch, seq, WIDTH_A), F32),
        scratch_shapes=[
            pltpu.VMEM((seq, tq), F32),
            pltpu.VMEM((WIDTH_A, tq), F32),
            pltpu.VMEM((N_IDX_HEADS, 128, tq), BF16),
            pltpu.VMEM((N_HEADS_A // 2, 128, 2 * tq), BF16),
            pltpu.VMEM((tq, tq), BF16),
            pltpu.VMEM((N_HEADS_A, tq), F32),
            pltpu.VMEM((N_HEADS_A, tq), F32),
            pltpu.VMEM((tq, N_HEADS_A * tq), F32),
            pltpu.VMEM((tq, N_HEADS_A * tq), F32),
            pltpu.VMEM((seq // tq, 8, tq), F32),
            pltpu.VMEM((seq // tq, 8, tq), F32),
        ],
        compiler_params=pltpu.CompilerParams(
            dimension_semantics=("arbitrary", "arbitrary"), vmem_limit_bytes=VMEM_LIMIT_BYTES),
        name="dsa_attn",
    )(qat, qit, wt, ka.reshape(batch, seq, WIDTH_A), vat, kib.reshape(batch, seq, 128), direct)


_M_BG, _M_CG, _M_HB, _M_ZB, _M_QM, _M_ZM, _M_G = (0, 512, 1024, 1536, 2048, 2560, 3072)
_M_COLS = 3072 + N_BRANCHES * D_MODEL


def _merge_kernel(x_ref, attn_ref, g_ref, wza_ref, wc_ref, bg_ref, cw_ref, mk_ref, mv_ref,
                  gqm_ref, wa_ref, wb_ref, wm_ref, wo_ref, out_ref, utail_ref, *, per_b):
    tm = x_ref.shape[0]
    xf = x_ref[...]
    xn = (_rms_rows(xf) * g_ref[...]).astype(BF16)

    def proj(lhs, c0, width):
        return _dot(lhs, wc_ref[:, c0:c0 + width])

    za = _dot(xn, wza_ref[...])
    ya = _dot((attn_ref[...] * jax.nn.silu(za)).astype(BF16), wa_ref[...])

    u = proj(xn, _M_CG, WIDTH_B) * proj(xn, _M_HB, WIDTH_B)
    @pl.when((pl.program_id(0) % per_b) == 0)
    def _():
        utail_ref[...] = jnp.zeros((8, WIDTH_B), F32)

    uh = utail_ref[...]
    utail_ref[...] = u[tm - 8:, :]
    rows = lax.broadcasted_iota(jnp.int32, (tm, WIDTH_B), 0)
    u1 = jnp.where(rows == 0, uh[7:8, :], pltpu.roll(u, 1, 0))
    u2 = jnp.where(rows == 0, uh[6:7, :], jnp.where(rows == 1, uh[7:8, :], pltpu.roll(u, 2, 0)))
    conv = cw_ref[0:1, :] * u2 + cw_ref[1:2, :] * u1 + cw_ref[2:3, :] * u
    bgate = proj(xn, _M_BG, WIDTH_B)
    zb = proj(xn, _M_ZB, WIDTH_B)
    yb = _dot(((bgate * conv) * jax.nn.silu(zb)).astype(BF16), wb_ref[...])

    qm = proj(xn, _M_QM, WIDTH_M)
    zm = proj(xn, _M_ZM, WIDTH_M)
    heads = []
    for h in range(N_HEADS_M):
        sl = slice(h * HEAD_DIM_M, (h + 1) * HEAD_DIM_M)
        qh = (_rms_rows(qm[:, sl]) * gqm_ref[...]).astype(BF16)
        lg = _dot_nt(qh, mk_ref[:, sl]) * (HEAD_DIM_M ** -0.5)
        e = jnp.exp(lg - jnp.max(lg, axis=-1, keepdims=True))
        p = e / jnp.sum(e, axis=-1, keepdims=True)
        heads.append(_dot(p.astype(BF16), mv_ref[:, sl]))
    attn_m = jnp.concatenate(heads, axis=1)
    ym = _dot((attn_m * jax.nn.silu(zm)).astype(BF16), wm_ref[...])

    def gate(j):
        gj = proj(xn, _M_G + j * D_MODEL, D_MODEL) + bg_ref[:, j * D_MODEL:(j + 1) * D_MODEL]
        return jax.nn.sigmoid(gj)

    merged = gate(0) * ya + gate(1) * yb + gate(2) * ym
    out_ref[...] = xf + _dot(merged.astype(BF16), wo_ref[...])


def _merge(x2, attn2, mk, mv, norm_g, w_in, b_gate, conv_w, q_norm_m,
           w_out_a, w_out_b, w_out_m, w_o, batch, seq):
    n = x2.shape[0]
    tm = min(TM_C, seq)
    per_b = seq // tm
    m_len = mk.shape[1]
    wza = w_in[:, _C_ZA:_C_ZA + WIDTH_A].astype(BF16)
    wc = w_in[:, _C_REST:].astype(BF16)
    const = lambda shape: pl.BlockSpec(shape, lambda i: (0,) * len(shape),
                                       pipeline_mode=pl.Buffered(1))
    kern = functools.partial(_merge_kernel, per_b=per_b)
    return pl.pallas_call(
        kern,
        grid=(n // tm,),
        in_specs=[
            pl.BlockSpec((tm, D_MODEL), lambda i: (i, 0)),
            pl.BlockSpec((tm, WIDTH_A), lambda i: (i, 0)),
            const((1, D_MODEL)), const((D_MODEL, WIDTH_A)), const((D_MODEL, _M_COLS)),
            const((1, N_BRANCHES * D_MODEL)),
            const((CONV_WIDTH, WIDTH_B)),
            pl.BlockSpec((None, m_len, WIDTH_M), lambda i: (i // per_b, 0, 0)),
            pl.BlockSpec((None, m_len, WIDTH_M), lambda i: (i // per_b, 0, 0)),
            const((1, HEAD_DIM_M)),
            const((WIDTH_A, D_MODEL)), const((WIDTH_B, D_MODEL)), const((WIDTH_M, D_MODEL)),
            const((D_MODEL, D_MODEL)),
        ],
        out_specs=pl.BlockSpec((tm, D_MODEL), lambda i: (i, 0)),
        out_shape=jax.ShapeDtypeStruct((n, D_MODEL), F32),
        scratch_shapes=[pltpu.VMEM((8, WIDTH_B), F32)],
        compiler_params=pltpu.CompilerParams(
            dimension_semantics=("arbitrary",), vmem_limit_bytes=VMEM_LIMIT_BYTES),
        name="merge",
    )(x2, attn2, norm_g.reshape(1, D_MODEL), wza, wc, b_gate.reshape(1, -1), conv_w,
      mk, mv, q_norm_m.reshape(1, HEAD_DIM_M),
      w_out_a.astype(BF16), w_out_b.astype(BF16), w_out_m.astype(BF16), w_o.astype(BF16))


def _layer(h, mem, norm_g, mem_norm_g, w_in, b_gate, w_mem_kv, q_norm_a, k_norm_a,
           q_norm_m, k_norm_m, conv_w, w_out_a, w_out_b, w_out_m, w_o):
    batch, seq, _ = h.shape
    x2 = h.reshape(batch * seq, D_MODEL)
    qat, ka, vat, qit, kib, wt = _proj_a(x2, norm_g, w_in, q_norm_a, k_norm_a, batch, seq)
    mk, mv = _mem_kv(mem, mem_norm_g, w_mem_kv, k_norm_m)
    attn = _dsa_attn(qat, qit, wt, ka, vat, kib, q_norm_a, k_norm_a, batch, seq)
    out = _merge(x2, attn.reshape(batch * seq, WIDTH_A), mk, mv, norm_g, w_in, b_gate, conv_w,
                 q_norm_m, w_out_a, w_out_b, w_out_m, w_o, batch, seq)
    return out.reshape(batch, seq, D_MODEL)


def kernel(x, mem, norm_g, mem_norm_g, w_in, b_gate, w_mem_kv, q_norm_a, k_norm_a,
           q_norm_m, k_norm_m, conv_w, w_out_a, w_out_b, w_out_m, w_o):
    h = x
    for l in range(norm_g.shape[0]):
        h = _layer(h, mem, norm_g[l], mem_norm_g[l], w_in[l], b_gate[l], w_mem_kv[l],
                   q_norm_a[l], k_norm_a[l], q_norm_m[l], k_norm_m[l], conv_w[l],
                   w_out_a[l], w_out_b[l], w_out_m[l], w_o[l])
    return h
```

```python
import functools

import jax
import jax.numpy as jnp
import numpy as np
from jax import lax
from jax.experimental import pallas as pl
from jax.experimental.pallas import tpu as pltpu

F32 = jnp.float32
BF16 = jnp.bfloat16

D_MODEL = 1024
N_HEADS_A = 8
HEAD_DIM_A = 64
WIDTH_A = 512
N_IDX_HEADS = 8
IDX_DIM = 64
TOPK_MAX = 256
WIDTH_B = 512
CONV_WIDTH = 3
N_HEADS_M = 4
HEAD_DIM_M = 128
WIDTH_M = 512
N_BRANCHES = 3
RMS_EPS = 1e-6

_C_QA, _C_KA, _C_VA, _C_ZA = 0, 512, 1024, 1536
_C_QI, _C_KI, _C_WI = 2048, 2560, 2624
_C_REST = 2632
_D_IN = 8776

VMEM_LIMIT_BYTES = 56 * 1024 * 1024

TM_A = 512
TM_C = 512
TQ = 256
FOLD_ROWS = 32
BISECT_STEPS = 19
SOFTMAX_DIRECT_BOUND = 60.0
SOFTMAX_M_INIT = -1e30


def _rms_rows(xf, eps=RMS_EPS):
    return xf * lax.rsqrt(jnp.mean(xf * xf, axis=-1, keepdims=True) + eps)


def _dot(a, b):
    return jnp.dot(a, b, preferred_element_type=F32)


def _dot_nt(a, b):
    return lax.dot_general(a, b, (((1,), (1,)), ((), ())), preferred_element_type=F32)


def _proj_a_kernel(x_ref, g_ref, w1_ref, w2t_ref, gsum_ref, gk_ref, gq_ref,
                   qat_ref, ka_ref, vat_ref, qit_ref, kib_ref, wt_ref):
    tm = x_ref.shape[0]
    xn = (_rms_rows(x_ref[...]) * g_ref[...]).astype(BF16)

    y1 = _dot(xn, w1_ref[...])
    ka_raw = y1[:, :WIDTH_A]
    sq = ka_raw * ka_raw
    sq_hi = sq.astype(BF16)
    sq_lo = (sq - sq_hi.astype(F32)).astype(BF16)
    ss = _dot(sq_hi, gsum_ref[...]) + _dot(sq_lo, gsum_ref[...])
    ka = ka_raw * lax.rsqrt(ss * (1.0 / HEAD_DIM_A) + RMS_EPS) * gk_ref[...]
    ka_ref[...] = ka.astype(BF16)
    kiwi = y1[:, WIDTH_A:]
    lane = lax.broadcasted_iota(jnp.int32, kiwi.shape, 1)
    kib_ref[...] = jnp.where(lane < IDX_DIM, kiwi, 0.0).astype(BF16)
    wt_ref[...] = kiwi.T[IDX_DIM:IDX_DIM + N_IDX_HEADS, :]

    yt = _dot_nt(w2t_ref[...], xn)
    gq = jnp.concatenate([gq_ref[...]] * (tm // 128), axis=1)
    for h in range(N_HEADS_A):
        qh = yt[h * HEAD_DIM_A:(h + 1) * HEAD_DIM_A, :]
        ms = jnp.mean(qh * qh, axis=0, keepdims=True)
        qn = qh * lax.rsqrt(ms + RMS_EPS) * gq
        qat_ref[h * HEAD_DIM_A:(h + 1) * HEAD_DIM_A, :] = (qn * (HEAD_DIM_A ** -0.5)).astype(BF16)
    vat_ref[...] = yt[WIDTH_A:2 * WIDTH_A, :].astype(BF16)
    qit_ref[...] = yt[2 * WIDTH_A:3 * WIDTH_A, :].astype(BF16)


def _proj_a(x2, norm_g, w_in, q_norm_a, k_norm_a, batch, seq):
    n = x2.shape[0]
    tm = min(TM_A, seq)
    per_b = seq // tm
    w1 = jnp.concatenate(
        [w_in[:, _C_KA:_C_KA + 512], w_in[:, _C_KI:_C_KI + 72],
         jnp.zeros((D_MODEL, 56), F32)], axis=1).astype(BF16)
    w2t = jnp.concatenate(
        [w_in[:, _C_QA:_C_QA + 512], w_in[:, _C_VA:_C_VA + 512],
         w_in[:, _C_QI:_C_QI + 512]], axis=1).T.astype(BF16)
    hid = np.arange(WIDTH_A) // HEAD_DIM_A
    gsum = jnp.asarray(hid[:, None] == hid[None, :], BF16)
    gk = jnp.tile(k_norm_a.reshape(1, HEAD_DIM_A), (1, N_HEADS_A))
    gq = jnp.broadcast_to(q_norm_a.reshape(HEAD_DIM_A, 1), (HEAD_DIM_A, 128))

    full = lambda shape: pl.BlockSpec(shape, lambda i: (0,) * len(shape))
    t_spec = lambda rows: pl.BlockSpec((None, rows, tm), lambda i: (i // per_b, 0, i % per_b))
    return pl.pallas_call(
        _proj_a_kernel,
        grid=(n // tm,),
        in_specs=[
            pl.BlockSpec((tm, D_MODEL), lambda i: (i, 0)),
            full((1, D_MODEL)), full((D_MODEL, 640)), full((1536, D_MODEL)),
            full((WIDTH_A, WIDTH_A)), full((1, WIDTH_A)), full((HEAD_DIM_A, 128)),
        ],
        out_specs=[
            t_spec(WIDTH_A),
            pl.BlockSpec((tm, WIDTH_A), lambda i: (i, 0)),
            t_spec(WIDTH_A), t_spec(WIDTH_A),
            pl.BlockSpec((tm, 128), lambda i: (i, 0)),
            t_spec(N_IDX_HEADS),
        ],
        out_shape=[
            jax.ShapeDtypeStruct((batch, WIDTH_A, seq), BF16),
            jax.ShapeDtypeStruct((n, WIDTH_A), BF16),
            jax.ShapeDtypeStruct((batch, WIDTH_A, seq), BF16),
            jax.ShapeDtypeStruct((batch, WIDTH_A, seq), BF16),
            jax.ShapeDtypeStruct((n, 128), BF16),
            jax.ShapeDtypeStruct((batch, N_IDX_HEADS, seq), F32),
        ],
        compiler_params=pltpu.CompilerParams(
            dimension_semantics=("arbitrary",), vmem_limit_bytes=VMEM_LIMIT_BYTES),
        name="proj_a",
    )(x2, norm_g.reshape(1, D_MODEL), w1, w2t, gsum, gk, gq)


def _mem_kv_kernel(mem_ref, g_ref, w_ref, gk_ref, mk_ref, mv_ref):
    mn = (_rms_rows(mem_ref[...]) * g_ref[...]).astype(BF16)
    y = _dot(mn, w_ref[...])
    for h in range(N_HEADS_M):
        kh = y[:, h * HEAD_DIM_M:(h + 1) * HEAD_DIM_M]
        mk_ref[:, h * HEAD_DIM_M:(h + 1) * HEAD_DIM_M] = (_rms_rows(kh) * gk_ref[...]).astype(BF16)
    mv_ref[...] = y[:, WIDTH_M:].astype(BF16)


def _mem_kv(mem, mem_norm_g, w_mem_kv, k_norm_m):
    batch, m_len, _ = mem.shape
    full = lambda shape: pl.BlockSpec(shape, lambda b: (0,) * len(shape))
    return pl.pallas_call(
        _mem_kv_kernel,
        grid=(batch,),
        in_specs=[
            pl.BlockSpec((None, m_len, D_MODEL), lambda b: (b, 0, 0)),
            full((1, D_MODEL)), full((D_MODEL, 2 * WIDTH_M)), full((1, HEAD_DIM_M)),
        ],
        out_specs=[pl.BlockSpec((None, m_len, WIDTH_M), lambda b: (b, 0, 0))] * 2,
        out_shape=[jax.ShapeDtypeStruct((batch, m_len, WIDTH_M), BF16)] * 2,
        compiler_params=pltpu.CompilerParams(
            dimension_semantics=("arbitrary",), vmem_limit_bytes=VMEM_LIMIT_BYTES),
        name="mem_kv",
    )(mem, mem_norm_g.reshape(1, D_MODEL), w_mem_kv.astype(BF16), k_norm_m.reshape(1, HEAD_DIM_M))


def _alibi_slope(h):
    return float(2.0 ** (-8.0 * (h + 1) / N_HEADS_A))


def _dsa_attn_kernel(qat_ref, qit_ref, wt_ref, ka_ref, vat_ref, kib_ref, direct_ref, out_ref,
                     sc_ref, ot_ref, rhsi_ref, rhsa_ref, tri_ref, m_ref, d_ref, lga_ref, lgb_ref,
                     zc_ref, zb_ref, *, seq, topk):
    tq = qat_ref.shape[1]
    ck = tq
    i = pl.program_id(1)
    t0 = i * tq
    nk = i + 1
    inf = jnp.float32(jnp.inf)
    kf = jnp.float32(topk)
    t_row = t0 + lax.broadcasted_iota(jnp.int32, (1, tq), 1)

    def fold(w, op):
        return op(w.reshape(ck // FOLD_ROWS, FOLD_ROWS, w.shape[1]), axis=0)

    def over_chunks(body, init):
        def step(c, carry):
            r0 = pl.multiple_of(c * ck, ck)
            return body(sc_ref[pl.ds(r0, ck), :], r0, carry)
        return lax.fori_loop(0, nk, step, init)

    def col(acc, op):
        return op(acc, axis=0, keepdims=True)

    facc = lambda v: jnp.full((FOLD_ROWS, tq), v, F32)

    zero_rows = jnp.zeros((128 - IDX_DIM, tq), BF16)
    for h in range(N_IDX_HEADS):
        rhsi_ref[h] = jnp.concatenate(
            [qit_ref[h * IDX_DIM:(h + 1) * IDX_DIM, :], zero_rows], axis=0)
    idx_scale = (IDX_DIM ** -0.5) * (N_IDX_HEADS ** -0.5)

    n_ahead = N_IDX_HEADS // 2

    def dots_ahead(kc):
        for h in range(n_ahead):
            lgb_ref[:, h * tq:(h + 1) * tq] = _dot(kc, rhsi_ref[h])

    dots_ahead(kib_ref[0:ck, :])

    def score_one(c, carry, diagonal):
        lo_a, hi_a, pos_a, nonneg_a = carry
        r0 = pl.multiple_of(c * ck, ck)
        kc = kib_ref[pl.ds(r0, ck), :]
        acc = jnp.zeros((ck, tq), F32)
        for h in range(n_ahead):
            acc = acc + jnp.maximum(lgb_ref[:, h * tq:(h + 1) * tq], 0.0) * wt_ref[h:h + 1, :]
        for h in range(n_ahead, N_IDX_HEADS):
            d = _dot(kc, rhsi_ref[h])
            acc = acc + jnp.maximum(d, 0.0) * wt_ref[h:h + 1, :]
        acc = acc * idx_scale
        if diagonal:
            causal = r0 + lax.broadcasted_iota(jnp.int32, (ck, tq), 0) <= t_row
            masked = jnp.where(causal, acc, -inf)
            floor_in = jnp.where(causal, acc, inf)
        else:
            dots_ahead(kib_ref[pl.ds(pl.multiple_of((c + 1) * ck, ck), ck), :])
            masked = floor_in = acc
        sc_ref[pl.ds(r0, ck), :] = masked
        pos_c = fold(jnp.where(masked > 0.0, 1.0, 0.0), jnp.sum)
        nonneg_c = fold(jnp.where(masked >= 0.0, 1.0, 0.0), jnp.sum)
        zc_ref[c] = jnp.broadcast_to(col(nonneg_c - pos_c, jnp.sum), (8, tq))
        return (jnp.minimum(lo_a, fold(floor_in, jnp.min)),
                jnp.maximum(hi_a, fold(masked, jnp.max)),
                pos_a + pos_c, nonneg_a + nonneg_c)

    odd_full = (nk - 1) % 2
    stats0 = (facc(inf), facc(-inf), facc(0.0), facc(0.0))
    stats0 = lax.cond(odd_full == 1, lambda st: score_one(0, st, False), lambda st: st, stats0)

    def score_pair(cc, carry):
        c0 = odd_full + 2 * cc
        return score_one(c0 + 1, score_one(c0, carry, False), False)

    stats = lax.fori_loop(0, (nk - 1) // 2, score_pair, stats0)
    lo_a, hi_a, pos_a, nonneg_a = score_one(nk - 1, stats, True)
    odd = nk % 2

    def count_ge(v):
        acc = over_chunks(lambda x, r0, a: a + fold(jnp.where(x >= v, 1.0, 0.0), jnp.sum), facc(0.0))
        return col(acc, jnp.sum)

    lo_min = col(lo_a, jnp.min)
    hi_max = col(hi_a, jnp.max)
    n_causal = (t_row + 1).astype(F32)
    short = n_causal < kf

    def bis_step(lo, hi, cl):
        mid = lo * 0.5 + hi * 0.5
        c = count_ge(mid)
        ok = c >= kf
        return jnp.where(ok, mid, lo), jnp.where(ok, hi, mid), jnp.where(ok, c, cl)

    n_pos = col(pos_a, jnp.sum)
    n_nonneg = col(nonneg_a, jnp.sum)
    above_zero = n_nonneg >= kf
    lo0 = jnp.where(above_zero, 0.0, lo_min)
    hi0 = jnp.where(above_zero, hi_max, 0.0)
    cl0 = jnp.where(short, kf, jnp.where(above_zero, n_nonneg, n_causal))
    lo, _, cl = lax.fori_loop(0, BISECT_STEPS, lambda _, s: bis_step(*s), (lo0, hi0, cl0))

    def walk_cond(st):
        return jnp.max(st[3]) > 0.0

    def walk_body(st):
        lo, thr, take, todo = st
        lo_e = col(over_chunks(
            lambda x, r0, a: jnp.minimum(a, fold(jnp.where(x >= lo, x, inf), jnp.min)), facc(inf)), jnp.min)

        def above(x, r0, carry):
            cgt_a, cge_a, nxt_a = carry
            gt = x > lo_e
            return (cgt_a + fold(jnp.where(gt, 1.0, 0.0), jnp.sum),
                    cge_a + fold(jnp.where(x >= lo_e, 1.0, 0.0), jnp.sum),
                    jnp.minimum(nxt_a, fold(jnp.where(gt, x, inf), jnp.min)))

        cgt_a, cge_a, nxt_a = over_chunks(above, (facc(0.0), facc(0.0), facc(inf)))
        cgt, cge, nxt = col(cgt_a, jnp.sum), col(cge_a, jnp.sum), col(nxt_a, jnp.min)
        active = todo > 0.0
        fin = active & (cgt < kf)
        thr = jnp.where(fin, lo_e, thr)
        take = jnp.where(fin & (cge > kf), kf - cgt, take)
        lo = jnp.where(active & (cgt >= kf), nxt, lo)
        todo = jnp.where(fin, 0.0, todo)
        return lo, thr, take, todo

    zero_tie = jnp.logical_not(short) & (n_pos < kf) & above_zero
    thr0 = jnp.where(zero_tie, 0.0, jnp.where(short, lo_min, lo))
    take0 = jnp.where(zero_tie, kf - n_pos, float(seq))
    todo0 = jnp.where(zero_tie | (cl == kf), 0.0, 1.0)
    _, thr, take, _ = lax.while_loop(walk_cond, walk_body, (lo, thr0, take0, todo0))

    def alibi_dist(r0):
        s_idx = r0 + lax.broadcasted_iota(jnp.int32, (ck, tq), 0)
        return (t_row - s_idx).astype(F32)

    tied = take < float(seq)
    far = jnp.float32(seq)

    def locate(c, carry):
        zeros_before, cut_chunk = carry
        zb_ref[c] = jnp.broadcast_to(zeros_before, (8, tq))
        zeros_upto = zeros_before + zc_ref[c][0:1, :]
        hit = zero_tie & (cut_chunk >= far) & (zeros_upto >= take)
        return zeros_upto, jnp.where(hit, c.astype(F32), cut_chunk)

    _, cut_chunk = lax.fori_loop(0, nk, locate, (jnp.zeros((1, tq), F32), jnp.full((1, tq), far)))
    other_tie = jnp.max(jnp.where(tied & jnp.logical_not(zero_tie), 1.0, 0.0)) > 0.0
    first_cut = jnp.min(cut_chunk).astype(jnp.int32)
    last_cut = jnp.max(jnp.where(cut_chunk < far, cut_chunk, -1.0)).astype(jnp.int32)
    ca = jnp.where(other_tie, 0, jnp.minimum(first_cut, nk))
    cb = jnp.where(other_tie, nk - 1, last_cut)
    ties_at_ca = jnp.where(other_tie, 0.0, zb_ref[jnp.minimum(ca, nk - 1)][0:1, :])

    @pl.when((pl.program_id(0) == 0) & (i == 0))
    def _():
        r_i = lax.broadcasted_iota(jnp.int32, (ck, ck), 0)
        c_i = lax.broadcasted_iota(jnp.int32, (ck, ck), 1)
        tri_ref[...] = jnp.where(c_i <= r_i, 1.0, 0.0).astype(BF16)

    def keep_all(c, dmin_a):
        r0 = pl.multiple_of(c * ck, ck)
        x = sc_ref[pl.ds(r0, ck), :]
        pen = jnp.where(x >= thr, alibi_dist(r0), inf)
        sc_ref[pl.ds(r0, ck), :] = pen
        return jnp.minimum(dmin_a, fold(pen, jnp.min))

    def ranked(c, carry):
        ties_before, dmin_a = carry
        r0 = pl.multiple_of(c * ck, ck)
        x = sc_ref[pl.ds(r0, ck), :]
        dist = alibi_dist(r0)
        eq = x == thr
        rank = _dot(tri_ref[...], jnp.where(eq, 1.0, 0.0).astype(BF16)) + ties_before
        tie = jnp.where(eq, jnp.where(rank <= take, dist, inf), inf)
        pen = jnp.where(x > thr, dist, tie)
        sc_ref[pl.ds(r0, ck), :] = pen
        return rank[ck - 1:ck, :], jnp.minimum(dmin_a, fold(pen, jnp.min))

    def keep_untied(c, dmin_a):
        r0 = pl.multiple_of(c * ck, ck)
        x = sc_ref[pl.ds(r0, ck), :]
        dist = alibi_dist(r0)
        tie = jnp.where(x == thr, jnp.where(tied, inf, dist), inf)
        pen = jnp.where(x > thr, dist, tie)
        sc_ref[pl.ds(r0, ck), :] = pen
        return jnp.minimum(dmin_a, fold(pen, jnp.min))

    dmin_a = lax.fori_loop(0, ca, keep_all, facc(inf))
    _, dmin_a = lax.fori_loop(ca, cb + 1, ranked, (ties_at_ca, dmin_a))
    dmin_a = lax.fori_loop(jnp.maximum(cb + 1, ca), nk, keep_untied, dmin_a)

    dmin = col(dmin_a, jnp.min)

    row = lax.broadcasted_iota(jnp.int32, (128, tq), 0)
    for p in range(N_HEADS_A // 2):
        qpair = qat_ref[p * 128:(p + 1) * 128, :]
        rhsa_ref[p] = jnp.concatenate(
            [jnp.where(row < HEAD_DIM_A, qpair, jnp.zeros_like(qpair)),
             jnp.where(row >= HEAD_DIM_A, qpair, jnp.zeros_like(qpair))], axis=1)

    m_ref[...] = jnp.full((N_HEADS_A, tq), SOFTMAX_M_INIT, F32)
    d_ref[...] = jnp.zeros((N_HEADS_A, tq), F32)
    ot_ref[...] = jnp.zeros((WIDTH_A, tq), F32)

    def k_chunk(c):
        r0 = pl.multiple_of(c * ck, ck)
        return [ka_ref[pl.ds(r0, ck), p * 128:(p + 1) * 128] for p in range(N_HEADS_A // 2)]

    def qk_pair(buf, p, kc):
        buf[:, p * 2 * tq:(p + 1) * 2 * tq] = _dot(kc, rhsa_ref[p])

    def qk_into(buf, kcs):
        for p in range(N_HEADS_A // 2):
            qk_pair(buf, p, kcs[p])

    def step(cur, nxt, c, c_next):
        r0 = pl.multiple_of(c * ck, ck)
        pen = sc_ref[pl.ds(r0, ck), :] - dmin
        vts = [vat_ref[h * HEAD_DIM_A:(h + 1) * HEAD_DIM_A, pl.ds(r0, ck)]
               for h in range(N_HEADS_A)]
        kcs = k_chunk(c_next)
        qk_pair(nxt, 0, kcs[0])
        for p in range(N_HEADS_A // 2):
            m_seen = soft_pv(cur, pen, vts, (2 * p, 2 * p + 1))
            if p + 1 < N_HEADS_A // 2:
                tie = jnp.where(m_seen != m_seen, 1.0, 0.0).astype(BF16)
                qk_pair(nxt, p + 1, kcs[p + 1] + tie)

    def soft_pv(buf, pen, vts, heads):
        m_seen = None
        for h in heads:
            rows = slice(h * HEAD_DIM_A, (h + 1) * HEAD_DIM_A)
            alphas, probs = [], []
            for lt in range(tq // 128):
                ln = slice(lt * 128, (lt + 1) * 128)
                m = m_ref[h:h + 1, ln]
                lg = buf[:, h * tq + lt * 128:h * tq + (lt + 1) * 128] - _alibi_slope(h) * pen[:, ln]
                m_new = jnp.maximum(m, col(fold(lg, jnp.max), jnp.max))
                alpha = jnp.exp(m - m_new)
                pr = jnp.exp(lg - m_new)
                m_ref[h:h + 1, ln] = m_new
                d_ref[h:h + 1, ln] = d_ref[h:h + 1, ln] * alpha + col(fold(pr, jnp.sum), jnp.sum)
                alphas.append(alpha)
                probs.append(pr.astype(BF16))
                m_seen = m_new if m_seen is None else m_seen
            ot_ref[rows, :] = (ot_ref[rows, :] * jnp.concatenate(alphas, axis=1)
                               + _dot(vts[h], jnp.concatenate(probs, axis=1)))
        return m_seen

    last = nk - 1
    lg0_ref = lga_ref.at[:, 0:2 * tq]

    def direct_one(c):
        r0 = pl.multiple_of(c * ck, ck)
        pen = sc_ref[pl.ds(r0, ck), :] - dmin
        vts = [vat_ref[h * HEAD_DIM_A:(h + 1) * HEAD_DIM_A, pl.ds(r0, ck)]
               for h in range(N_HEADS_A)]
        kcs = k_chunk(c)
        k_next = ka_ref[pl.ds(pl.multiple_of(jnp.minimum(c + 1, last) * ck, ck), ck), 0:128]
        lg2 = lg0_ref[...]
        for p in range(N_HEADS_A // 2):
            lg2_next = None
            for j in range(2):
                h = 2 * p + j
                rows = slice(h * HEAD_DIM_A, (h + 1) * HEAD_DIM_A)
                probs = []
                for lt in range(tq // 128):
                    ln = slice(lt * 128, (lt + 1) * 128)
                    e = jnp.exp(lg2[:, j * tq + lt * 128:j * tq + (lt + 1) * 128]
                                - _alibi_slope(h) * pen[:, ln])
                    esum = col(fold(e, jnp.sum), jnp.sum)
                    d_ref[h:h + 1, ln] = d_ref[h:h + 1, ln] + esum
                    probs.append(e.astype(BF16))
                    if j == 0 and lt == 0:
                        if p + 1 < N_HEADS_A // 2:
                            lg2_next = _dot(kcs[p + 1], rhsa_ref[p + 1])
                        if p + 2 == N_HEADS_A // 2:
                            lg0_ref[...] = _dot(k_next, rhsa_ref[0])
                ot_ref[rows, :] = ot_ref[rows, :] + _dot(vts[h], jnp.concatenate(probs, axis=1))
            lg2 = lg2_next

    @pl.when(direct_ref[0] > 0)
    def _():
        lg0_ref[...] = _dot(ka_ref[0:ck, 0:128], rhsa_ref[0])

        @pl.when(odd == 1)
        def _():
            direct_one(0)

        def direct_pair(cc, carry):
            c0 = odd + 2 * cc
            direct_one(c0)
            direct_one(c0 + 1)
            return carry

        lax.fori_loop(0, nk // 2, direct_pair, 0)

    @pl.when(direct_ref[0] <= 0)
    def _():
        qk_into(lga_ref, k_chunk(0))

        def attn_pair(cc, carry):
            c0 = 2 * cc
            step(lga_ref, lgb_ref, c0, jnp.minimum(c0 + 1, last))

            @pl.when(c0 + 1 < nk)
            def _():
                step(lgb_ref, lga_ref, c0 + 1, jnp.minimum(c0 + 2, last))

            return carry

        lax.fori_loop(0, (nk + 1) // 2, attn_pair, 0)

    for h in range(N_HEADS_A):
        rows = slice(h * HEAD_DIM_A, (h + 1) * HEAD_DIM_A)
        ot_ref[rows, :] = ot_ref[rows, :] / d_ref[h:h + 1, :]
    out_ref[...] = ot_ref[...].T


def _dsa_attn(qat, qit, wt, ka, vat, kib, q_norm_a, k_norm_a, batch, seq):
    tq = min(TQ, seq)
    topk = min(TOPK_MAX, seq // 4)
    qk_bound = HEAD_DIM_A ** 0.5 * jnp.max(jnp.abs(q_norm_a)) * jnp.max(jnp.abs(k_norm_a)) * 1.01
    direct = (qk_bound <= SOFTMAX_DIRECT_BOUND).astype(jnp.int32).reshape(1)
    kern = functools.partial(_dsa_attn_kernel, seq=seq, topk=topk)
    q_spec = lambda rows: pl.BlockSpec((None, rows, tq), lambda b, i: (b, 0, i))
    return pl.pallas_call(
        kern,
        grid=(batch, seq // tq),
        in_specs=[
            q_spec(WIDTH_A), q_spec(WIDTH_A), q_spec(N_IDX_HEADS),
            pl.BlockSpec((None, seq, WIDTH_A), lambda b, i: (b, 0, 0)),
            pl.BlockSpec((None, WIDTH_A, seq), lambda b, i: (b, 0, 0)),
            pl.BlockSpec((None, seq, 128), lambda b, i: (b, 0, 0)),
            pl.BlockSpec(memory_space=pltpu.SMEM),
        ],
        out_specs=pl.BlockSpec((None, tq, WIDTH_A), lambda b, i: (b, i, 0)),
        out_shape=jax.ShapeDtypeStruct((batch, seq, WIDTH_A), F32),
        scratch_shapes=[
            pltpu.VMEM((seq, tq), F32),
            pltpu.VMEM((WIDTH_A, tq), F32),
            pltpu.VMEM((N_IDX_HEADS, 128, tq), BF16),
            pltpu.VMEM((N_HEADS_A // 2, 128, 2 * tq), BF16),
            pltpu.VMEM((tq, tq), BF16),
            pltpu.VMEM((N_HEADS_A, tq), F32),
            pltpu.VMEM((N_HEADS_A, tq), F32),
            pltpu.VMEM((tq, N_HEADS_A * tq), F32),
            pltpu.VMEM((tq, N_HEADS_A * tq), F32),
            pltpu.VMEM((seq // tq, 8, tq), F32),
            pltpu.VMEM((seq // tq, 8, tq), F32),
        ],
        compiler_params=pltpu.CompilerParams(
            dimension_semantics=("arbitrary", "arbitrary"), vmem_limit_bytes=VMEM_LIMIT_BYTES),
        name="dsa_attn",
    )(qat, qit, wt, ka.reshape(batch, seq, WIDTH_A), vat, kib.reshape(batch, seq, 128), direct)


_M_BG, _M_CG, _M_HB, _M_ZB, _M_QM, _M_ZM, _M_G = (0, 512, 1024, 1536, 2048, 2560, 3072)
_M_COLS = 3072 + N_BRANCHES * D_MODEL


def _merge_kernel(x_ref, attn_ref, g_ref, wza_ref, wc_ref, bg_ref, cw_ref, mk_ref, mv_ref,
                  gqm_ref, wa_ref, wb_ref, wm_ref, wo_ref, out_ref, utail_ref, *, per_b):
    tm = x_ref.shape[0]
    xf = x_ref[...]
    xn = (_rms_rows(xf) * g_ref[...]).astype(BF16)

    def proj(lhs, c0, width):
        return _dot(lhs, wc_ref[:, c0:c0 + width])

    za = _dot(xn, wza_ref[...])
    ya = _dot((attn_ref[...] * jax.nn.silu(za)).astype(BF16), wa_ref[...])

    u = proj(xn, _M_CG, WIDTH_B) * proj(xn, _M_HB, WIDTH_B)
    @pl.when((pl.program_id(0) % per_b) == 0)
    def _():
        utail_ref[...] = jnp.zeros((8, WIDTH_B), F32)

    uh = utail_ref[...]
    utail_ref[...] = u[tm - 8:, :]
    rows = lax.broadcasted_iota(jnp.int32, (tm, WIDTH_B), 0)
    u1 = jnp.where(rows == 0, uh[7:8, :], pltpu.roll(u, 1, 0))
    u2 = jnp.where(rows == 0, uh[6:7, :], jnp.where(rows == 1, uh[7:8, :], pltpu.roll(u, 2, 0)))
    conv = cw_ref[0:1, :] * u2 + cw_ref[1:2, :] * u1 + cw_ref[2:3, :] * u
    bgate = proj(xn, _M_BG, WIDTH_B)
    zb = proj(xn, _M_ZB, WIDTH_B)
    yb = _dot(((bgate * conv) * jax.nn.silu(zb)).astype(BF16), wb_ref[...])

    qm = proj(xn, _M_QM, WIDTH_M)
    zm = proj(xn, _M_ZM, WIDTH_M)
    heads = []
    for h in range(N_HEADS_M):
        sl = slice(h * HEAD_DIM_M, (h + 1) * HEAD_DIM_M)
        qh = (_rms_rows(qm[:, sl]) * gqm_ref[...]).astype(BF16)
        lg = _dot_nt(qh, mk_ref[:, sl]) * (HEAD_DIM_M ** -0.5)
        e = jnp.exp(lg - jnp.max(lg, axis=-1, keepdims=True))
        p = e / jnp.sum(e, axis=-1, keepdims=True)
        heads.append(_dot(p.astype(BF16), mv_ref[:, sl]))
    attn_m = jnp.concatenate(heads, axis=1)
    ym = _dot((attn_m * jax.nn.silu(zm)).astype(BF16), wm_ref[...])

    def gate(j):
        gj = proj(xn, _M_G + j * D_MODEL, D_MODEL) + bg_ref[:, j * D_MODEL:(j + 1) * D_MODEL]
        return jax.nn.sigmoid(gj)

    merged = gate(0) * ya + gate(1) * yb + gate(2) * ym
    out_ref[...] = xf + _dot(merged.astype(BF16), wo_ref[...])


def _merge(x2, attn2, mk, mv, norm_g, w_in, b_gate, conv_w, q_norm_m,
           w_out_a, w_out_b, w_out_m, w_o, batch, seq):
    n = x2.shape[0]
    tm = min(TM_C, seq)
    per_b = seq // tm
    m_len = mk.shape[1]
    wza = w_in[:, _C_ZA:_C_ZA + WIDTH_A].astype(BF16)
    wc = w_in[:, _C_REST:].astype(BF16)
    const = lambda shape: pl.BlockSpec(shape, lambda i: (0,) * len(shape),
                                       pipeline_mode=pl.Buffered(1))
    kern = functools.partial(_merge_kernel, per_b=per_b)
    return pl.pallas_call(
        kern,
        grid=(n // tm,),
        in_specs=[
            pl.BlockSpec((tm, D_MODEL), lambda i: (i, 0)),
            pl.BlockSpec((tm, WIDTH_A), lambda i: (i, 0)),
            const((1, D_MODEL)), const((D_MODEL, WIDTH_A)), const((D_MODEL, _M_COLS)),
            const((1, N_BRANCHES * D_MODEL)),
            const((CONV_WIDTH, WIDTH_B)),
            pl.BlockSpec((None, m_len, WIDTH_M), lambda i: (i // per_b, 0, 0)),
            pl.BlockSpec((None, m_len, WIDTH_M), lambda i: (i // per_b, 0, 0)),
            const((1, HEAD_DIM_M)),
            const((WIDTH_A, D_MODEL)), const((WIDTH_B, D_MODEL)), const((WIDTH_M, D_MODEL)),
            const((D_MODEL, D_MODEL)),
        ],
        out_specs=pl.BlockSpec((tm, D_MODEL), lambda i: (i, 0)),
        out_shape=jax.ShapeDtypeStruct((n, D_MODEL), F32),
        scratch_shapes=[pltpu.VMEM((8, WIDTH_B), F32)],
        compiler_params=pltpu.CompilerParams(
            dimension_semantics=("arbitrary",), vmem_limit_bytes=VMEM_LIMIT_BYTES),
        name="merge",
    )(x2, attn2, norm_g.reshape(1, D_MODEL), wza, wc, b_gate.reshape(1, -1), conv_w,
      mk, mv, q_norm_m.reshape(1, HEAD_DIM_M),
      w_out_a.astype(BF16), w_out_b.astype(BF16), w_out_m.astype(BF16), w_o.astype(BF16))


def _layer(h, mem, norm_g, mem_norm_g, w_in, b_gate, w_mem_kv, q_norm_a, k_norm_a,
           q_norm_m, k_norm_m, conv_w, w_out_a, w_out_b, w_out_m, w_o):
    batch, seq, _ = h.shape
    x2 = h.reshape(batch * seq, D_MODEL)
    qat, ka, vat, qit, kib, wt = _proj_a(x2, norm_g, w_in, q_norm_a, k_norm_a, batch, seq)
    mk, mv = _mem_kv(mem, mem_norm_g, w_mem_kv, k_norm_m)
    attn = _dsa_attn(qat, qit, wt, ka, vat, kib, q_norm_a, k_norm_a, batch, seq)
    out = _merge(x2, attn.reshape(batch * seq, WIDTH_A), mk, mv, norm_g, w_in, b_gate, conv_w,
                 q_norm_m, w_out_a, w_out_b, w_out_m, w_o, batch, seq)
    return out.reshape(batch, seq, D_MODEL)


def kernel(x, mem, norm_g, mem_norm_g, w_in, b_gate, w_mem_kv, q_norm_a, k_norm_a,
           q_norm_m, k_norm_m, conv_w, w_out_a, w_out_b, w_out_m, w_o):
    h = x
    for l in range(norm_g.shape[0]):
        h = _layer(h, mem, norm_g[l], mem_norm_g[l], w_in[l], b_gate[l], w_mem_kv[l],
                   q_norm_a[l], k_norm_a[l], q_norm_m[l], k_norm_m[l], conv_w[l],
                   w_out_a[l], w_out_b[l], w_out_m[l], w_o[l])
    return h
```

```python
import functools

import jax
import jax.numpy as jnp
import numpy as np
from jax import lax
from jax.experimental import pallas as pl
from jax.experimental.pallas import tpu as pltpu

F32 = jnp.float32
BF16 = jnp.bfloat16

D_MODEL = 1024
N_HEADS_A = 8
HEAD_DIM_A = 64
WIDTH_A = 512
N_IDX_HEADS = 8
IDX_DIM = 64
TOPK_MAX = 256
WIDTH_B = 512
CONV_WIDTH = 3
N_HEADS_M = 4
HEAD_DIM_M = 128
WIDTH_M = 512
N_BRANCHES = 3
RMS_EPS = 1e-6

_C_QA, _C_KA, _C_VA, _C_ZA = 0, 512, 1024, 1536
_C_QI, _C_KI = 2048, 2560
_C_REST = 2632

VMEM_LIMIT_BYTES = 56 * 1024 * 1024

TM_A = 512
TM_C = 512
TQ = 256
FOLD_ROWS = 32
BISECT_STEPS = 19
SOFTMAX_DIRECT_BOUND = 60.0
SOFTMAX_M_INIT = -1e30


def _rms_rows(xf, eps=RMS_EPS):
    return xf * lax.rsqrt(jnp.mean(xf * xf, axis=-1, keepdims=True) + eps)


def _dot(a, b):
    return jnp.dot(a, b, preferred_element_type=F32)


def _dot_nt(a, b):
    return lax.dot_general(a, b, (((1,), (1,)), ((), ())), preferred_element_type=F32)


def _proj_a_kernel(x_ref, g_ref, w1_ref, w2t_ref, gsum_ref, gk_ref, gq_ref,
                   qat_ref, ka_ref, vat_ref, qit_ref, kib_ref, wt_ref):
    tm = x_ref.shape[0]
    xn = (_rms_rows(x_ref[...]) * g_ref[...]).astype(BF16)

    y1 = _dot(xn, w1_ref[...])
    ka_raw = y1[:, :WIDTH_A]
    sq = ka_raw * ka_raw
    sq_hi = sq.astype(BF16)
    sq_lo = (sq - sq_hi.astype(F32)).astype(BF16)
    ss = _dot(sq_hi, gsum_ref[...]) + _dot(sq_lo, gsum_ref[...])
    ka = ka_raw * lax.rsqrt(ss * (1.0 / HEAD_DIM_A) + RMS_EPS) * gk_ref[...]
    ka_ref[...] = ka.astype(BF16)
    kiwi = y1[:, WIDTH_A:]
    lane = lax.broadcasted_iota(jnp.int32, kiwi.shape, 1)
    kib_ref[...] = jnp.where(lane < IDX_DIM, kiwi, 0.0).astype(BF16)
    wt_ref[...] = kiwi.T[IDX_DIM:IDX_DIM + N_IDX_HEADS, :]

    yt = _dot_nt(w2t_ref[...], xn)
    gq = jnp.concatenate([gq_ref[...]] * (tm // 128), axis=1)
    for h in range(N_HEADS_A):
        qh = yt[h * HEAD_DIM_A:(h + 1) * HEAD_DIM_A, :]
        ms = jnp.mean(qh * qh, axis=0, keepdims=True)
        qn = qh * lax.rsqrt(ms + RMS_EPS) * gq
        qat_ref[h * HEAD_DIM_A:(h + 1) * HEAD_DIM_A, :] = (qn * (HEAD_DIM_A ** -0.5)).astype(BF16)
    vat_ref[...] = yt[WIDTH_A:2 * WIDTH_A, :].astype(BF16)
    qit_ref[...] = yt[2 * WIDTH_A:3 * WIDTH_A, :].astype(BF16)


def _proj_a(x2, norm_g, w_in, q_norm_a, k_norm_a, batch, seq):
    n = x2.shape[0]
    tm = min(TM_A, seq)
    per_b = seq // tm
    n_kiwi = IDX_DIM + N_IDX_HEADS
    w1 = jnp.concatenate(
        [w_in[:, _C_KA:_C_KA + WIDTH_A], w_in[:, _C_KI:_C_KI + n_kiwi],
         jnp.zeros((D_MODEL, 128 - n_kiwi), F32)], axis=1).astype(BF16)
    w2t = jnp.concatenate(
        [w_in[:, _C_QA:_C_QA + WIDTH_A], w_in[:, _C_VA:_C_VA + WIDTH_A],
         w_in[:, _C_QI:_C_QI + WIDTH_A]], axis=1).T.astype(BF16)
    hid = np.arange(WIDTH_A) // HEAD_DIM_A
    gsum = jnp.asarray(hid[:, None] == hid[None, :], BF16)
    gk = jnp.tile(k_norm_a.reshape(1, HEAD_DIM_A), (1, N_HEADS_A))
    gq = jnp.broadcast_to(q_norm_a.reshape(HEAD_DIM_A, 1), (HEAD_DIM_A, 128))

    full = lambda shape: pl.BlockSpec(shape, lambda i: (0,) * len(shape))
    t_spec = lambda rows: pl.BlockSpec((None, rows, tm), lambda i: (i // per_b, 0, i % per_b))
    return pl.pallas_call(
        _proj_a_kernel,
        grid=(n // tm,),
        in_specs=[
            pl.BlockSpec((tm, D_MODEL), lambda i: (i, 0)),
            full((1, D_MODEL)), full((D_MODEL, WIDTH_A + 128)), full((3 * WIDTH_A, D_MODEL)),
            full((WIDTH_A, WIDTH_A)), full((1, WIDTH_A)), full((HEAD_DIM_A, 128)),
        ],
        out_specs=[
            t_spec(WIDTH_A),
            pl.BlockSpec((tm, WIDTH_A), lambda i: (i, 0)),
            t_spec(WIDTH_A), t_spec(WIDTH_A),
            pl.BlockSpec((tm, 128), lambda i: (i, 0)),
            t_spec(N_IDX_HEADS),
        ],
        out_shape=[
            jax.ShapeDtypeStruct((batch, WIDTH_A, seq), BF16),
            jax.ShapeDtypeStruct((n, WIDTH_A), BF16),
            jax.ShapeDtypeStruct((batch, WIDTH_A, seq), BF16),
            jax.ShapeDtypeStruct((batch, WIDTH_A, seq), BF16),
            jax.ShapeDtypeStruct((n, 128), BF16),
            jax.ShapeDtypeStruct((batch, N_IDX_HEADS, seq), F32),
        ],
        compiler_params=pltpu.CompilerParams(
            dimension_semantics=("arbitrary",), vmem_limit_bytes=VMEM_LIMIT_BYTES),
        name="proj_a",
    )(x2, norm_g.reshape(1, D_MODEL), w1, w2t, gsum, gk, gq)


def _mem_kv_kernel(mem_ref, g_ref, w_ref, gk_ref, mk_ref, mv_ref):
    mn = (_rms_rows(mem_ref[...]) * g_ref[...]).astype(BF16)
    y = _dot(mn, w_ref[...])
    for h in range(N_HEADS_M):
        kh = y[:, h * HEAD_DIM_M:(h + 1) * HEAD_DIM_M]
        mk_ref[:, h * HEAD_DIM_M:(h + 1) * HEAD_DIM_M] = (_rms_rows(kh) * gk_ref[...]).astype(BF16)
    mv_ref[...] = y[:, WIDTH_M:].astype(BF16)


def _mem_kv(mem, mem_norm_g, w_mem_kv, k_norm_m):
    batch, m_len, _ = mem.shape
    full = lambda shape: pl.BlockSpec(shape, lambda b: (0,) * len(shape))
    return pl.pallas_call(
        _mem_kv_kernel,
        grid=(batch,),
        in_specs=[
            pl.BlockSpec((None, m_len, D_MODEL), lambda b: (b, 0, 0)),
            full((1, D_MODEL)), full((D_MODEL, 2 * WIDTH_M)), full((1, HEAD_DIM_M)),
        ],
        out_specs=[pl.BlockSpec((None, m_len, WIDTH_M), lambda b: (b, 0, 0))] * 2,
        out_shape=[jax.ShapeDtypeStruct((batch, m_len, WIDTH_M), BF16)] * 2,
        compiler_params=pltpu.CompilerParams(
            dimension_semantics=("arbitrary",), vmem_limit_bytes=VMEM_LIMIT_BYTES),
        name="mem_kv",
    )(mem, mem_norm_g.reshape(1, D_MODEL), w_mem_kv.astype(BF16), k_norm_m.reshape(1, HEAD_DIM_M))


def _alibi_slope(h):
    return float(2.0 ** (-8.0 * (h + 1) / N_HEADS_A))


def _dsa_attn_kernel(qat_ref, qit_ref, wt_ref, ka_ref, vat_ref, kib_ref, direct_ref, out_ref,
                     sc_ref, ot_ref, rhsi_ref, rhsa_ref, tri_ref, m_ref, d_ref, lga_ref, lgb_ref,
                     zc_ref, zb_ref, *, seq, topk):
    tq = qat_ref.shape[1]
    ck = tq
    i = pl.program_id(1)
    t0 = i * tq
    nk = i + 1
    inf = jnp.float32(jnp.inf)
    kf = jnp.float32(topk)
    t_row = t0 + lax.broadcasted_iota(jnp.int32, (1, tq), 1)

    def fold(w, op):
        return op(w.reshape(ck // FOLD_ROWS, FOLD_ROWS, w.shape[1]), axis=0)

    def over_chunks(body, init):
        def step(c, carry):
            r0 = pl.multiple_of(c * ck, ck)
            return body(sc_ref[pl.ds(r0, ck), :], r0, carry)
        return lax.fori_loop(0, nk, step, init)

    def col(acc, op):
        return op(acc, axis=0, keepdims=True)

    facc = lambda v: jnp.full((FOLD_ROWS, tq), v, F32)

    zero_rows = jnp.zeros((128 - IDX_DIM, tq), BF16)
    for h in range(N_IDX_HEADS):
        rhsi_ref[h] = jnp.concatenate(
            [qit_ref[h * IDX_DIM:(h + 1) * IDX_DIM, :], zero_rows], axis=0)
    idx_scale = (IDX_DIM ** -0.5) * (N_IDX_HEADS ** -0.5)

    n_ahead = N_IDX_HEADS // 2

    def dots_ahead(kc):
        for h in range(n_ahead):
            lgb_ref[:, h * tq:(h + 1) * tq] = _dot(kc, rhsi_ref[h])

    dots_ahead(kib_ref[0:ck, :])

    row = lax.broadcasted_iota(jnp.int32, (128, tq), 0)
    for p in range(N_HEADS_A // 2):
        qpair = qat_ref[p * 128:(p + 1) * 128, :]
        rhsa_ref[p] = jnp.concatenate(
            [jnp.where(row < HEAD_DIM_A, qpair, jnp.zeros_like(qpair)),
             jnp.where(row >= HEAD_DIM_A, qpair, jnp.zeros_like(qpair))], axis=1)
    lg0_ref = lga_ref.at[:, 0:2 * tq]
    lg0_ref[...] = _dot(ka_ref[0:ck, 0:128], rhsa_ref[0])

    def score_one(c, carry, diagonal):
        lo_a, hi_a, pos_a, nonneg_a = carry
        r0 = pl.multiple_of(c * ck, ck)
        kc = kib_ref[pl.ds(r0, ck), :]
        acc = jnp.zeros((ck, tq), F32)
        for h in range(n_ahead):
            acc = acc + jnp.maximum(lgb_ref[:, h * tq:(h + 1) * tq], 0.0) * wt_ref[h:h + 1, :]
        for h in range(n_ahead, N_IDX_HEADS):
            d = _dot(kc, rhsi_ref[h])
            acc = acc + jnp.maximum(d, 0.0) * wt_ref[h:h + 1, :]
        acc = acc * idx_scale
        if diagonal:
            causal = r0 + lax.broadcasted_iota(jnp.int32, (ck, tq), 0) <= t_row
            masked = jnp.where(causal, acc, -inf)
            floor_in = jnp.where(causal, acc, inf)
        else:
            dots_ahead(kib_ref[pl.ds(pl.multiple_of((c + 1) * ck, ck), ck), :])
            masked = floor_in = acc
        sc_ref[pl.ds(r0, ck), :] = masked
        pos_c = fold(jnp.where(masked > 0.0, 1.0, 0.0), jnp.sum)
        nonneg_c = fold(jnp.where(masked >= 0.0, 1.0, 0.0), jnp.sum)
        zc_ref[c] = jnp.broadcast_to(col(nonneg_c - pos_c, jnp.sum), (8, tq))
        return (jnp.minimum(lo_a, fold(floor_in, jnp.min)),
                jnp.maximum(hi_a, fold(masked, jnp.max)),
                pos_a + pos_c, nonneg_a + nonneg_c)

    odd_full = (nk - 1) % 2
    stats0 = (facc(inf), facc(-inf), facc(0.0), facc(0.0))
    stats0 = lax.cond(odd_full == 1, lambda st: score_one(0, st, False), lambda st: st, stats0)

    def score_pair(cc, carry):
        c0 = odd_full + 2 * cc
        return score_one(c0 + 1, score_one(c0, carry, False), False)

    stats = lax.fori_loop(0, (nk - 1) // 2, score_pair, stats0)
    lo_a, hi_a, pos_a, nonneg_a = score_one(nk - 1, stats, True)
    odd = nk % 2

    def count_ge(v):
        acc = over_chunks(lambda x, r0, a: a + fold(jnp.where(x >= v, 1.0, 0.0), jnp.sum), facc(0.0))
        return col(acc, jnp.sum)

    lo_min = col(lo_a, jnp.min)
    hi_max = col(hi_a, jnp.max)
    n_causal = (t_row + 1).astype(F32)
    short = n_causal < kf

    def bis_step(lo, hi, cl):
        mid = lo * 0.5 + hi * 0.5
        c = count_ge(mid)
        ok = c >= kf
        return jnp.where(ok, mid, lo), jnp.where(ok, hi, mid), jnp.where(ok, c, cl)

    n_pos = col(pos_a, jnp.sum)
    n_nonneg = col(nonneg_a, jnp.sum)
    above_zero = n_nonneg >= kf
    lo0 = jnp.where(above_zero, 0.0, lo_min)
    hi0 = jnp.where(above_zero, hi_max, 0.0)
    cl0 = jnp.where(short, kf, jnp.where(above_zero, n_nonneg, n_causal))
    lo, _, cl = lax.fori_loop(0, BISECT_STEPS, lambda _, s: bis_step(*s), (lo0, hi0, cl0))

    def walk_cond(st):
        return jnp.max(st[3]) > 0.0

    def walk_body(st):
        lo, thr, take, todo = st
        lo_e = col(over_chunks(
            lambda x, r0, a: jnp.minimum(a, fold(jnp.where(x >= lo, x, inf), jnp.min)), facc(inf)), jnp.min)

        def above(x, r0, carry):
            cgt_a, cge_a, nxt_a = carry
            gt = x > lo_e
            return (cgt_a + fold(jnp.where(gt, 1.0, 0.0), jnp.sum),
                    cge_a + fold(jnp.where(x >= lo_e, 1.0, 0.0), jnp.sum),
                    jnp.minimum(nxt_a, fold(jnp.where(gt, x, inf), jnp.min)))

        cgt_a, cge_a, nxt_a = over_chunks(above, (facc(0.0), facc(0.0), facc(inf)))
        cgt, cge, nxt = col(cgt_a, jnp.sum), col(cge_a, jnp.sum), col(nxt_a, jnp.min)
        active = todo > 0.0
        fin = active & (cgt < kf)
        thr = jnp.where(fin, lo_e, thr)
        take = jnp.where(fin & (cge > kf), kf - cgt, take)
        lo = jnp.where(active & (cgt >= kf), nxt, lo)
        todo = jnp.where(fin, 0.0, todo)
        return lo, thr, take, todo

    zero_tie = jnp.logical_not(short) & (n_pos < kf) & above_zero
    thr0 = jnp.where(zero_tie, 0.0, jnp.where(short, lo_min, lo))
    take0 = jnp.where(zero_tie, kf - n_pos, float(seq))
    todo0 = jnp.where(zero_tie | (cl == kf), 0.0, 1.0)
    _, thr, take, _ = lax.while_loop(walk_cond, walk_body, (lo, thr0, take0, todo0))

    def alibi_dist(r0):
        s_idx = r0 + lax.broadcasted_iota(jnp.int32, (ck, tq), 0)
        return (t_row - s_idx).astype(F32)

    tied = take < float(seq)
    far = jnp.float32(seq)

    def locate(c, carry):
        zeros_before, cut_chunk = carry
        zb_ref[c] = jnp.broadcast_to(zeros_before, (8, tq))
        zeros_upto = zeros_before + zc_ref[c][0:1, :]
        hit = zero_tie & (cut_chunk >= far) & (zeros_upto >= take)
        return zeros_upto, jnp.where(hit, c.astype(F32), cut_chunk)

    _, cut_chunk = lax.fori_loop(0, nk, locate, (jnp.zeros((1, tq), F32), jnp.full((1, tq), far)))
    other_tie = jnp.max(jnp.where(tied & jnp.logical_not(zero_tie), 1.0, 0.0)) > 0.0
    first_cut = jnp.min(cut_chunk).astype(jnp.int32)
    last_cut = jnp.max(jnp.where(cut_chunk < far, cut_chunk, -1.0)).astype(jnp.int32)
    ca = jnp.where(other_tie, 0, jnp.minimum(first_cut, nk))
    cb = jnp.where(other_tie, nk - 1, last_cut)
    ties_at_ca = jnp.where(other_tie, 0.0, zb_ref[jnp.minimum(ca, nk - 1)][0:1, :])

    @pl.when((pl.program_id(0) == 0) & (i == 0))
    def _():
        r_i = lax.broadcasted_iota(jnp.int32, (ck, ck), 0)
        c_i = lax.broadcasted_iota(jnp.int32, (ck, ck), 1)
        tri_ref[...] = jnp.where(c_i <= r_i, 1.0, 0.0).astype(BF16)

    def keep_all(c, dmin_a):
        r0 = pl.multiple_of(c * ck, ck)
        x = sc_ref[pl.ds(r0, ck), :]
        pen = jnp.where(x >= thr, alibi_dist(r0), inf)
        sc_ref[pl.ds(r0, ck), :] = pen
        return jnp.minimum(dmin_a, fold(pen, jnp.min))

    def ranked(c, carry):
        ties_before, dmin_a = carry
        r0 = pl.multiple_of(c * ck, ck)
        x = sc_ref[pl.ds(r0, ck), :]
        dist = alibi_dist(r0)
        eq = x == thr
        rank = _dot(tri_ref[...], jnp.where(eq, 1.0, 0.0).astype(BF16)) + ties_before
        tie = jnp.where(eq, jnp.where(rank <= take, dist, inf), inf)
        pen = jnp.where(x > thr, dist, tie)
        sc_ref[pl.ds(r0, ck), :] = pen
        return rank[ck - 1:ck, :], jnp.minimum(dmin_a, fold(pen, jnp.min))

    def keep_untied(c, dmin_a):
        r0 = pl.multiple_of(c * ck, ck)
        x = sc_ref[pl.ds(r0, ck), :]
        dist = alibi_dist(r0)
        tie = jnp.where(x == thr, jnp.where(tied, inf, dist), inf)
        pen = jnp.where(x > thr, dist, tie)
        sc_ref[pl.ds(r0, ck), :] = pen
        return jnp.minimum(dmin_a, fold(pen, jnp.min))

    dmin_a = lax.fori_loop(0, ca, keep_all, facc(inf))
    _, dmin_a = lax.fori_loop(ca, cb + 1, ranked, (ties_at_ca, dmin_a))
    dmin_a = lax.fori_loop(jnp.maximum(cb + 1, ca), nk, keep_untied, dmin_a)

    dmin = col(dmin_a, jnp.min)

    m_ref[...] = jnp.full((N_HEADS_A, tq), SOFTMAX_M_INIT, F32)
    d_ref[...] = jnp.zeros((N_HEADS_A, tq), F32)
    ot_ref[...] = jnp.zeros((WIDTH_A, tq), F32)

    def k_chunk(c):
        r0 = pl.multiple_of(c * ck, ck)
        return [ka_ref[pl.ds(r0, ck), p * 128:(p + 1) * 128] for p in range(N_HEADS_A // 2)]

    def qk_pair(buf, p, kc):
        buf[:, p * 2 * tq:(p + 1) * 2 * tq] = _dot(kc, rhsa_ref[p])

    def qk_into(buf, kcs):
        for p in range(N_HEADS_A // 2):
            qk_pair(buf, p, kcs[p])

    def step(cur, nxt, c, c_next):
        r0 = pl.multiple_of(c * ck, ck)
        pen = sc_ref[pl.ds(r0, ck), :] - dmin
        vts = [vat_ref[h * HEAD_DIM_A:(h + 1) * HEAD_DIM_A, pl.ds(r0, ck)]
               for h in range(N_HEADS_A)]
        kcs = k_chunk(c_next)
        qk_pair(nxt, 0, kcs[0])
        for p in range(N_HEADS_A // 2):
            m_seen = soft_pv(cur, pen, vts, (2 * p, 2 * p + 1))
            if p + 1 < N_HEADS_A // 2:
                tie = jnp.where(m_seen != m_seen, 1.0, 0.0).astype(BF16)
                qk_pair(nxt, p + 1, kcs[p + 1] + tie)

    def soft_pv(buf, pen, vts, heads):
        m_seen = None
        for h in heads:
            rows = slice(h * HEAD_DIM_A, (h + 1) * HEAD_DIM_A)
            alphas, probs = [], []
            for lt in range(tq // 128):
                ln = slice(lt * 128, (lt + 1) * 128)
                m = m_ref[h:h + 1, ln]
                lg = buf[:, h * tq + lt * 128:h * tq + (lt + 1) * 128] - _alibi_slope(h) * pen[:, ln]
                m_new = jnp.maximum(m, col(fold(lg, jnp.max), jnp.max))
                alpha = jnp.exp(m - m_new)
                pr = jnp.exp(lg - m_new)
                m_ref[h:h + 1, ln] = m_new
                d_ref[h:h + 1, ln] = d_ref[h:h + 1, ln] * alpha + col(fold(pr, jnp.sum), jnp.sum)
                alphas.append(alpha)
                probs.append(pr.astype(BF16))
                m_seen = m_new if m_seen is None else m_seen
            ot_ref[rows, :] = (ot_ref[rows, :] * jnp.concatenate(alphas, axis=1)
                               + _dot(vts[h], jnp.concatenate(probs, axis=1)))
        return m_seen

    last = nk - 1
    def direct_one(c):
        r0 = pl.multiple_of(c * ck, ck)
        pen = sc_ref[pl.ds(r0, ck), :] - dmin
        vts = [vat_ref[h * HEAD_DIM_A:(h + 1) * HEAD_DIM_A, pl.ds(r0, ck)]
               for h in range(N_HEADS_A)]
        kcs = k_chunk(c)
        k_next = ka_ref[pl.ds(pl.multiple_of(jnp.minimum(c + 1, last) * ck, ck), ck), 0:128]
        lg2 = lg0_ref[...]
        for p in range(N_HEADS_A // 2):
            lg2_next = None
            for j in range(2):
                h = 2 * p + j
                rows = slice(h * HEAD_DIM_A, (h + 1) * HEAD_DIM_A)
                probs = []
                for lt in range(tq // 128):
                    ln = slice(lt * 128, (lt + 1) * 128)
                    e = jnp.exp(lg2[:, j * tq + lt * 128:j * tq + (lt + 1) * 128]
                                - _alibi_slope(h) * pen[:, ln])
                    esum = col(fold(e, jnp.sum), jnp.sum)
                    d_ref[h:h + 1, ln] = d_ref[h:h + 1, ln] + esum
                    probs.append(e.astype(BF16))
                    if j == 0 and lt == 0:
                        if p + 1 < N_HEADS_A // 2:
                            lg2_next = _dot(kcs[p + 1], rhsa_ref[p + 1])
                        if p + 2 == N_HEADS_A // 2:
                            lg0_ref[...] = _dot(k_next, rhsa_ref[0])
                ot_ref[rows, :] = ot_ref[rows, :] + _dot(vts[h], jnp.concatenate(probs, axis=1))
            lg2 = lg2_next

    @pl.when(direct_ref[0] > 0)
    def _():
        @pl.when(odd == 1)
        def _():
            direct_one(0)

        def direct_pair(cc, carry):
            c0 = odd + 2 * cc
            direct_one(c0)
            direct_one(c0 + 1)
            return carry

        lax.fori_loop(0, nk // 2, direct_pair, 0)

    @pl.when(direct_ref[0] <= 0)
    def _():
        qk_into(lga_ref, k_chunk(0))

        def attn_pair(cc, carry):
            c0 = 2 * cc
            step(lga_ref, lgb_ref, c0, jnp.minimum(c0 + 1, last))

            @pl.when(c0 + 1 < nk)
            def _():
                step(lgb_ref, lga_ref, c0 + 1, jnp.minimum(c0 + 2, last))

            return carry

        lax.fori_loop(0, (nk + 1) // 2, attn_pair, 0)

    for h in range(N_HEADS_A):
        rows = slice(h * HEAD_DIM_A, (h + 1) * HEAD_DIM_A)
        ot_ref[rows, :] = ot_ref[rows, :] / d_ref[h:h + 1, :]
    out_ref[...] = ot_ref[...].T


def _dsa_attn(qat, qit, wt, ka, vat, kib, q_norm_a, k_norm_a, batch, seq):
    tq = min(TQ, seq)
    topk = min(TOPK_MAX, seq // 4)
    qk_bound = HEAD_DIM_A ** 0.5 * jnp.max(jnp.abs(q_norm_a)) * jnp.max(jnp.abs(k_norm_a)) * 1.01
    direct = (qk_bound <= SOFTMAX_DIRECT_BOUND).astype(jnp.int32).reshape(1)
    kern = functools.partial(_dsa_attn_kernel, seq=seq, topk=topk)
    q_spec = lambda rows: pl.BlockSpec((None, rows, tq), lambda b, i: (b, 0, i))
    return pl.pallas_call(
        kern,
        grid=(batch, seq // tq),
        in_specs=[
            q_spec(WIDTH_A), q_spec(WIDTH_A), q_spec(N_IDX_HEADS),
            pl.BlockSpec((None, seq, WIDTH_A), lambda b, i: (b, 0, 0)),
            pl.BlockSpec((None, WIDTH_A, seq), lambda b, i: (b, 0, 0)),
            pl.BlockSpec((None, seq, 128), lambda b, i: (b, 0, 0)),
            pl.BlockSpec(memory_space=pltpu.SMEM),
        ],
        out_specs=pl.BlockSpec((None, tq, WIDTH_A), lambda b, i: (b, i, 0)),
        out_shape=jax.ShapeDtypeStruct((batch, seq, WIDTH_A), F32),
        scratch_shapes=[
            pltpu.VMEM((seq, tq), F32),
            pltpu.VMEM((WIDTH_A, tq), F32),
            pltpu.VMEM((N_IDX_HEADS, 128, tq), BF16),
            pltpu.VMEM((N_HEADS_A // 2, 128, 2 * tq), BF16),
            pltpu.VMEM((tq, tq), BF16),
            pltpu.VMEM((N_HEADS_A, tq), F32),
            pltpu.VMEM((N_HEADS_A, tq), F32),
            pltpu.VMEM((tq, N_HEADS_A * tq), F32),
            pltpu.VMEM((tq, N_HEADS_A * tq), F32),
            pltpu.VMEM((seq // tq, 8, tq), F32),
            pltpu.VMEM((seq // tq, 8, tq), F32),
        ],
        compiler_params=pltpu.CompilerParams(
            dimension_semantics=("arbitrary", "arbitrary"), vmem_limit_bytes=VMEM_LIMIT_BYTES),
        name="dsa_attn",
    )(qat, qit, wt, ka.reshape(batch, seq, WIDTH_A), vat, kib.reshape(batch, seq, 128), direct)


_M_BG, _M_CG, _M_HB, _M_ZB, _M_QM, _M_ZM, _M_G = (0, 512, 1024, 1536, 2048, 2560, 3072)
_M_COLS = 3072 + N_BRANCHES * D_MODEL


def _merge_kernel(x_ref, attn_ref, g_ref, wza_ref, wc_ref, bg_ref, cw_ref, mk_ref, mv_ref,
                  gqm_ref, wa_ref, wb_ref, wm_ref, wo_ref, out_ref, utail_ref, *, per_b):
    tm = x_ref.shape[0]
    xf = x_ref[...]
    xn = (_rms_rows(xf) * g_ref[...]).astype(BF16)

    def proj(lhs, c0, width):
        return _dot(lhs, wc_ref[:, c0:c0 + width])

    za = _dot(xn, wza_ref[...])
    ya = _dot((attn_ref[...] * jax.nn.silu(za)).astype(BF16), wa_ref[...])

    u = proj(xn, _M_CG, WIDTH_B) * proj(xn, _M_HB, WIDTH_B)
    @pl.when((pl.program_id(0) % per_b) == 0)
    def _():
        utail_ref[...] = jnp.zeros((8, WIDTH_B), F32)

    uh = utail_ref[...]
    utail_ref[...] = u[tm - 8:, :]
    rows = lax.broadcasted_iota(jnp.int32, (tm, WIDTH_B), 0)
    u1 = jnp.where(rows == 0, uh[7:8, :], pltpu.roll(u, 1, 0))
    u2 = jnp.where(rows == 0, uh[6:7, :], jnp.where(rows == 1, uh[7:8, :], pltpu.roll(u, 2, 0)))
    conv = cw_ref[0:1, :] * u2 + cw_ref[1:2, :] * u1 + cw_ref[2:3, :] * u
    bgate = proj(xn, _M_BG, WIDTH_B)
    zb = proj(xn, _M_ZB, WIDTH_B)
    yb = _dot(((bgate * conv) * jax.nn.silu(zb)).astype(BF16), wb_ref[...])

    qm = proj(xn, _M_QM, WIDTH_M)
    zm = proj(xn, _M_ZM, WIDTH_M)
    heads = []
    for h in range(N_HEADS_M):
        sl = slice(h * HEAD_DIM_M, (h + 1) * HEAD_DIM_M)
        qh = (_rms_rows(qm[:, sl]) * gqm_ref[...]).astype(BF16)
        lg = _dot_nt(qh, mk_ref[:, sl]) * (HEAD_DIM_M ** -0.5)
        e = jnp.exp(lg - jnp.max(lg, axis=-1, keepdims=True))
        p = e / jnp.sum(e, axis=-1, keepdims=True)
        heads.append(_dot(p.astype(BF16), mv_ref[:, sl]))
    attn_m = jnp.concatenate(heads, axis=1)
    ym = _dot((attn_m * jax.nn.silu(zm)).astype(BF16), wm_ref[...])

    def gate(j):
        gj = proj(xn, _M_G + j * D_MODEL, D_MODEL) + bg_ref[:, j * D_MODEL:(j + 1) * D_MODEL]
        return jax.nn.sigmoid(gj)

    merged = gate(0) * ya + gate(1) * yb + gate(2) * ym
    out_ref[...] = xf + _dot(merged.astype(BF16), wo_ref[...])


def _merge(x2, attn2, mk, mv, norm_g, w_in, b_gate, conv_w, q_norm_m,
           w_out_a, w_out_b, w_out_m, w_o, batch, seq):
    n = x2.shape[0]
    tm = min(TM_C, seq)
    per_b = seq // tm
    m_len = mk.shape[1]
    wza = w_in[:, _C_ZA:_C_ZA + WIDTH_A].astype(BF16)
    wc = w_in[:, _C_REST:].astype(BF16)
    const = lambda shape: pl.BlockSpec(shape, lambda i: (0,) * len(shape),
                                       pipeline_mode=pl.Buffered(1))
    kern = functools.partial(_merge_kernel, per_b=per_b)
    return pl.pallas_call(
        kern,
        grid=(n // tm,),
        in_specs=[
            pl.BlockSpec((tm, D_MODEL), lambda i: (i, 0)),
            pl.BlockSpec((tm, WIDTH_A), lambda i: (i, 0)),
            const((1, D_MODEL)), const((D_MODEL, WIDTH_A)), const((D_MODEL, _M_COLS)),
            const((1, N_BRANCHES * D_MODEL)),
            const((CONV_WIDTH, WIDTH_B)),
            pl.BlockSpec((None, m_len, WIDTH_M), lambda i: (i // per_b, 0, 0)),
            pl.BlockSpec((None, m_len, WIDTH_M), lambda i: (i // per_b, 0, 0)),
            const((1, HEAD_DIM_M)),
            const((WIDTH_A, D_MODEL)), const((WIDTH_B, D_MODEL)), const((WIDTH_M, D_MODEL)),
            const((D_MODEL, D_MODEL)),
        ],
        out_specs=pl.BlockSpec((tm, D_MODEL), lambda i: (i, 0)),
        out_shape=jax.ShapeDtypeStruct((n, D_MODEL), F32),
        scratch_shapes=[pltpu.VMEM((8, WIDTH_B), F32)],
        compiler_params=pltpu.CompilerParams(
            dimension_semantics=("arbitrary",), vmem_limit_bytes=VMEM_LIMIT_BYTES),
        name="merge",
    )(x2, attn2, norm_g.reshape(1, D_MODEL), wza, wc, b_gate.reshape(1, -1), conv_w,
      mk, mv, q_norm_m.reshape(1, HEAD_DIM_M),
      w_out_a.astype(BF16), w_out_b.astype(BF16), w_out_m.astype(BF16), w_o.astype(BF16))


def _layer(h, mem, norm_g, mem_norm_g, w_in, b_gate, w_mem_kv, q_norm_a, k_norm_a,
           q_norm_m, k_norm_m, conv_w, w_out_a, w_out_b, w_out_m, w_o):
    batch, seq, _ = h.shape
    x2 = h.reshape(batch * seq, D_MODEL)
    qat, ka, vat, qit, kib, wt = _proj_a(x2, norm_g, w_in, q_norm_a, k_norm_a, batch, seq)
    mk, mv = _mem_kv(mem, mem_norm_g, w_mem_kv, k_norm_m)
    attn = _dsa_attn(qat, qit, wt, ka, vat, kib, q_norm_a, k_norm_a, batch, seq)
    out = _merge(x2, attn.reshape(batch * seq, WIDTH_A), mk, mv, norm_g, w_in, b_gate, conv_w,
                 q_norm_m, w_out_a, w_out_b, w_out_m, w_o, batch, seq)
    return out.reshape(batch, seq, D_MODEL)


def kernel(x, mem, norm_g, mem_norm_g, w_in, b_gate, w_mem_kv, q_norm_a, k_norm_a,
           q_norm_m, k_norm_m, conv_w, w_out_a, w_out_b, w_out_m, w_o):
    h = x
    for l in range(norm_g.shape[0]):
        h = _layer(h, mem, norm_g[l], mem_norm_g[l], w_in[l], b_gate[l], w_mem_kv[l],
                   q_norm_a[l], k_norm_a[l], q_norm_m[l], k_norm_m[l], conv_w[l],
                   w_out_a[l], w_out_b[l], w_out_m[l], w_o[l])
    return h
```

```python
import functools

import jax
import jax.numpy as jnp
import numpy as np
from jax import lax
from jax.experimental import pallas as pl
from jax.experimental.pallas import tpu as pltpu

F32 = jnp.float32
BF16 = jnp.bfloat16

D_MODEL = 1024
N_HEADS_A = 8
HEAD_DIM_A = 64
WIDTH_A = 512
N_IDX_HEADS = 8
IDX_DIM = 64
TOPK_MAX = 256
WIDTH_B = 512
CONV_WIDTH = 3
N_HEADS_M = 4
HEAD_DIM_M = 128
WIDTH_M = 512
N_BRANCHES = 3
RMS_EPS = 1e-6

_C_QA, _C_KA, _C_VA, _C_ZA = 0, 512, 1024, 1536
_C_QI, _C_KI = 2048, 2560
_C_REST = 2632

VMEM_LIMIT_BYTES = 56 * 1024 * 1024

TM_A = 512
TM_C = 512
TQ = 256
FOLD_ROWS = 32
BISECT_STEPS = 20
SOFTMAX_DIRECT_BOUND = 60.0
SOFTMAX_M_INIT = -1e30


def _rms_rows(xf, eps=RMS_EPS):
    return xf * lax.rsqrt(jnp.mean(xf * xf, axis=-1, keepdims=True) + eps)


def _dot(a, b):
    return jnp.dot(a, b, preferred_element_type=F32)


def _dot_nt(a, b):
    return lax.dot_general(a, b, (((1,), (1,)), ((), ())), preferred_element_type=F32)


def _proj_a_kernel(x_ref, g_ref, w1_ref, w2t_ref, gsum_ref, gk_ref, gq_ref,
                   qat_ref, ka_ref, vat_ref, qit_ref, kib_ref, wt_ref):
    tm = x_ref.shape[0]
    xn = (_rms_rows(x_ref[...]) * g_ref[...]).astype(BF16)

    y1 = _dot(xn, w1_ref[...])
    ka_raw = y1[:, :WIDTH_A]
    sq = ka_raw * ka_raw
    sq_hi = sq.astype(BF16)
    sq_lo = (sq - sq_hi.astype(F32)).astype(BF16)
    ss = _dot(sq_hi, gsum_ref[...]) + _dot(sq_lo, gsum_ref[...])
    ka = ka_raw * lax.rsqrt(ss * (1.0 / HEAD_DIM_A) + RMS_EPS) * gk_ref[...]
    ka_ref[...] = ka.astype(BF16)
    kiwi = y1[:, WIDTH_A:]
    lane = lax.broadcasted_iota(jnp.int32, kiwi.shape, 1)
    kib_ref[...] = jnp.where(lane < IDX_DIM, kiwi, 0.0).astype(BF16)
    wt_ref[...] = kiwi.T[IDX_DIM:IDX_DIM + N_IDX_HEADS, :]

    yt = _dot_nt(w2t_ref[...], xn)
    gq = jnp.concatenate([gq_ref[...]] * (tm // 128), axis=1)
    for h in range(N_HEADS_A):
        qh = yt[h * HEAD_DIM_A:(h + 1) * HEAD_DIM_A, :]
        ms = jnp.mean(qh * qh, axis=0, keepdims=True)
        qn = qh * lax.rsqrt(ms + RMS_EPS) * gq
        qat_ref[h * HEAD_DIM_A:(h + 1) * HEAD_DIM_A, :] = (qn * (HEAD_DIM_A ** -0.5)).astype(BF16)
    vat_ref[...] = yt[WIDTH_A:2 * WIDTH_A, :].astype(BF16)
    qit_ref[...] = yt[2 * WIDTH_A:3 * WIDTH_A, :].astype(BF16)


def _proj_a(x2, norm_g, w_in, q_norm_a, k_norm_a, batch, seq):
    n = x2.shape[0]
    tm = min(TM_A, seq)
    per_b = seq // tm
    n_kiwi = IDX_DIM + N_IDX_HEADS
    w1 = jnp.concatenate(
        [w_in[:, _C_KA:_C_KA + WIDTH_A], w_in[:, _C_KI:_C_KI + n_kiwi],
         jnp.zeros((D_MODEL, 128 - n_kiwi), F32)], axis=1).astype(BF16)
    w2t = jnp.concatenate(
        [w_in[:, _C_QA:_C_QA + WIDTH_A], w_in[:, _C_VA:_C_VA + WIDTH_A],
         w_in[:, _C_QI:_C_QI + WIDTH_A]], axis=1).T.astype(BF16)
    hid = np.arange(WIDTH_A) // HEAD_DIM_A
    gsum = jnp.asarray(hid[:, None] == hid[None, :], BF16)
    gk = jnp.tile(k_norm_a.reshape(1, HEAD_DIM_A), (1, N_HEADS_A))
    gq = jnp.broadcast_to(q_norm_a.reshape(HEAD_DIM_A, 1), (HEAD_DIM_A, 128))

    full = lambda shape: pl.BlockSpec(shape, lambda i: (0,) * len(shape))
    t_spec = lambda rows: pl.BlockSpec((None, rows, tm), lambda i: (i // per_b, 0, i % per_b))
    return pl.pallas_call(
        _proj_a_kernel,
        grid=(n // tm,),
        in_specs=[
            pl.BlockSpec((tm, D_MODEL), lambda i: (i, 0)),
            full((1, D_MODEL)), full((D_MODEL, WIDTH_A + 128)), full((3 * WIDTH_A, D_MODEL)),
            full((WIDTH_A, WIDTH_A)), full((1, WIDTH_A)), full((HEAD_DIM_A, 128)),
        ],
        out_specs=[
            t_spec(WIDTH_A),
            pl.BlockSpec((tm, WIDTH_A), lambda i: (i, 0)),
            t_spec(WIDTH_A), t_spec(WIDTH_A),
            pl.BlockSpec((tm, 128), lambda i: (i, 0)),
            t_spec(N_IDX_HEADS),
        ],
        out_shape=[
            jax.ShapeDtypeStruct((batch, WIDTH_A, seq), BF16),
            jax.ShapeDtypeStruct((n, WIDTH_A), BF16),
            jax.ShapeDtypeStruct((batch, WIDTH_A, seq), BF16),
            jax.ShapeDtypeStruct((batch, WIDTH_A, seq), BF16),
            jax.ShapeDtypeStruct((n, 128), BF16),
            jax.ShapeDtypeStruct((batch, N_IDX_HEADS, seq), F32),
        ],
        compiler_params=pltpu.CompilerParams(
            dimension_semantics=("arbitrary",), vmem_limit_bytes=VMEM_LIMIT_BYTES),
        name="proj_a",
    )(x2, norm_g.reshape(1, D_MODEL), w1, w2t, gsum, gk, gq)


def _mem_kv_kernel(mem_ref, g_ref, w_ref, gk_ref, mk_ref, mv_ref):
    mn = (_rms_rows(mem_ref[...]) * g_ref[...]).astype(BF16)
    y = _dot(mn, w_ref[...])
    for h in range(N_HEADS_M):
        kh = y[:, h * HEAD_DIM_M:(h + 1) * HEAD_DIM_M]
        mk_ref[:, h * HEAD_DIM_M:(h + 1) * HEAD_DIM_M] = (_rms_rows(kh) * gk_ref[...]).astype(BF16)
    mv_ref[...] = y[:, WIDTH_M:].astype(BF16)


def _mem_kv(mem, mem_norm_g, w_mem_kv, k_norm_m):
    batch, m_len, _ = mem.shape
    full = lambda shape: pl.BlockSpec(shape, lambda b: (0,) * len(shape))
    return pl.pallas_call(
        _mem_kv_kernel,
        grid=(batch,),
        in_specs=[
            pl.BlockSpec((None, m_len, D_MODEL), lambda b: (b, 0, 0)),
            full((1, D_MODEL)), full((D_MODEL, 2 * WIDTH_M)), full((1, HEAD_DIM_M)),
        ],
        out_specs=[pl.BlockSpec((None, m_len, WIDTH_M), lambda b: (b, 0, 0))] * 2,
        out_shape=[jax.ShapeDtypeStruct((batch, m_len, WIDTH_M), BF16)] * 2,
        compiler_params=pltpu.CompilerParams(
            dimension_semantics=("arbitrary",), vmem_limit_bytes=VMEM_LIMIT_BYTES),
        name="mem_kv",
    )(mem, mem_norm_g.reshape(1, D_MODEL), w_mem_kv.astype(BF16), k_norm_m.reshape(1, HEAD_DIM_M))


def _alibi_slope(h):
    return float(2.0 ** (-8.0 * (h + 1) / N_HEADS_A))


def _dsa_attn_kernel(qat_ref, qit_ref, wt_ref, ka_ref, vat_ref, kib_ref, direct_ref, out_ref,
                     sc_ref, ot_ref, rhsi_ref, rhsa_ref, tri_ref, m_ref, d_ref, lga_ref, lgb_ref,
                     zc_ref, zb_ref, *, seq, topk):
    tq = qat_ref.shape[1]
    ck = tq
    i = pl.program_id(1)
    t0 = i * tq
    nk = i + 1
    inf = jnp.float32(jnp.inf)
    kf = jnp.float32(topk)
    t_row = t0 + lax.broadcasted_iota(jnp.int32, (1, tq), 1)

    def fold(w, op):
        return op(w.reshape(ck // FOLD_ROWS, FOLD_ROWS, w.shape[1]), axis=0)

    def over_chunks(body, init):
        def step(c, carry):
            r0 = pl.multiple_of(c * ck, ck)
            return body(sc_ref[pl.ds(r0, ck), :], r0, carry)
        return lax.fori_loop(0, nk, step, init)

    def col(acc, op):
        return op(acc, axis=0, keepdims=True)

    facc = lambda v: jnp.full((FOLD_ROWS, tq), v, F32)

    zero_rows = jnp.zeros((128 - IDX_DIM, tq), BF16)
    for h in range(N_IDX_HEADS):
        rhsi_ref[h] = jnp.concatenate(
            [qit_ref[h * IDX_DIM:(h + 1) * IDX_DIM, :], zero_rows], axis=0)
    idx_scale = (IDX_DIM ** -0.5) * (N_IDX_HEADS ** -0.5)

    n_ahead = N_IDX_HEADS // 2

    def dots_ahead(kc):
        for h in range(n_ahead):
            lgb_ref[:, h * tq:(h + 1) * tq] = _dot(kc, rhsi_ref[h])

    dots_ahead(kib_ref[0:ck, :])

    row = lax.broadcasted_iota(jnp.int32, (128, tq), 0)
    for p in range(N_HEADS_A // 2):
        qpair = qat_ref[p * 128:(p + 1) * 128, :]
        rhsa_ref[p] = jnp.concatenate(
            [jnp.where(row < HEAD_DIM_A, qpair, jnp.zeros_like(qpair)),
             jnp.where(row >= HEAD_DIM_A, qpair, jnp.zeros_like(qpair))], axis=1)
    lg0_ref = lga_ref.at[:, 0:2 * tq]
    lg0_ref[...] = _dot(ka_ref[0:ck, 0:128], rhsa_ref[0])

    def score_one(c, carry, diagonal):
        lo_a, hi_a, pos_a, nonneg_a = carry
        r0 = pl.multiple_of(c * ck, ck)
        kc = kib_ref[pl.ds(r0, ck), :]
        acc = jnp.zeros((ck, tq), F32)
        for h in range(n_ahead):
            acc = acc + jnp.maximum(lgb_ref[:, h * tq:(h + 1) * tq], 0.0) * wt_ref[h:h + 1, :]
        for h in range(n_ahead, N_IDX_HEADS):
            d = _dot(kc, rhsi_ref[h])
            acc = acc + jnp.maximum(d, 0.0) * wt_ref[h:h + 1, :]
        acc = acc * idx_scale
        if diagonal:
            causal = r0 + lax.broadcasted_iota(jnp.int32, (ck, tq), 0) <= t_row
            masked = jnp.where(causal, acc, -inf)
            floor_in = jnp.where(causal, acc, inf)
        else:
            dots_ahead(kib_ref[pl.ds(pl.multiple_of((c + 1) * ck, ck), ck), :])
            masked = floor_in = acc
        sc_ref[pl.ds(r0, ck), :] = masked
        pos_c = fold(jnp.where(masked > 0.0, 1.0, 0.0), jnp.sum)
        nonneg_c = fold(jnp.where(masked >= 0.0, 1.0, 0.0), jnp.sum)
        zc_ref[c] = jnp.broadcast_to(col(nonneg_c - pos_c, jnp.sum), (8, tq))
        return (jnp.minimum(lo_a, fold(floor_in, jnp.min)),
                jnp.maximum(hi_a, fold(masked, jnp.max)),
                pos_a + pos_c, nonneg_a + nonneg_c)

    odd_full = (nk - 1) % 2
    pair_full = ((nk - 1) // 2) % 2
    stats = (facc(inf), facc(-inf), facc(0.0), facc(0.0))
    stats = lax.cond(odd_full == 1, lambda st: score_one(0, st, False), lambda st: st, stats)
    stats = lax.cond(
        pair_full == 1,
        lambda st: score_one(odd_full + 1, score_one(odd_full, st, False), False),
        lambda st: st, stats)

    def score_quad(cc, carry):
        c0 = odd_full + 2 * pair_full + 4 * cc
        for j in range(4):
            carry = score_one(c0 + j, carry, False)
        return carry

    stats = lax.fori_loop(0, (nk - 1) // 4, score_quad, stats)
    lo_a, hi_a, pos_a, nonneg_a = score_one(nk - 1, stats, True)
    odd = nk % 2

    def count_ge(v):
        acc = over_chunks(lambda x, r0, a: a + fold(jnp.where(x >= v, 1.0, 0.0), jnp.sum), facc(0.0))
        return col(acc, jnp.sum)

    lo_min = col(lo_a, jnp.min)
    hi_max = col(hi_a, jnp.max)
    n_causal = (t_row + 1).astype(F32)
    short = n_causal < kf

    def bis_step(lo, hi, cl):
        mid = lo * 0.5 + hi * 0.5
        c = count_ge(mid)
        ok = c >= kf
        return jnp.where(ok, mid, lo), jnp.where(ok, hi, mid), jnp.where(ok, c, cl)

    n_pos = col(pos_a, jnp.sum)
    n_nonneg = col(nonneg_a, jnp.sum)
    above_zero = n_nonneg >= kf
    lo0 = jnp.where(above_zero, 0.0, lo_min)
    hi0 = jnp.where(above_zero, hi_max, 0.0)
    cl0 = jnp.where(short, kf, jnp.where(above_zero, n_nonneg, n_causal))
    lo, _, cl = lax.fori_loop(0, BISECT_STEPS, lambda _, s: bis_step(*s), (lo0, hi0, cl0))

    def walk_cond(st):
        return jnp.max(st[3]) > 0.0

    def walk_body(st):
        lo, thr, take, todo = st
        lo_e = col(over_chunks(
            lambda x, r0, a: jnp.minimum(a, fold(jnp.where(x >= lo, x, inf), jnp.min)), facc(inf)), jnp.min)

        def above(x, r0, carry):
            cgt_a, cge_a, nxt_a = carry
            gt = x > lo_e
            return (cgt_a + fold(jnp.where(gt, 1.0, 0.0), jnp.sum),
                    cge_a + fold(jnp.where(x >= lo_e, 1.0, 0.0), jnp.sum),
                    jnp.minimum(nxt_a, fold(jnp.where(gt, x, inf), jnp.min)))

        cgt_a, cge_a, nxt_a = over_chunks(above, (facc(0.0), facc(0.0), facc(inf)))
        cgt, cge, nxt = col(cgt_a, jnp.sum), col(cge_a, jnp.sum), col(nxt_a, jnp.min)
        active = todo > 0.0
        fin = active & (cgt < kf)
        thr = jnp.where(fin, lo_e, thr)
        take = jnp.where(fin & (cge > kf), kf - cgt, take)
        lo = jnp.where(active & (cgt >= kf), nxt, lo)
        todo = jnp.where(fin, 0.0, todo)
        return lo, thr, take, todo

    zero_tie = jnp.logical_not(short) & (n_pos < kf) & above_zero
    thr0 = jnp.where(zero_tie, 0.0, jnp.where(short, lo_min, lo))
    take0 = jnp.where(zero_tie, kf - n_pos, float(seq))
    todo0 = jnp.where(zero_tie | (cl == kf), 0.0, 1.0)
    _, thr, take, _ = lax.while_loop(walk_cond, walk_body, (lo, thr0, take0, todo0))

    def alibi_dist(r0):
        s_idx = r0 + lax.broadcasted_iota(jnp.int32, (ck, tq), 0)
        return (t_row - s_idx).astype(F32)

    tied = take < float(seq)
    far = jnp.float32(seq)

    def locate(c, carry):
        zeros_before, cut_chunk = carry
        zb_ref[c] = jnp.broadcast_to(zeros_before, (8, tq))
        zeros_upto = zeros_before + zc_ref[c][0:1, :]
        hit = zero_tie & (cut_chunk >= far) & (zeros_upto >= take)
        return zeros_upto, jnp.where(hit, c.astype(F32), cut_chunk)

    _, cut_chunk = lax.fori_loop(0, nk, locate, (jnp.zeros((1, tq), F32), jnp.full((1, tq), far)))
    other_tie = jnp.max(jnp.where(tied & jnp.logical_not(zero_tie), 1.0, 0.0)) > 0.0
    first_cut = jnp.min(cut_chunk).astype(jnp.int32)
    last_cut = jnp.max(jnp.where(cut_chunk < far, cut_chunk, -1.0)).astype(jnp.int32)
    ca = jnp.where(other_tie, 0, jnp.minimum(first_cut, nk))
    cb = jnp.where(other_tie, nk - 1, last_cut)
    ties_at_ca = jnp.where(other_tie, 0.0, zb_ref[jnp.minimum(ca, nk - 1)][0:1, :])

    @pl.when((pl.program_id(0) == 0) & (i == 0))
    def _():
        r_i = lax.broadcasted_iota(jnp.int32, (ck, ck), 0)
        c_i = lax.broadcasted_iota(jnp.int32, (ck, ck), 1)
        tri_ref[...] = jnp.where(c_i <= r_i, 1.0, 0.0).astype(BF16)

    def keep_all(c, dmin_a):
        r0 = pl.multiple_of(c * ck, ck)
        x = sc_ref[pl.ds(r0, ck), :]
        pen = jnp.where(x >= thr, alibi_dist(r0), inf)
        sc_ref[pl.ds(r0, ck), :] = pen
        return jnp.minimum(dmin_a, fold(pen, jnp.min))

    def ranked(c, carry):
        ties_before, dmin_a = carry
        r0 = pl.multiple_of(c * ck, ck)
        x = sc_ref[pl.ds(r0, ck), :]
        dist = alibi_dist(r0)
        eq = x == thr
        rank = _dot(tri_ref[...], jnp.where(eq, 1.0, 0.0).astype(BF16)) + ties_before
        tie = jnp.where(eq, jnp.where(rank <= take, dist, inf), inf)
        pen = jnp.where(x > thr, dist, tie)
        sc_ref[pl.ds(r0, ck), :] = pen
        return rank[ck - 1:ck, :], jnp.minimum(dmin_a, fold(pen, jnp.min))

    def keep_untied(c, dmin_a):
        r0 = pl.multiple_of(c * ck, ck)
        x = sc_ref[pl.ds(r0, ck), :]
        dist = alibi_dist(r0)
        tie = jnp.where(x == thr, jnp.where(tied, inf, dist), inf)
        pen = jnp.where(x > thr, dist, tie)
        sc_ref[pl.ds(r0, ck), :] = pen
        return jnp.minimum(dmin_a, fold(pen, jnp.min))

    dmin_a = lax.fori_loop(0, ca, keep_all, facc(inf))
    _, dmin_a = lax.fori_loop(ca, cb + 1, ranked, (ties_at_ca, dmin_a))
    dmin_a = lax.fori_loop(jnp.maximum(cb + 1, ca), nk, keep_untied, dmin_a)

    dmin = col(dmin_a, jnp.min)

    m_ref[...] = jnp.full((N_HEADS_A, tq), SOFTMAX_M_INIT, F32)
    d_ref[...] = jnp.zeros((N_HEADS_A, tq), F32)
    ot_ref[...] = jnp.zeros((WIDTH_A, tq), F32)

    def k_chunk(c):
        r0 = pl.multiple_of(c * ck, ck)
        return [ka_ref[pl.ds(r0, ck), p * 128:(p + 1) * 128] for p in range(N_HEADS_A // 2)]

    def qk_pair(buf, p, kc):
        buf[:, p * 2 * tq:(p + 1) * 2 * tq] = _dot(kc, rhsa_ref[p])

    def qk_into(buf, kcs):
        for p in range(N_HEADS_A // 2):
            qk_pair(buf, p, kcs[p])

    def step(cur, nxt, c, c_next):
        r0 = pl.multiple_of(c * ck, ck)
        pen = sc_ref[pl.ds(r0, ck), :] - dmin
        vts = [vat_ref[h * HEAD_DIM_A:(h + 1) * HEAD_DIM_A, pl.ds(r0, ck)]
               for h in range(N_HEADS_A)]
        kcs = k_chunk(c_next)
        qk_pair(nxt, 0, kcs[0])
        for p in range(N_HEADS_A // 2):
            m_seen = soft_pv(cur, pen, vts, (2 * p, 2 * p + 1))
            if p + 1 < N_HEADS_A // 2:
                tie = jnp.where(m_seen != m_seen, 1.0, 0.0).astype(BF16)
                qk_pair(nxt, p + 1, kcs[p + 1] + tie)

    def soft_pv(buf, pen, vts, heads):
        m_seen = None
        for h in heads:
            rows = slice(h * HEAD_DIM_A, (h + 1) * HEAD_DIM_A)
            alphas, probs = [], []
            for lt in range(tq // 128):
                ln = slice(lt * 128, (lt + 1) * 128)
                m = m_ref[h:h + 1, ln]
                lg = buf[:, h * tq + lt * 128:h * tq + (lt + 1) * 128] - _alibi_slope(h) * pen[:, ln]
                m_new = jnp.maximum(m, col(fold(lg, jnp.max), jnp.max))
                alpha = jnp.exp(m - m_new)
                pr = jnp.exp(lg - m_new)
                m_ref[h:h + 1, ln] = m_new
                d_ref[h:h + 1, ln] = d_ref[h:h + 1, ln] * alpha + col(fold(pr, jnp.sum), jnp.sum)
                alphas.append(alpha)
                probs.append(pr.astype(BF16))
                m_seen = m_new if m_seen is None else m_seen
            ot_ref[rows, :] = (ot_ref[rows, :] * jnp.concatenate(alphas, axis=1)
                               + _dot(vts[h], jnp.concatenate(probs, axis=1)))
        return m_seen

    last = nk - 1
    def direct_one(c):
        r0 = pl.multiple_of(c * ck, ck)
        pen = sc_ref[pl.ds(r0, ck), :] - dmin
        vts = [vat_ref[h * HEAD_DIM_A:(h + 1) * HEAD_DIM_A, pl.ds(r0, ck)]
               for h in range(N_HEADS_A)]
        kcs = k_chunk(c)
        k_next = ka_ref[pl.ds(pl.multiple_of(jnp.minimum(c + 1, last) * ck, ck), ck), 0:128]
        lg2 = lg0_ref[...]
        for p in range(N_HEADS_A // 2):
            lg2_next = None
            for j in range(2):
                h = 2 * p + j
                rows = slice(h * HEAD_DIM_A, (h + 1) * HEAD_DIM_A)
                probs = []
                for lt in range(tq // 128):
                    ln = slice(lt * 128, (lt + 1) * 128)
                    e = jnp.exp(lg2[:, j * tq + lt * 128:j * tq + (lt + 1) * 128]
                                - _alibi_slope(h) * pen[:, ln])
                    esum = col(fold(e, jnp.sum), jnp.sum)
                    d_ref[h:h + 1, ln] = d_ref[h:h + 1, ln] + esum
                    probs.append(e.astype(BF16))
                    if j == 0 and lt == 0:
                        if p + 1 < N_HEADS_A // 2:
                            lg2_next = _dot(kcs[p + 1], rhsa_ref[p + 1])
                        if p + 2 == N_HEADS_A // 2:
                            lg0_ref[...] = _dot(k_next, rhsa_ref[0])
                ot_ref[rows, :] = ot_ref[rows, :] + _dot(vts[h], jnp.concatenate(probs, axis=1))
            lg2 = lg2_next

    @pl.when(direct_ref[0] > 0)
    def _():
        @pl.when(odd == 1)
        def _():
            direct_one(0)

        pair = (nk // 2) % 2

        @pl.when(pair == 1)
        def _():
            direct_one(odd)
            direct_one(odd + 1)

        def direct_quad(cc, carry):
            c0 = odd + 2 * pair + 4 * cc
            for j in range(4):
                direct_one(c0 + j)
            return carry

        lax.fori_loop(0, nk // 4, direct_quad, 0)

    @pl.when(direct_ref[0] <= 0)
    def _():
        qk_into(lga_ref, k_chunk(0))

        def attn_pair(cc, carry):
            c0 = 2 * cc
            step(lga_ref, lgb_ref, c0, jnp.minimum(c0 + 1, last))

            @pl.when(c0 + 1 < nk)
            def _():
                step(lgb_ref, lga_ref, c0 + 1, jnp.minimum(c0 + 2, last))

            return carry

        lax.fori_loop(0, (nk + 1) // 2, attn_pair, 0)

    for h in range(N_HEADS_A):
        rows = slice(h * HEAD_DIM_A, (h + 1) * HEAD_DIM_A)
        ot_ref[rows, :] = ot_ref[rows, :] / d_ref[h:h + 1, :]
    out_ref[...] = ot_ref[...].T


def _dsa_attn(qat, qit, wt, ka, vat, kib, q_norm_a, k_norm_a, batch, seq):
    tq = min(TQ, seq)
    topk = min(TOPK_MAX, seq // 4)
    qk_bound = HEAD_DIM_A ** 0.5 * jnp.max(jnp.abs(q_norm_a)) * jnp.max(jnp.abs(k_norm_a)) * 1.01
    direct = (qk_bound <= SOFTMAX_DIRECT_BOUND).astype(jnp.int32).reshape(1)
    kern = functools.partial(_dsa_attn_kernel, seq=seq, topk=topk)
    q_spec = lambda rows: pl.BlockSpec((None, rows, tq), lambda b, i: (b, 0, i))
    return pl.pallas_call(
        kern,
        grid=(batch, seq // tq),
        in_specs=[
            q_spec(WIDTH_A), q_spec(WIDTH_A), q_spec(N_IDX_HEADS),
            pl.BlockSpec((None, seq, WIDTH_A), lambda b, i: (b, 0, 0)),
            pl.BlockSpec((None, WIDTH_A, seq), lambda b, i: (b, 0, 0)),
            pl.BlockSpec((None, seq, 128), lambda b, i: (b, 0, 0)),
            pl.BlockSpec(memory_space=pltpu.SMEM),
        ],
        out_specs=pl.BlockSpec((None, tq, WIDTH_A), lambda b, i: (b, i, 0)),
        out_shape=jax.ShapeDtypeStruct((batch, seq, WIDTH_A), F32),
        scratch_shapes=[
            pltpu.VMEM((seq, tq), F32),
            pltpu.VMEM((WIDTH_A, tq), F32),
            pltpu.VMEM((N_IDX_HEADS, 128, tq), BF16),
            pltpu.VMEM((N_HEADS_A // 2, 128, 2 * tq), BF16),
            pltpu.VMEM((tq, tq), BF16),
            pltpu.VMEM((N_HEADS_A, tq), F32),
            pltpu.VMEM((N_HEADS_A, tq), F32),
            pltpu.VMEM((tq, N_HEADS_A * tq), F32),
            pltpu.VMEM((tq, N_HEADS_A * tq), F32),
            pltpu.VMEM((seq // tq, 8, tq), F32),
            pltpu.VMEM((seq // tq, 8, tq), F32),
        ],
        compiler_params=pltpu.CompilerParams(
            dimension_semantics=("arbitrary", "arbitrary"), vmem_limit_bytes=VMEM_LIMIT_BYTES),
        name="dsa_attn",
    )(qat, qit, wt, ka.reshape(batch, seq, WIDTH_A), vat, kib.reshape(batch, seq, 128), direct)


_M_BG, _M_CG, _M_HB, _M_ZB, _M_QM, _M_ZM, _M_G = (0, 512, 1024, 1536, 2048, 2560, 3072)
_M_COLS = 3072 + N_BRANCHES * D_MODEL


def _merge_kernel(x_ref, attn_ref, g_ref, wza_ref, wc_ref, bg_ref, cw_ref, mk_ref, mv_ref,
                  gqm_ref, wa_ref, wb_ref, wm_ref, wo_ref, out_ref, utail_ref, *, per_b):
    tm = x_ref.shape[0]
    xf = x_ref[...]
    xn = (_rms_rows(xf) * g_ref[...]).astype(BF16)

    def proj(lhs, c0, width):
        return _dot(lhs, wc_ref[:, c0:c0 + width])

    za = _dot(xn, wza_ref[...])
    ya = _dot((attn_ref[...] * jax.nn.silu(za)).astype(BF16), wa_ref[...])

    u = proj(xn, _M_CG, WIDTH_B) * proj(xn, _M_HB, WIDTH_B)
    @pl.when((pl.program_id(0) % per_b) == 0)
    def _():
        utail_ref[...] = jnp.zeros((8, WIDTH_B), F32)

    uh = utail_ref[...]
    utail_ref[...] = u[tm - 8:, :]
    rows = lax.broadcasted_iota(jnp.int32, (tm, WIDTH_B), 0)
    u1 = jnp.where(rows == 0, uh[7:8, :], pltpu.roll(u, 1, 0))
    u2 = jnp.where(rows == 0, uh[6:7, :], jnp.where(rows == 1, uh[7:8, :], pltpu.roll(u, 2, 0)))
    conv = cw_ref[0:1, :] * u2 + cw_ref[1:2, :] * u1 + cw_ref[2:3, :] * u
    bgate = proj(xn, _M_BG, WIDTH_B)
    zb = proj(xn, _M_ZB, WIDTH_B)
    yb = _dot(((bgate * conv) * jax.nn.silu(zb)).astype(BF16), wb_ref[...])

    qm = proj(xn, _M_QM, WIDTH_M)
    zm = proj(xn, _M_ZM, WIDTH_M)
    heads = []
    for h in range(N_HEADS_M):
        sl = slice(h * HEAD_DIM_M, (h + 1) * HEAD_DIM_M)
        qh = (_rms_rows(qm[:, sl]) * gqm_ref[...]).astype(BF16)
        lg = _dot_nt(qh, mk_ref[:, sl]) * (HEAD_DIM_M ** -0.5)
        e = jnp.exp(lg - jnp.max(lg, axis=-1, keepdims=True))
        p = e / jnp.sum(e, axis=-1, keepdims=True)
        heads.append(_dot(p.astype(BF16), mv_ref[:, sl]))
    attn_m = jnp.concatenate(heads, axis=1)
    ym = _dot((attn_m * jax.nn.silu(zm)).astype(BF16), wm_ref[...])

    def gate(j):
        gj = proj(xn, _M_G + j * D_MODEL, D_MODEL) + bg_ref[:, j * D_MODEL:(j + 1) * D_MODEL]
        return jax.nn.sigmoid(gj)

    merged = gate(0) * ya + gate(1) * yb + gate(2) * ym
    out_ref[...] = xf + _dot(merged.astype(BF16), wo_ref[...])


def _merge(x2, attn2, mk, mv, norm_g, w_in, b_gate, conv_w, q_norm_m,
           w_out_a, w_out_b, w_out_m, w_o, batch, seq):
    n = x2.shape[0]
    tm = min(TM_C, seq)
    per_b = seq // tm
    m_len = mk.shape[1]
    wza = w_in[:, _C_ZA:_C_ZA + WIDTH_A].astype(BF16)
    wc = w_in[:, _C_REST:].astype(BF16)
    const = lambda shape: pl.BlockSpec(shape, lambda i: (0,) * len(shape),
                                       pipeline_mode=pl.Buffered(1))
    kern = functools.partial(_merge_kernel, per_b=per_b)
    return pl.pallas_call(
        kern,
        grid=(n // tm,),
        in_specs=[
            pl.BlockSpec((tm, D_MODEL), lambda i: (i, 0)),
            pl.BlockSpec((tm, WIDTH_A), lambda i: (i, 0)),
            const((1, D_MODEL)), const((D_MODEL, WIDTH_A)), const((D_MODEL, _M_COLS)),
            const((1, N_BRANCHES * D_MODEL)),
            const((CONV_WIDTH, WIDTH_B)),
            pl.BlockSpec((None, m_len, WIDTH_M), lambda i: (i // per_b, 0, 0)),
            pl.BlockSpec((None, m_len, WIDTH_M), lambda i: (i // per_b, 0, 0)),
            const((1, HEAD_DIM_M)),
            const((WIDTH_A, D_MODEL)), const((WIDTH_B, D_MODEL)), const((WIDTH_M, D_MODEL)),
            const((D_MODEL, D_MODEL)),
        ],
        out_specs=pl.BlockSpec((tm, D_MODEL), lambda i: (i, 0)),
        out_shape=jax.ShapeDtypeStruct((n, D_MODEL), F32),
        scratch_shapes=[pltpu.VMEM((8, WIDTH_B), F32)],
        compiler_params=pltpu.CompilerParams(
            dimension_semantics=("arbitrary",), vmem_limit_bytes=VMEM_LIMIT_BYTES),
        name="merge",
    )(x2, attn2, norm_g.reshape(1, D_MODEL), wza, wc, b_gate.reshape(1, -1), conv_w,
      mk, mv, q_norm_m.reshape(1, HEAD_DIM_M),
      w_out_a.astype(BF16), w_out_b.astype(BF16), w_out_m.astype(BF16), w_o.astype(BF16))


def _layer(h, mem, norm_g, mem_norm_g, w_in, b_gate, w_mem_kv, q_norm_a, k_norm_a,
           q_norm_m, k_norm_m, conv_w, w_out_a, w_out_b, w_out_m, w_o):
    batch, seq, _ = h.shape
    x2 = h.reshape(batch * seq, D_MODEL)
    qat, ka, vat, qit, kib, wt = _proj_a(x2, norm_g, w_in, q_norm_a, k_norm_a, batch, seq)
    mk, mv = _mem_kv(mem, mem_norm_g, w_mem_kv, k_norm_m)
    attn = _dsa_attn(qat, qit, wt, ka, vat, kib, q_norm_a, k_norm_a, batch, seq)
    out = _merge(x2, attn.reshape(batch * seq, WIDTH_A), mk, mv, norm_g, w_in, b_gate, conv_w,
                 q_norm_m, w_out_a, w_out_b, w_out_m, w_o, batch, seq)
    return out.reshape(batch, seq, D_MODEL)


def kernel(x, mem, norm_g, mem_norm_g, w_in, b_gate, w_mem_kv, q_norm_a, k_norm_a,
           q_norm_m, k_norm_m, conv_w, w_out_a, w_out_b, w_out_m, w_o):
    h = x
    for l in range(norm_g.shape[0]):
        h = _layer(h, mem, norm_g[l], mem_norm_g[l], w_in[l], b_gate[l], w_mem_kv[l],
                   q_norm_a[l], k_norm_a[l], q_norm_m[l], k_norm_m[l], conv_w[l],
                   w_out_a[l], w_out_b[l], w_out_m[l], w_o[l])
    return h
```

```python
import functools

import jax
import jax.numpy as jnp
import numpy as np
from jax import lax
from jax.experimental import pallas as pl
from jax.experimental.pallas import tpu as pltpu

F32 = jnp.float32
BF16 = jnp.bfloat16

D_MODEL = 1024
N_HEADS_A = 8
HEAD_DIM_A = 64
WIDTH_A = 512
N_IDX_HEADS = 8
IDX_DIM = 64
TOPK_MAX = 256
WIDTH_B = 512
CONV_WIDTH = 3
N_HEADS_M = 4
HEAD_DIM_M = 128
WIDTH_M = 512
N_BRANCHES = 3
RMS_EPS = 1e-6

_C_QA, _C_KA, _C_VA, _C_ZA = 0, 512, 1024, 1536
_C_QI, _C_KI = 2048, 2560
_C_REST = 2632

VMEM_LIMIT_BYTES = 56 * 1024 * 1024

TM_A = 512
TM_C = 512
TQ = 256
FOLD_ROWS = 32
BISECT_STEPS = 20
SOFTMAX_DIRECT_BOUND = 60.0
SOFTMAX_M_INIT = -1e30


def _rms_rows(xf, eps=RMS_EPS):
    return xf * lax.rsqrt(jnp.mean(xf * xf, axis=-1, keepdims=True) + eps)


def _dot(a, b):
    return jnp.dot(a, b, preferred_element_type=F32)


def _dot_nt(a, b):
    return lax.dot_general(a, b, (((1,), (1,)), ((), ())), preferred_element_type=F32)


def _proj_a_kernel(x_ref, g_ref, w1_ref, w2t_ref, gsum_ref, gk_ref, gq_ref,
                   qat_ref, ka_ref, vat_ref, qit_ref, kib_ref, wt_ref):
    tm = x_ref.shape[0]
    xn = (_rms_rows(x_ref[...]) * g_ref[...]).astype(BF16)

    y1 = _dot(xn, w1_ref[...])
    ka_raw = y1[:, :WIDTH_A]
    sq = ka_raw * ka_raw
    sq_hi = sq.astype(BF16)
    sq_lo = (sq - sq_hi.astype(F32)).astype(BF16)
    ss = _dot(sq_hi, gsum_ref[...]) + _dot(sq_lo, gsum_ref[...])
    ka = ka_raw * lax.rsqrt(ss * (1.0 / HEAD_DIM_A) + RMS_EPS) * gk_ref[...]
    ka_ref[...] = ka.astype(BF16)
    kiwi = y1[:, WIDTH_A:]
    lane = lax.broadcasted_iota(jnp.int32, kiwi.shape, 1)
    kib_ref[...] = jnp.where(lane < IDX_DIM, kiwi, 0.0).astype(BF16)
    wt_ref[...] = kiwi.T[IDX_DIM:IDX_DIM + N_IDX_HEADS, :]

    yt = _dot_nt(w2t_ref[...], xn)
    gq = jnp.concatenate([gq_ref[...]] * (tm // 128), axis=1)
    for h in range(N_HEADS_A):
        qh = yt[h * HEAD_DIM_A:(h + 1) * HEAD_DIM_A, :]
        ms = jnp.mean(qh * qh, axis=0, keepdims=True)
        qn = qh * lax.rsqrt(ms + RMS_EPS) * gq
        qat_ref[h * HEAD_DIM_A:(h + 1) * HEAD_DIM_A, :] = (qn * (HEAD_DIM_A ** -0.5)).astype(BF16)
    vat_ref[...] = yt[WIDTH_A:2 * WIDTH_A, :].astype(BF16)
    qit_ref[...] = yt[2 * WIDTH_A:3 * WIDTH_A, :].astype(BF16)


def _proj_a(x2, norm_g, w_in, q_norm_a, k_norm_a, batch, seq):
    n = x2.shape[0]
    tm = min(TM_A, seq)
    per_b = seq // tm
    n_kiwi = IDX_DIM + N_IDX_HEADS
    w1 = jnp.concatenate(
        [w_in[:, _C_KA:_C_KA + WIDTH_A], w_in[:, _C_KI:_C_KI + n_kiwi],
         jnp.zeros((D_MODEL, 128 - n_kiwi), F32)], axis=1).astype(BF16)
    w2t = jnp.concatenate(
        [w_in[:, _C_QA:_C_QA + WIDTH_A], w_in[:, _C_VA:_C_VA + WIDTH_A],
         w_in[:, _C_QI:_C_QI + WIDTH_A]], axis=1).T.astype(BF16)
    hid = np.arange(WIDTH_A) // HEAD_DIM_A
    gsum = jnp.asarray(hid[:, None] == hid[None, :], BF16)
    gk = jnp.tile(k_norm_a.reshape(1, HEAD_DIM_A), (1, N_HEADS_A))
    gq = jnp.broadcast_to(q_norm_a.reshape(HEAD_DIM_A, 1), (HEAD_DIM_A, 128))

    full = lambda shape: pl.BlockSpec(shape, lambda i: (0,) * len(shape))
    t_spec = lambda rows: pl.BlockSpec((None, rows, tm), lambda i: (i // per_b, 0, i % per_b))
    return pl.pallas_call(
        _proj_a_kernel,
        grid=(n // tm,),
        in_specs=[
            pl.BlockSpec((tm, D_MODEL), lambda i: (i, 0)),
            full((1, D_MODEL)), full((D_MODEL, WIDTH_A + 128)), full((3 * WIDTH_A, D_MODEL)),
            full((WIDTH_A, WIDTH_A)), full((1, WIDTH_A)), full((HEAD_DIM_A, 128)),
        ],
        out_specs=[
            t_spec(WIDTH_A),
            pl.BlockSpec((tm, WIDTH_A), lambda i: (i, 0)),
            t_spec(WIDTH_A), t_spec(WIDTH_A),
            pl.BlockSpec((tm, 128), lambda i: (i, 0)),
            t_spec(N_IDX_HEADS),
        ],
        out_shape=[
            jax.ShapeDtypeStruct((batch, WIDTH_A, seq), BF16),
            jax.ShapeDtypeStruct((n, WIDTH_A), BF16),
            jax.ShapeDtypeStruct((batch, WIDTH_A, seq), BF16),
            jax.ShapeDtypeStruct((batch, WIDTH_A, seq), BF16),
            jax.ShapeDtypeStruct((n, 128), BF16),
            jax.ShapeDtypeStruct((batch, N_IDX_HEADS, seq), F32),
        ],
        compiler_params=pltpu.CompilerParams(
            dimension_semantics=("arbitrary",), vmem_limit_bytes=VMEM_LIMIT_BYTES),
        name="proj_a",
    )(x2, norm_g.reshape(1, D_MODEL), w1, w2t, gsum, gk, gq)


def _mem_kv_kernel(mem_ref, g_ref, w_ref, gk_ref, mk_ref, mv_ref):
    mn = (_rms_rows(mem_ref[...]) * g_ref[...]).astype(BF16)
    y = _dot(mn, w_ref[...])
    for h in range(N_HEADS_M):
        kh = y[:, h * HEAD_DIM_M:(h + 1) * HEAD_DIM_M]
        mk_ref[:, h * HEAD_DIM_M:(h + 1) * HEAD_DIM_M] = (_rms_rows(kh) * gk_ref[...]).astype(BF16)
    mv_ref[...] = y[:, WIDTH_M:].astype(BF16)


def _mem_kv(mem, mem_norm_g, w_mem_kv, k_norm_m):
    batch, m_len, _ = mem.shape
    full = lambda shape: pl.BlockSpec(shape, lambda b: (0,) * len(shape))
    return pl.pallas_call(
        _mem_kv_kernel,
        grid=(batch,),
        in_specs=[
            pl.BlockSpec((None, m_len, D_MODEL), lambda b: (b, 0, 0)),
            full((1, D_MODEL)), full((D_MODEL, 2 * WIDTH_M)), full((1, HEAD_DIM_M)),
        ],
        out_specs=[pl.BlockSpec((None, m_len, WIDTH_M), lambda b: (b, 0, 0))] * 2,
        out_shape=[jax.ShapeDtypeStruct((batch, m_len, WIDTH_M), BF16)] * 2,
        compiler_params=pltpu.CompilerParams(
            dimension_semantics=("arbitrary",), vmem_limit_bytes=VMEM_LIMIT_BYTES),
        name="mem_kv",
    )(mem, mem_norm_g.reshape(1, D_MODEL), w_mem_kv.astype(BF16), k_norm_m.reshape(1, HEAD_DIM_M))


def _alibi_slope(h):
    return float(2.0 ** (-8.0 * (h + 1) / N_HEADS_A))


def _dsa_attn_kernel(qat_ref, qit_ref, wt_ref, ka_ref, vat_ref, kib_ref, direct_ref, out_ref,
                     sc_ref, ot_ref, rhsi_ref, rhsa_ref, tri_ref, m_ref, d_ref, lga_ref, lgb_ref,
                     zc_ref, zb_ref, *, seq, topk):
    tq = qat_ref.shape[1]
    ck = tq
    i = pl.program_id(1)
    t0 = i * tq
    nk = i + 1
    inf = jnp.float32(jnp.inf)
    kf = jnp.float32(topk)
    t_row = t0 + lax.broadcasted_iota(jnp.int32, (1, tq), 1)

    def fold(w, op):
        return op(w.reshape(ck // FOLD_ROWS, FOLD_ROWS, w.shape[1]), axis=0)

    def over_chunks(body, init):
        def step(c, carry):
            r0 = pl.multiple_of(c * ck, ck)
            return body(sc_ref[pl.ds(r0, ck), :], r0, carry)

        def quad(cc, carry):
            for j in range(4):
                carry = step(4 * cc + j, carry)
            return carry

        carry = lax.fori_loop(0, nk // 4, quad, init)
        return lax.fori_loop(4 * (nk // 4), nk, step, carry)

    def col(acc, op):
        return op(acc, axis=0, keepdims=True)

    facc = lambda v: jnp.full((FOLD_ROWS, tq), v, F32)

    zero_rows = jnp.zeros((128 - IDX_DIM, tq), BF16)
    for h in range(N_IDX_HEADS):
        rhsi_ref[h] = jnp.concatenate(
            [qit_ref[h * IDX_DIM:(h + 1) * IDX_DIM, :], zero_rows], axis=0)
    idx_scale = (IDX_DIM ** -0.5) * (N_IDX_HEADS ** -0.5)

    n_ahead = N_IDX_HEADS // 2

    def dots_ahead(kc):
        for h in range(n_ahead):
            lgb_ref[:, h * tq:(h + 1) * tq] = _dot(kc, rhsi_ref[h])

    dots_ahead(kib_ref[0:ck, :])

    row = lax.broadcasted_iota(jnp.int32, (128, tq), 0)
    for p in range(N_HEADS_A // 2):
        qpair = qat_ref[p * 128:(p + 1) * 128, :]
        rhsa_ref[p] = jnp.concatenate(
            [jnp.where(row < HEAD_DIM_A, qpair, jnp.zeros_like(qpair)),
             jnp.where(row >= HEAD_DIM_A, qpair, jnp.zeros_like(qpair))], axis=1)
    lg0_ref = lga_ref.at[:, 0:2 * tq]
    lg0_ref[...] = _dot(ka_ref[0:ck, 0:128], rhsa_ref[0])

    def score_one(c, carry, diagonal):
        lo_a, hi_a, pos_a, nonneg_a = carry
        r0 = pl.multiple_of(c * ck, ck)
        kc = kib_ref[pl.ds(r0, ck), :]
        acc = jnp.zeros((ck, tq), F32)
        for h in range(n_ahead):
            acc = acc + jnp.maximum(lgb_ref[:, h * tq:(h + 1) * tq], 0.0) * wt_ref[h:h + 1, :]
        for h in range(n_ahead, N_IDX_HEADS):
            d = _dot(kc, rhsi_ref[h])
            acc = acc + jnp.maximum(d, 0.0) * wt_ref[h:h + 1, :]
        acc = acc * idx_scale
        if diagonal:
            causal = r0 + lax.broadcasted_iota(jnp.int32, (ck, tq), 0) <= t_row
            masked = jnp.where(causal, acc, -inf)
            floor_in = jnp.where(causal, acc, inf)
        else:
            dots_ahead(kib_ref[pl.ds(pl.multiple_of((c + 1) * ck, ck), ck), :])
            masked = floor_in = acc
        sc_ref[pl.ds(r0, ck), :] = masked
        pos_c = fold(jnp.where(masked > 0.0, 1.0, 0.0), jnp.sum)
        nonneg_c = fold(jnp.where(masked >= 0.0, 1.0, 0.0), jnp.sum)
        zc_ref[c] = jnp.broadcast_to(col(nonneg_c - pos_c, jnp.sum), (8, tq))
        return (jnp.minimum(lo_a, fold(floor_in, jnp.min)),
                jnp.maximum(hi_a, fold(masked, jnp.max)),
                pos_a + pos_c, nonneg_a + nonneg_c)

    odd_full = (nk - 1) % 2
    pair_full = ((nk - 1) // 2) % 2
    stats = (facc(inf), facc(-inf), facc(0.0), facc(0.0))
    stats = lax.cond(odd_full == 1, lambda st: score_one(0, st, False), lambda st: st, stats)
    stats = lax.cond(
        pair_full == 1,
        lambda st: score_one(odd_full + 1, score_one(odd_full, st, False), False),
        lambda st: st, stats)

    def score_quad(cc, carry):
        c0 = odd_full + 2 * pair_full + 4 * cc
        for j in range(4):
            carry = score_one(c0 + j, carry, False)
        return carry

    stats = lax.fori_loop(0, (nk - 1) // 4, score_quad, stats)
    lo_a, hi_a, pos_a, nonneg_a = score_one(nk - 1, stats, True)
    odd = nk % 2

    def count_ge(v):
        acc = over_chunks(lambda x, r0, a: a + fold(jnp.where(x >= v, 1.0, 0.0), jnp.sum), facc(0.0))
        return col(acc, jnp.sum)

    lo_min = col(lo_a, jnp.min)
    hi_max = col(hi_a, jnp.max)
    n_causal = (t_row + 1).astype(F32)
    short = n_causal < kf

    def bis_step(lo, hi, cl):
        mid = lo * 0.5 + hi * 0.5
        c = count_ge(mid)
        ok = c >= kf
        return jnp.where(ok, mid, lo), jnp.where(ok, hi, mid), jnp.where(ok, c, cl)

    n_pos = col(pos_a, jnp.sum)
    n_nonneg = col(nonneg_a, jnp.sum)
    above_zero = n_nonneg >= kf
    lo0 = jnp.where(above_zero, 0.0, lo_min)
    hi0 = jnp.where(above_zero, hi_max, 0.0)
    cl0 = jnp.where(short, kf, jnp.where(above_zero, n_nonneg, n_causal))
    lo, _, cl = lax.fori_loop(0, BISECT_STEPS, lambda _, s: bis_step(*s), (lo0, hi0, cl0))

    def walk_cond(st):
        return jnp.max(st[3]) > 0.0

    def walk_body(st):
        lo, thr, take, todo = st
        lo_e = col(over_chunks(
            lambda x, r0, a: jnp.minimum(a, fold(jnp.where(x >= lo, x, inf), jnp.min)), facc(inf)), jnp.min)

        def above(x, r0, carry):
            cgt_a, cge_a, nxt_a = carry
            gt = x > lo_e
            return (cgt_a + fold(jnp.where(gt, 1.0, 0.0), jnp.sum),
                    cge_a + fold(jnp.where(x >= lo_e, 1.0, 0.0), jnp.sum),
                    jnp.minimum(nxt_a, fold(jnp.where(gt, x, inf), jnp.min)))

        cgt_a, cge_a, nxt_a = over_chunks(above, (facc(0.0), facc(0.0), facc(inf)))
        cgt, cge, nxt = col(cgt_a, jnp.sum), col(cge_a, jnp.sum), col(nxt_a, jnp.min)
        active = todo > 0.0
        fin = active & (cgt < kf)
        thr = jnp.where(fin, lo_e, thr)
        take = jnp.where(fin & (cge > kf), kf - cgt, take)
        lo = jnp.where(active & (cgt >= kf), nxt, lo)
        todo = jnp.where(fin, 0.0, todo)
        return lo, thr, take, todo

    zero_tie = jnp.logical_not(short) & (n_pos < kf) & above_zero
    thr0 = jnp.where(zero_tie, 0.0, jnp.where(short, lo_min, lo))
    take0 = jnp.where(zero_tie, kf - n_pos, float(seq))
    todo0 = jnp.where(zero_tie | (cl == kf), 0.0, 1.0)
    _, thr, take, _ = lax.while_loop(walk_cond, walk_body, (lo, thr0, take0, todo0))

    def alibi_dist(r0):
        s_idx = r0 + lax.broadcasted_iota(jnp.int32, (ck, tq), 0)
        return (t_row - s_idx).astype(F32)

    tied = take < float(seq)
    far = jnp.float32(seq)

    def locate(c, carry):
        zeros_before, cut_chunk = carry
        zb_ref[c] = jnp.broadcast_to(zeros_before, (8, tq))
        zeros_upto = zeros_before + zc_ref[c][0:1, :]
        hit = zero_tie & (cut_chunk >= far) & (zeros_upto >= take)
        return zeros_upto, jnp.where(hit, c.astype(F32), cut_chunk)

    _, cut_chunk = lax.fori_loop(0, nk, locate, (jnp.zeros((1, tq), F32), jnp.full((1, tq), far)))
    other_tie = jnp.max(jnp.where(tied & jnp.logical_not(zero_tie), 1.0, 0.0)) > 0.0
    first_cut = jnp.min(cut_chunk).astype(jnp.int32)
    last_cut = jnp.max(jnp.where(cut_chunk < far, cut_chunk, -1.0)).astype(jnp.int32)
    ca = jnp.where(other_tie, 0, jnp.minimum(first_cut, nk))
    cb = jnp.where(other_tie, nk - 1, last_cut)
    ties_at_ca = jnp.where(other_tie, 0.0, zb_ref[jnp.minimum(ca, nk - 1)][0:1, :])

    @pl.when((pl.program_id(0) == 0) & (i == 0))
    def _():
        r_i = lax.broadcasted_iota(jnp.int32, (ck, ck), 0)
        c_i = lax.broadcasted_iota(jnp.int32, (ck, ck), 1)
        tri_ref[...] = jnp.where(c_i <= r_i, 1.0, 0.0).astype(BF16)

    def keep_all(c, dmin_a):
        r0 = pl.multiple_of(c * ck, ck)
        x = sc_ref[pl.ds(r0, ck), :]
        pen = jnp.where(x >= thr, alibi_dist(r0), inf)
        sc_ref[pl.ds(r0, ck), :] = pen
        return jnp.minimum(dmin_a, fold(pen, jnp.min))

    def ranked(c, carry):
        ties_before, dmin_a = carry
        r0 = pl.multiple_of(c * ck, ck)
        x = sc_ref[pl.ds(r0, ck), :]
        dist = alibi_dist(r0)
        eq = x == thr
        rank = _dot(tri_ref[...], jnp.where(eq, 1.0, 0.0).astype(BF16)) + ties_before
        tie = jnp.where(eq, jnp.where(rank <= take, dist, inf), inf)
        pen = jnp.where(x > thr, dist, tie)
        sc_ref[pl.ds(r0, ck), :] = pen
        return rank[ck - 1:ck, :], jnp.minimum(dmin_a, fold(pen, jnp.min))

    def keep_untied(c, dmin_a):
        r0 = pl.multiple_of(c * ck, ck)
        x = sc_ref[pl.ds(r0, ck), :]
        dist = alibi_dist(r0)
        tie = jnp.where(x == thr, jnp.where(tied, inf, dist), inf)
        pen = jnp.where(x > thr, dist, tie)
        sc_ref[pl.ds(r0, ck), :] = pen
        return jnp.minimum(dmin_a, fold(pen, jnp.min))

    dmin_a = lax.fori_loop(0, ca, keep_all, facc(inf))
    _, dmin_a = lax.fori_loop(ca, cb + 1, ranked, (ties_at_ca, dmin_a))
    dmin_a = lax.fori_loop(jnp.maximum(cb + 1, ca), nk, keep_untied, dmin_a)

    dmin = col(dmin_a, jnp.min)

    m_ref[...] = jnp.full((N_HEADS_A, tq), SOFTMAX_M_INIT, F32)
    d_ref[...] = jnp.zeros((N_HEADS_A, tq), F32)
    ot_ref[...] = jnp.zeros((WIDTH_A, tq), F32)

    def k_chunk(c):
        r0 = pl.multiple_of(c * ck, ck)
        return [ka_ref[pl.ds(r0, ck), p * 128:(p + 1) * 128] for p in range(N_HEADS_A // 2)]

    def qk_pair(buf, p, kc):
        buf[:, p * 2 * tq:(p + 1) * 2 * tq] = _dot(kc, rhsa_ref[p])

    def qk_into(buf, kcs):
        for p in range(N_HEADS_A // 2):
            qk_pair(buf, p, kcs[p])

    def step(cur, nxt, c, c_next):
        r0 = pl.multiple_of(c * ck, ck)
        pen = sc_ref[pl.ds(r0, ck), :] - dmin
        vts = [vat_ref[h * HEAD_DIM_A:(h + 1) * HEAD_DIM_A, pl.ds(r0, ck)]
               for h in range(N_HEADS_A)]
        kcs = k_chunk(c_next)
        qk_pair(nxt, 0, kcs[0])
        for p in range(N_HEADS_A // 2):
            m_seen = soft_pv(cur, pen, vts, (2 * p, 2 * p + 1))
            if p + 1 < N_HEADS_A // 2:
                tie = jnp.where(m_seen != m_seen, 1.0, 0.0).astype(BF16)
                qk_pair(nxt, p + 1, kcs[p + 1] + tie)

    def soft_pv(buf, pen, vts, heads):
        m_seen = None
        for h in heads:
            rows = slice(h * HEAD_DIM_A, (h + 1) * HEAD_DIM_A)
            alphas, probs = [], []
            for lt in range(tq // 128):
                ln = slice(lt * 128, (lt + 1) * 128)
                m = m_ref[h:h + 1, ln]
                lg = buf[:, h * tq + lt * 128:h * tq + (lt + 1) * 128] - _alibi_slope(h) * pen[:, ln]
                m_new = jnp.maximum(m, col(fold(lg, jnp.max), jnp.max))
                alpha = jnp.exp(m - m_new)
                pr = jnp.exp(lg - m_new)
                m_ref[h:h + 1, ln] = m_new
                d_ref[h:h + 1, ln] = d_ref[h:h + 1, ln] * alpha + col(fold(pr, jnp.sum), jnp.sum)
                alphas.append(alpha)
                probs.append(pr.astype(BF16))
                m_seen = m_new if m_seen is None else m_seen
            ot_ref[rows, :] = (ot_ref[rows, :] * jnp.concatenate(alphas, axis=1)
                               + _dot(vts[h], jnp.concatenate(probs, axis=1)))
        return m_seen

    last = nk - 1
    def direct_one(c):
        r0 = pl.multiple_of(c * ck, ck)
        pen = sc_ref[pl.ds(r0, ck), :] - dmin
        vts = [vat_ref[h * HEAD_DIM_A:(h + 1) * HEAD_DIM_A, pl.ds(r0, ck)]
               for h in range(N_HEADS_A)]
        kcs = k_chunk(c)
        k_next = ka_ref[pl.ds(pl.multiple_of(jnp.minimum(c + 1, last) * ck, ck), ck), 0:128]
        lg2 = lg0_ref[...]
        for p in range(N_HEADS_A // 2):
            lg2_next = None
            for j in range(2):
                h = 2 * p + j
                rows = slice(h * HEAD_DIM_A, (h + 1) * HEAD_DIM_A)
                probs = []
                for lt in range(tq // 128):
                    ln = slice(lt * 128, (lt + 1) * 128)
                    e = jnp.exp(lg2[:, j * tq + lt * 128:j * tq + (lt + 1) * 128]
                                - _alibi_slope(h) * pen[:, ln])
                    esum = col(fold(e, jnp.sum), jnp.sum)
                    d_ref[h:h + 1, ln] = d_ref[h:h + 1, ln] + esum
                    probs.append(e.astype(BF16))
                    if j == 0 and lt == 0:
                        if p + 1 < N_HEADS_A // 2:
                            lg2_next = _dot(kcs[p + 1], rhsa_ref[p + 1])
                        if p + 2 == N_HEADS_A // 2:
                            lg0_ref[...] = _dot(k_next, rhsa_ref[0])
                ot_ref[rows, :] = ot_ref[rows, :] + _dot(vts[h], jnp.concatenate(probs, axis=1))
            lg2 = lg2_next

    @pl.when(direct_ref[0] > 0)
    def _():
        @pl.when(odd == 1)
        def _():
            direct_one(0)

        pair = (nk // 2) % 2

        @pl.when(pair == 1)
        def _():
            direct_one(odd)
            direct_one(odd + 1)

        def direct_quad(cc, carry):
            c0 = odd + 2 * pair + 4 * cc
            for j in range(4):
                direct_one(c0 + j)
            return carry

        lax.fori_loop(0, nk // 4, direct_quad, 0)

    @pl.when(direct_ref[0] <= 0)
    def _():
        qk_into(lga_ref, k_chunk(0))

        def attn_pair(cc, carry):
            c0 = 2 * cc
            step(lga_ref, lgb_ref, c0, jnp.minimum(c0 + 1, last))

            @pl.when(c0 + 1 < nk)
            def _():
                step(lgb_ref, lga_ref, c0 + 1, jnp.minimum(c0 + 2, last))

            return carry

        lax.fori_loop(0, (nk + 1) // 2, attn_pair, 0)

    for h in range(N_HEADS_A):
        rows = slice(h * HEAD_DIM_A, (h + 1) * HEAD_DIM_A)
        ot_ref[rows, :] = ot_ref[rows, :] / d_ref[h:h + 1, :]
    out_ref[...] = ot_ref[...].T


def _dsa_attn(qat, qit, wt, ka, vat, kib, q_norm_a, k_norm_a, batch, seq):
    tq = min(TQ, seq)
    topk = min(TOPK_MAX, seq // 4)
    qk_bound = HEAD_DIM_A ** 0.5 * jnp.max(jnp.abs(q_norm_a)) * jnp.max(jnp.abs(k_norm_a)) * 1.01
    direct = (qk_bound <= SOFTMAX_DIRECT_BOUND).astype(jnp.int32).reshape(1)
    kern = functools.partial(_dsa_attn_kernel, seq=seq, topk=topk)
    q_spec = lambda rows: pl.BlockSpec((None, rows, tq), lambda b, i: (b, 0, i))
    return pl.pallas_call(
        kern,
        grid=(batch, seq // tq),
        in_specs=[
            q_spec(WIDTH_A), q_spec(WIDTH_A), q_spec(N_IDX_HEADS),
            pl.BlockSpec((None, seq, WIDTH_A), lambda b, i: (b, 0, 0)),
            pl.BlockSpec((None, WIDTH_A, seq), lambda b, i: (b, 0, 0)),
            pl.BlockSpec((None, seq, 128), lambda b, i: (b, 0, 0)),
            pl.BlockSpec(memory_space=pltpu.SMEM),
        ],
        out_specs=pl.BlockSpec((None, tq, WIDTH_A), lambda b, i: (b, i, 0)),
        out_shape=jax.ShapeDtypeStruct((batch, seq, WIDTH_A), F32),
        scratch_shapes=[
            pltpu.VMEM((seq, tq), F32),
            pltpu.VMEM((WIDTH_A, tq), F32),
            pltpu.VMEM((N_IDX_HEADS, 128, tq), BF16),
            pltpu.VMEM((N_HEADS_A // 2, 128, 2 * tq), BF16),
            pltpu.VMEM((tq, tq), BF16),
            pltpu.VMEM((N_HEADS_A, tq), F32),
            pltpu.VMEM((N_HEADS_A, tq), F32),
            pltpu.VMEM((tq, N_HEADS_A * tq), F32),
            pltpu.VMEM((tq, N_HEADS_A * tq), F32),
            pltpu.VMEM((seq // tq, 8, tq), F32),
            pltpu.VMEM((seq // tq, 8, tq), F32),
        ],
        compiler_params=pltpu.CompilerParams(
            dimension_semantics=("arbitrary", "arbitrary"), vmem_limit_bytes=VMEM_LIMIT_BYTES),
        name="dsa_attn",
    )(qat, qit, wt, ka.reshape(batch, seq, WIDTH_A), vat, kib.reshape(batch, seq, 128), direct)


_M_BG, _M_CG, _M_HB, _M_ZB, _M_QM, _M_ZM, _M_G = (0, 512, 1024, 1536, 2048, 2560, 3072)
_M_COLS = 3072 + N_BRANCHES * D_MODEL


def _merge_kernel(x_ref, attn_ref, g_ref, wza_ref, wc_ref, bg_ref, cw_ref, mk_ref, mv_ref,
                  gqm_ref, wa_ref, wb_ref, wm_ref, wo_ref, out_ref, utail_ref, *, per_b):
    tm = x_ref.shape[0]
    xf = x_ref[...]
    xn = (_rms_rows(xf) * g_ref[...]).astype(BF16)

    def proj(lhs, c0, width):
        return _dot(lhs, wc_ref[:, c0:c0 + width])

    za = _dot(xn, wza_ref[...])
    ya = _dot((attn_ref[...] * jax.nn.silu(za)).astype(BF16), wa_ref[...])

    u = proj(xn, _M_CG, WIDTH_B) * proj(xn, _M_HB, WIDTH_B)
    @pl.when((pl.program_id(0) % per_b) == 0)
    def _():
        utail_ref[...] = jnp.zeros((8, WIDTH_B), F32)

    uh = utail_ref[...]
    utail_ref[...] = u[tm - 8:, :]
    rows = lax.broadcasted_iota(jnp.int32, (tm, WIDTH_B), 0)
    u1 = jnp.where(rows == 0, uh[7:8, :], pltpu.roll(u, 1, 0))
    u2 = jnp.where(rows == 0, uh[6:7, :], jnp.where(rows == 1, uh[7:8, :], pltpu.roll(u, 2, 0)))
    conv = cw_ref[0:1, :] * u2 + cw_ref[1:2, :] * u1 + cw_ref[2:3, :] * u
    bgate = proj(xn, _M_BG, WIDTH_B)
    zb = proj(xn, _M_ZB, WIDTH_B)
    yb = _dot(((bgate * conv) * jax.nn.silu(zb)).astype(BF16), wb_ref[...])

    qm = proj(xn, _M_QM, WIDTH_M)
    zm = proj(xn, _M_ZM, WIDTH_M)
    heads = []
    for h in range(N_HEADS_M):
        sl = slice(h * HEAD_DIM_M, (h + 1) * HEAD_DIM_M)
        qh = (_rms_rows(qm[:, sl]) * gqm_ref[...]).astype(BF16)
        lg = _dot_nt(qh, mk_ref[:, sl]) * (HEAD_DIM_M ** -0.5)
        e = jnp.exp(lg - jnp.max(lg, axis=-1, keepdims=True))
        p = e / jnp.sum(e, axis=-1, keepdims=True)
        heads.append(_dot(p.astype(BF16), mv_ref[:, sl]))
    attn_m = jnp.concatenate(heads, axis=1)
    ym = _dot((attn_m * jax.nn.silu(zm)).astype(BF16), wm_ref[...])

    def gate(j):
        gj = proj(xn, _M_G + j * D_MODEL, D_MODEL) + bg_ref[:, j * D_MODEL:(j + 1) * D_MODEL]
        return jax.nn.sigmoid(gj)

    merged = gate(0) * ya + gate(1) * yb + gate(2) * ym
    out_ref[...] = xf + _dot(merged.astype(BF16), wo_ref[...])


def _merge(x2, attn2, mk, mv, norm_g, w_in, b_gate, conv_w, q_norm_m,
           w_out_a, w_out_b, w_out_m, w_o, batch, seq):
    n = x2.shape[0]
    tm = min(TM_C, seq)
    per_b = seq // tm
    m_len = mk.shape[1]
    wza = w_in[:, _C_ZA:_C_ZA + WIDTH_A].astype(BF16)
    wc = w_in[:, _C_REST:].astype(BF16)
    const = lambda shape: pl.BlockSpec(shape, lambda i: (0,) * len(shape),
                                       pipeline_mode=pl.Buffered(1))
    kern = functools.partial(_merge_kernel, per_b=per_b)
    return pl.pallas_call(
        kern,
        grid=(n // tm,),
        in_specs=[
            pl.BlockSpec((tm, D_MODEL), lambda i: (i, 0)),
            pl.BlockSpec((tm, WIDTH_A), lambda i: (i, 0)),
            const((1, D_MODEL)), const((D_MODEL, WIDTH_A)), const((D_MODEL, _M_COLS)),
            const((1, N_BRANCHES * D_MODEL)),
            const((CONV_WIDTH, WIDTH_B)),
            pl.BlockSpec((None, m_len, WIDTH_M), lambda i: (i // per_b, 0, 0)),
            pl.BlockSpec((None, m_len, WIDTH_M), lambda i: (i // per_b, 0, 0)),
            const((1, HEAD_DIM_M)),
            const((WIDTH_A, D_MODEL)), const((WIDTH_B, D_MODEL)), const((WIDTH_M, D_MODEL)),
            const((D_MODEL, D_MODEL)),
        ],
        out_specs=pl.BlockSpec((tm, D_MODEL), lambda i: (i, 0)),
        out_shape=jax.ShapeDtypeStruct((n, D_MODEL), F32),
        scratch_shapes=[pltpu.VMEM((8, WIDTH_B), F32)],
        compiler_params=pltpu.CompilerParams(
            dimension_semantics=("arbitrary",), vmem_limit_bytes=VMEM_LIMIT_BYTES),
        name="merge",
    )(x2, attn2, norm_g.reshape(1, D_MODEL), wza, wc, b_gate.reshape(1, -1), conv_w,
      mk, mv, q_norm_m.reshape(1, HEAD_DIM_M),
      w_out_a.astype(BF16), w_out_b.astype(BF16), w_out_m.astype(BF16), w_o.astype(BF16))


def _layer(h, mem, norm_g, mem_norm_g, w_in, b_gate, w_mem_kv, q_norm_a, k_norm_a,
           q_norm_m, k_norm_m, conv_w, w_out_a, w_out_b, w_out_m, w_o):
    batch, seq, _ = h.shape
    x2 = h.reshape(batch * seq, D_MODEL)
    qat, ka, vat, qit, kib, wt = _proj_a(x2, norm_g, w_in, q_norm_a, k_norm_a, batch, seq)
    mk, mv = _mem_kv(mem, mem_norm_g, w_mem_kv, k_norm_m)
    attn = _dsa_attn(qat, qit, wt, ka, vat, kib, q_norm_a, k_norm_a, batch, seq)
    out = _merge(x2, attn.reshape(batch * seq, WIDTH_A), mk, mv, norm_g, w_in, b_gate, conv_w,
                 q_norm_m, w_out_a, w_out_b, w_out_m, w_o, batch, seq)
    return out.reshape(batch, seq, D_MODEL)


def kernel(x, mem, norm_g, mem_norm_g, w_in, b_gate, w_mem_kv, q_norm_a, k_norm_a,
           q_norm_m, k_norm_m, conv_w, w_out_a, w_out_b, w_out_m, w_o):
    h = x
    for l in range(norm_g.shape[0]):
        h = _layer(h, mem, norm_g[l], mem_norm_g[l], w_in[l], b_gate[l], w_mem_kv[l],
                   q_norm_a[l], k_norm_a[l], q_norm_m[l], k_norm_m[l], conv_w[l],
                   w_out_a[l], w_out_b[l], w_out_m[l], w_o[l])
    return h
```
